```python
import math
import jax, jax.numpy as jnp
from jax import lax
import numpy as np

D_MODEL = 2048
BATCH = 4
SEQ = 4096
DEPTH = 1

GRID_W = 64
CTX_LEN = 256
SSD_HEADS = 32
SSD_HEAD_DIM = 64
SSD_INNER = SSD_HEADS * SSD_HEAD_DIM
SSD_GROUPS = 4
SSD_HPG = SSD_HEADS // SSD_GROUPS
SSD_STATE = 128
SSD_CONV = 3
CHUNK = 128
SSD_XBC = SSD_INNER + 2 * SSD_GROUPS * SSD_STATE
SC_DIM = 2048
SC_CONV = 3
N_EXPERTS = 16
EXPERT_FF = 2048
CAPACITY_FACTOR = 2
EPS = 1e-6
PROJ_OFFSETS = (
    SSD_INNER,
    SSD_INNER + SSD_XBC,
    SSD_INNER + SSD_XBC + 2 * SSD_HEADS,
    SSD_INNER + SSD_XBC + 2 * SSD_HEADS + SC_DIM,
    SSD_INNER + SSD_XBC + 2 * SSD_HEADS + 2 * SC_DIM,
    SSD_INNER + SSD_XBC + 2 * SSD_HEADS + 3 * SC_DIM,
)
PROJ_COLS = SSD_INNER + SSD_XBC + 2 * SSD_HEADS + 3 * SC_DIM + 2 * D_MODEL

kernel_name = "hybrid_ssd_shortconv_ecmoe_dit_block"


def rmsnorm(x, w):
    x32 = x.astype(jnp.float32)
    y = x32 * lax.rsqrt(jnp.mean(x32 * x32, axis=-1, keepdims=True) + EPS)
    return (y * w.astype(jnp.float32)).astype(x.dtype)


def rev(a):
    return jnp.flip(a, axis=1)


def dwconv(x, w, b=None):
    k_w = w.shape[0]
    pad = k_w // 2
    length = x.shape[1]
    xp = jnp.pad(x, ((0, 0), (pad, pad), (0, 0)))
    y = sum(xp[:, k:k + length] * w[k] for k in range(k_w))
    return y if b is None else y + b


def seq_conv(x, w, b, grid):
    if grid:
        bsz, length, ch = x.shape
        rows = length // GRID_W
        return dwconv(x.reshape(bsz * rows, GRID_W, ch), w, b).reshape(bsz, length, ch)
    return dwconv(x, w, b)


def ssd_prep(xbc, dt_raw, p, grid):
    bsz, length, _ = xbc.shape
    xbc = jax.nn.silu(seq_conv(xbc, p['ssd_conv_w'], p['ssd_conv_b'], grid))
    xs, bm, cm = jnp.split(xbc, [SSD_INNER, SSD_INNER + SSD_GROUPS * SSD_STATE], axis=-1)
    xs = xs.reshape(bsz, length, SSD_GROUPS, SSD_HPG, SSD_HEAD_DIM)
    bm = bm.reshape(bsz, length, SSD_GROUPS, SSD_STATE)
    cm = cm.reshape(bsz, length, SSD_GROUPS, SSD_STATE)
    dt = jax.nn.softplus(dt_raw.astype(jnp.float32).reshape(bsz, length, 2, SSD_HEADS)
                         + p['dt_bias'].astype(jnp.float32))
    dt = dt.reshape(bsz, length, 2, SSD_GROUPS, SSD_HPG)
    a = -jnp.exp(p['a_log'].astype(jnp.float32)).reshape(2, SSD_GROUPS, SSD_HPG)
    return xs, bm, cm, dt, a


def ssd_chunk_states(xs, dt, a, bm, h0):
    bsz, length = xs.shape[:2]
    nc = length // CHUNK
    xc = xs.reshape(bsz, nc, CHUNK, SSD_GROUPS, SSD_HPG, SSD_HEAD_DIM)
    dtc = dt.reshape(bsz, nc, CHUNK, SSD_GROUPS, SSD_HPG)
    bc = bm.reshape(bsz, nc, CHUNK, SSD_GROUPS, SSD_STATE)
    acs = jnp.cumsum(dtc * a, axis=2)
    decay_to_end = jnp.exp(acs[:, :, -1:] - acs)
    s = jnp.einsum('bcjgr,bcjgn,bcjgrp->bcgrpn', decay_to_end * dtc, bc, xc)
    chunk_decay = jnp.exp(acs[:, :, -1])

    def step(h, inp):
        dec, s_c = inp
        return dec[..., None, None] * h + s_c, h

    h_final, h_enter = lax.scan(step, h0,
                                (jnp.moveaxis(chunk_decay, 1, 0), jnp.moveaxis(s, 1, 0)))
    return acs, jnp.moveaxis(h_enter, 0, 1), h_final


def ssd_scan(xs, dt, a, bm, cm, h0):
    bsz, length = xs.shape[:2]
    nc = length // CHUNK
    acs, h_enter, h_final = ssd_chunk_states(xs, dt, a, bm, h0)
    xc = xs.reshape(bsz, nc, CHUNK, SSD_GROUPS, SSD_HPG, SSD_HEAD_DIM)
    dtc = dt.reshape(bsz, nc, CHUNK, SSD_GROUPS, SSD_HPG)
    bc = bm.reshape(bsz, nc, CHUNK, SSD_GROUPS, SSD_STATE)
    cc = cm.reshape(bsz, nc, CHUNK, SSD_GROUPS, SSD_STATE)
    cb = jnp.einsum('bcign,bcjgn->bcgij', cc, bc)
    at = jnp.moveaxis(acs, 2, -1)
    seg = at[..., :, None] - at[..., None, :]
    mask = jnp.tril(jnp.ones((CHUNK, CHUNK), dtype=bool))
    lmat = jnp.exp(jnp.where(mask, seg, -jnp.inf))
    m = cb[:, :, :, None] * lmat * jnp.moveaxis(dtc, 2, -1)[..., None, :]
    y_diag = jnp.einsum('bcgrij,bcjgrp->bcigrp', m, xc)
    y_off = jnp.einsum('bcign,bcgrpn->bcigrp', cc, h_enter) * jnp.exp(acs)[..., None]
    y = (y_diag + y_off).reshape(bsz, length, SSD_GROUPS, SSD_HPG, SSD_HEAD_DIM)
    return y, h_final


def token_mixer(h, states, p, grid):
    bsz, length, _ = h.shape
    proj = h @ p['w_in']
    z, xbc, dt_raw, sc_b, sc_c, sc_v, g_raw = jnp.split(proj, PROJ_OFFSETS, axis=-1)
    xs, bm, cm, dt, a = ssd_prep(xbc, dt_raw, p, grid)
    h0_f, h0_b = states
    y_f, hf_f = ssd_scan(xs, dt[:, :, 0], a[0], bm, cm, h0_f)
    y_b, hf_b = ssd_scan(rev(xs), rev(dt[:, :, 1]), a[1], rev(bm), rev(cm), h0_b)
    y = y_f + rev(y_b) + p['ssd_d'].reshape(SSD_GROUPS, SSD_HPG)[..., None] * xs
    y = y.reshape(bsz, length, SSD_INNER) * jax.nn.silu(z)
    y = rmsnorm(y.reshape(bsz, length, SSD_GROUPS, -1),
                p['ssd_norm_w'].reshape(SSD_GROUPS, -1)).reshape(bsz, length, SSD_INNER)
    ssd_out = y.astype(h.dtype) @ p['w_ssd_out']
    sc_y = sc_b * seq_conv(sc_c * sc_v, p['sc_conv_w'], None, grid)
    sc_out = sc_y @ p['w_sc_out']
    g_ssd, g_sc = jnp.split(jax.nn.sigmoid(g_raw + p['b_gate']), 2, axis=-1)
    merged = g_ssd * ssd_out + g_sc * sc_out
    return (merged @ p['w_o']).astype(h.dtype), (hf_f, hf_b)


def ctx_final_states(hc, p):
    cols = hc @ p['w_in'][:, SSD_INNER:SSD_INNER + SSD_XBC + 2 * SSD_HEADS]
    xbc, dt_raw = jnp.split(cols, [SSD_XBC], axis=-1)
    xs, bm, _, dt, a = ssd_prep(xbc, dt_raw, p, False)
    h0 = jnp.zeros((hc.shape[0], SSD_GROUPS, SSD_HPG, SSD_HEAD_DIM, SSD_STATE), jnp.float32)
    hf_f = ssd_chunk_states(xs, dt[:, :, 0], a[0], bm, h0)[2]
    hf_b = ssd_chunk_states(rev(xs), rev(dt[:, :, 1]), a[1], rev(bm), h0)[2]
    return (hf_f, hf_b)


def expert_choice_ffn(h, p):
    bsz, n, d = h.shape
    cap = CAPACITY_FACTOR * n // N_EXPERTS
    aff = jax.nn.softmax((h @ p['w_router']).astype(jnp.float32), axis=-1)
    vals, idx = lax.top_k(jnp.swapaxes(aff, 1, 2), cap)
    xg = jax.vmap(lambda hb, ib: hb[ib])(h, idx)
    g = jnp.einsum('becd,edf->becf', xg, p['w_e1'])
    u = jnp.einsum('becd,edf->becf', xg, p['w_e3'])
    out = jnp.einsum('becf,efd->becd', jax.nn.silu(g) * u, p['w_e2'])
    out = out * vals[..., None].astype(out.dtype)
    y = jax.vmap(lambda ib, ob: jnp.zeros((n, d), ob.dtype).at[ib.reshape(-1)].add(ob.reshape(-1, d)))(idx, out)
    return y.astype(h.dtype)


def modulate(h, shift, scale):
    return h * (1 + scale) + shift


def layer(x, ctx, c, c_ctx, p, last):
    mod_x = (jax.nn.silu(c) @ p['w_mod'] + p['b_mod'])[:, None, :]
    mod_c = jax.nn.silu(c_ctx) @ p['w_mod'] + p['b_mod']
    shx, scx, gx, shfx, scfx, gfx = jnp.split(mod_x, 6, axis=-1)
    shc, scc, gc, shfc, scfc, gfc = jnp.split(mod_c, 6, axis=-1)
    hc = modulate(rmsnorm(ctx, p['norm_pre_mix']), shc, scc)
    if last:
        states = ctx_final_states(hc, p)
    else:
        h0 = jnp.zeros((ctx.shape[0], SSD_GROUPS, SSD_HPG, SSD_HEAD_DIM, SSD_STATE), jnp.float32)
        out_c, states = token_mixer(hc, (h0, h0), p, False)
        ctx = ctx + gc * rmsnorm(out_c, p['norm_post_mix'])
        hc2 = modulate(rmsnorm(ctx, p['norm_pre_ffn']), shfc, scfc)
        ctx = ctx + gfc * rmsnorm(expert_choice_ffn(hc2, p), p['norm_post_ffn'])
    hx = modulate(rmsnorm(x, p['norm_pre_mix']), shx, scx)
    out_x, _ = token_mixer(hx, states, p, True)
    x = x + gx * rmsnorm(out_x, p['norm_post_mix'])
    hx2 = modulate(rmsnorm(x, p['norm_pre_ffn']), shfx, scfx)
    x = x + gfx * rmsnorm(expert_choice_ffn(hx2, p), p['norm_post_ffn'])
    return x, ctx


def setup_inputs(seed: int = 0) -> dict:
    key = jax.random.key(seed)
    ks = jax.random.split(key, 32)
    f32 = jnp.float32

    def nrm(k, shape, scale):
        return jax.random.normal(k, shape, f32) * scale

    def gain(k, shape):
        return 1.0 + 0.05 * jax.random.normal(k, shape, f32)

    u = jax.random.uniform(ks[10], (DEPTH, 2, SSD_HEADS), f32)
    dt0 = jnp.exp(u * (math.log(0.1) - math.log(0.001)) + math.log(0.001))
    dt_bias = dt0 + jnp.log(-jnp.expm1(-dt0))
    a_log = jnp.log(jax.random.uniform(ks[11], (DEPTH, 2, SSD_HEADS), f32, minval=1.0, maxval=16.0))
    return {
        'x': nrm(ks[0], (BATCH, SEQ, D_MODEL), 1.0),
        'c': nrm(ks[1], (BATCH, D_MODEL), 1.0),
        'ctx': nrm(ks[2], (BATCH, CTX_LEN, D_MODEL), 1.0),
        'c_ctx': nrm(ks[3], (D_MODEL,), 1.0),
        'w_mod': nrm(ks[4], (DEPTH, D_MODEL, 6 * D_MODEL), 0.3 * D_MODEL ** -0.5),
        'b_mod': nrm(ks[5], (DEPTH, 6 * D_MODEL), 0.02),
        'norm_pre_mix': gain(ks[6], (DEPTH, D_MODEL)),
        'norm_post_mix': gain(ks[7], (DEPTH, D_MODEL)),
        'w_in': nrm(ks[8], (DEPTH, D_MODEL, PROJ_COLS), D_MODEL ** -0.5),
        'ssd_conv_w': nrm(ks[9], (DEPTH, SSD_CONV, SSD_XBC), SSD_CONV ** -0.5),
        'ssd_conv_b': nrm(ks[12], (DEPTH, SSD_XBC), 0.02),
        'dt_bias': dt_bias,
        'a_log': a_log,
        'ssd_d': gain(ks[13], (DEPTH, SSD_HEADS)),
        'ssd_norm_w': gain(ks[14], (DEPTH, SSD_INNER)),
        'w_ssd_out': nrm(ks[15], (DEPTH, SSD_INNER, D_MODEL), SSD_INNER ** -0.5),
        'sc_conv_w': nrm(ks[16], (DEPTH, SC_CONV, SC_DIM), SC_CONV ** -0.5),
        'w_sc_out': nrm(ks[17], (DEPTH, SC_DIM, D_MODEL), SC_DIM ** -0.5),
        'b_gate': nrm(ks[18], (DEPTH, 2 * D_MODEL), 0.02),
        'w_o': nrm(ks[19], (DEPTH, D_MODEL, D_MODEL), D_MODEL ** -0.5),
        'norm_pre_ffn': gain(ks[20], (DEPTH, D_MODEL)),
        'norm_post_ffn': gain(ks[21], (DEPTH, D_MODEL)),
        'w_router': nrm(ks[22], (DEPTH, D_MODEL, N_EXPERTS), D_MODEL ** -0.5),
        'w_e1': nrm(ks[23], (DEPTH, N_EXPERTS, D_MODEL, EXPERT_FF), D_MODEL ** -0.5),
        'w_e3': nrm(ks[24], (DEPTH, N_EXPERTS, D_MODEL, EXPERT_FF), D_MODEL ** -0.5),
        'w_e2': nrm(ks[25], (DEPTH, N_EXPERTS, EXPERT_FF, D_MODEL), EXPERT_FF ** -0.5),
    }


def reference(x, c, ctx, c_ctx, w_mod, b_mod, norm_pre_mix, norm_post_mix, w_in, ssd_conv_w,
              ssd_conv_b, dt_bias, a_log, ssd_d, ssd_norm_w, w_ssd_out, sc_conv_w, w_sc_out,
              b_gate, w_o, norm_pre_ffn, norm_post_ffn, w_router, w_e1, w_e3, w_e2):
    for l in range(DEPTH):
        p = {
            'w_mod': w_mod[l], 'b_mod': b_mod[l],
            'norm_pre_mix': norm_pre_mix[l], 'norm_post_mix': norm_post_mix[l],
            'w_in': w_in[l], 'ssd_conv_w': ssd_conv_w[l], 'ssd_conv_b': ssd_conv_b[l],
            'dt_bias': dt_bias[l], 'a_log': a_log[l], 'ssd_d': ssd_d[l],
            'ssd_norm_w': ssd_norm_w[l], 'w_ssd_out': w_ssd_out[l],
            'sc_conv_w': sc_conv_w[l], 'w_sc_out': w_sc_out[l],
            'b_gate': b_gate[l], 'w_o': w_o[l],
            'norm_pre_ffn': norm_pre_ffn[l], 'norm_post_ffn': norm_post_ffn[l],
            'w_router': w_router[l], 'w_e1': w_e1[l], 'w_e3': w_e3[l], 'w_e2': w_e2[l],
        }
        x, ctx = layer(x, ctx, c, c_ctx, p, l == DEPTH - 1)
    return x
```

```python
import functools
import math

import jax
import jax.numpy as jnp
from jax import lax
from jax.experimental import pallas as pl
from jax.experimental.pallas import tpu as pltpu

F32 = jnp.float32
BF16 = jnp.bfloat16

D_MODEL = 2048
GRID_W = 64
SSD_HEADS = 32
SSD_HEAD_DIM = 64
SSD_INNER = SSD_HEADS * SSD_HEAD_DIM
SSD_GROUPS = 4
SSD_HPG = SSD_HEADS // SSD_GROUPS
SSD_STATE = 128
CHUNK = 128
SSD_XBC = SSD_INNER + 2 * SSD_GROUPS * SSD_STATE
SC_DIM = 2048
N_EXPERTS = 16
EXPERT_FF = 2048
CAPACITY_FACTOR = 2
EPS = 1e-6
GROUP_W = SSD_HPG * SSD_HEAD_DIM

LANES = 128
V7X_VMEM_BYTES = 64 * 1024 * 1024
VMEM_LIMIT = V7X_VMEM_BYTES - 8 * 1024 * 1024


def _params(n_grid):
    return pltpu.CompilerParams(dimension_semantics=("arbitrary",) * n_grid,
                                vmem_limit_bytes=VMEM_LIMIT)


def _silu(x):
    return x * (1.0 / (1.0 + jnp.exp(-x)))


def _sigmoid(x):
    return 1.0 / (1.0 + jnp.exp(-x))


def _softplus(x):
    return jnp.maximum(x, 0.0) + jnp.log(1.0 + jnp.exp(-jnp.abs(x)))


def _rms(x):
    return x * lax.rsqrt(jnp.mean(x * x, axis=-1, keepdims=True) + EPS)


def _mod_kernel(c_ref, w_ref, b_ref, o_ref):
    a = _silu(c_ref[...]).astype(BF16)
    o_ref[...] = jnp.dot(a, w_ref[...].astype(BF16), preferred_element_type=F32) + b_ref[...]


def _modulation(cs, w_mod, b_mod):
    rows, d = cs.shape
    n = w_mod.shape[1]
    tn = 1024
    return pl.pallas_call(
        _mod_kernel,
        grid=(n // tn,),
        in_specs=[pl.BlockSpec((rows, d), lambda j: (0, 0)),
                  pl.BlockSpec((d, tn), lambda j: (0, j)),
                  pl.BlockSpec((1, tn), lambda j: (0, j))],
        out_specs=pl.BlockSpec((rows, tn), lambda j: (0, j)),
        out_shape=jax.ShapeDtypeStruct((rows, n), F32),
        compiler_params=_params(1),
        name="modulation",
    )(cs, w_mod, b_mod.reshape(1, n))


def _norm_mod_kernel(x_ref, w_ref, sc_ref, sh_ref, o_ref):
    y = _rms(x_ref[...]) * w_ref[...]
    o_ref[...] = (y * (1.0 + sc_ref[...]) + sh_ref[...]).astype(o_ref.dtype)


def _norm_mod(x, w, mod3, sh_row, sc_row, tm):
    bsz, length, d = x.shape
    return pl.pallas_call(
        _norm_mod_kernel,
        grid=(bsz, length // tm),
        in_specs=[pl.BlockSpec((None, tm, d), lambda b, i: (b, i, 0)),
                  pl.BlockSpec((1, d), lambda b, i: (0, 0)),
                  pl.BlockSpec((None, 1, d), lambda b, i: (sc_row(b), 0, 0)),
                  pl.BlockSpec((None, 1, d), lambda b, i: (sh_row(b), 0, 0))],
        out_specs=pl.BlockSpec((None, tm, d), lambda b, i: (b, i, 0)),
        out_shape=jax.ShapeDtypeStruct((bsz, length, d), BF16),
        compiler_params=_params(2),
        name="norm_mod",
    )(x, w.reshape(1, d), mod3, mod3)


def _fused_matmul(a_list, w_list, w_to_a, col_list, tile_list, epilogue, out_dtypes, n, tm, tn, name):
    m, k = a_list[0].shape
    na, nw, nc, nt = len(a_list), len(w_list), len(col_list), len(tile_list)

    def body(*refs):
        a_refs = refs[:na]
        w_refs = refs[na:na + nw]
        c_refs = refs[na + nw:na + nw + nc]
        t_refs = refs[na + nw + nc:na + nw + nc + nt]
        o_refs = refs[na + nw + nc + nt:]
        prods = [jnp.dot(a_refs[w_to_a[i]][...], w_refs[i][...], preferred_element_type=F32)
                 for i in range(nw)]
        outs = epilogue(prods, [c[...] for c in c_refs], [t[...] for t in t_refs])
        for o_ref, o in zip(o_refs, outs):
            o_ref[...] = o.astype(o_ref.dtype)

    in_specs = [pl.BlockSpec((tm, k), lambda i, j: (i, 0)) for _ in a_list]
    args = list(a_list)
    for w, off in w_list:
        in_specs.append(pl.BlockSpec((k, tn), functools.partial(lambda i, j, off: (0, j + off), off=off)))
        args.append(w)
    for c in col_list:
        in_specs.append(pl.BlockSpec((c.shape[0], tn), lambda i, j: (0, j)))
        args.append(c)
    for t, off in tile_list:
        in_specs.append(pl.BlockSpec((tm, tn), functools.partial(lambda i, j, off: (i, j + off), off=off)))
        args.append(t)
    return pl.pallas_call(
        body,
        grid=(m // tm, n // tn),
        in_specs=in_specs,
        out_specs=[pl.BlockSpec((tm, tn), lambda i, j: (i, j)) for _ in out_dtypes],
        out_shape=[jax.ShapeDtypeStruct((m, n), dt) for dt in out_dtypes],
        compiler_params=_params(2),
        name=name,
    )(*args)


def _conv3_rows(x, cw, seg):
    tm = x.shape[0]
    r = lax.broadcasted_iota(jnp.int32, x.shape, 0) & (seg - 1)
    xp = jnp.where(r == 0, 0.0, pltpu.roll(x, 1, 0))
    xn = jnp.where(r == seg - 1, 0.0, pltpu.roll(x, tm - 1, 0))
    return xp * cw[0:1, :] + x * cw[1:2, :] + xn * cw[2:3, :]


def _epi_identity(prods, cols, tiles):
    return [prods[0]]


def _epi_conv_silu(seg, prods, cols, tiles):
    return [_silu(_conv3_rows(prods[0], cols[0], seg) + cols[1])]


def _epi_softplus(prods, cols, tiles):
    return [_softplus(prods[0] + cols[0])]


def _epi_shortconv(seg, prods, cols, tiles):
    sc_b, sc_c, sc_v = prods
    return [sc_b * _conv3_rows(sc_c * sc_v, cols[0], seg)]


def _epi_sigmoid(prods, cols, tiles):
    return [_sigmoid(prods[0] + cols[0])]


def _epi_gate_merge(prods, cols, tiles):
    return [tiles[0] * prods[0] + tiles[1] * prods[1]]


def _split3(x):
    hi = x.astype(BF16)
    r1 = x - hi.astype(F32)
    mid = r1.astype(BF16)
    lo = (r1 - mid.astype(F32)).astype(BF16)
    return hi, mid, lo


def _cumsum_rows(tri, x):
    return sum(jnp.dot(tri, p, preferred_element_type=F32) for p in _split3(x))


def _cumsum_cols(x, tri):
    return sum(jnp.dot(p, tri, preferred_element_type=F32) for p in _split3(x))


def _pair(lo_mask, v, k):
    return jnp.where(lo_mask, v[:, 2 * k:2 * k + 1], v[:, 2 * k + 1:2 * k + 2])


def _ssd_kernel(xs_ref, b_ref, c_ref, dtc_ref, dtr_ref, arow_ref, acol_ref, tri_ref, trit_ref, h0_ref,
                y_ref, hfin_ref, st_ref):
    q = CHUNK
    step = pl.program_id(3)

    @pl.when(step == 0)
    def _():
        st_ref[...] = h0_ref[...]

    tri = tri_ref[...]
    dtc = dtc_ref[...]
    dtr = dtr_ref[...]
    al_c = dtc * arow_ref[...]
    al_r = dtr * acol_ref[...]
    cs_c = _cumsum_rows(tri, al_c)
    cs_r = _cumsum_cols(al_r, trit_ref[...])
    tot_c = jnp.sum(al_c, axis=0, keepdims=True)
    mask = tri.astype(F32) > 0.5
    bm = b_ref[...]
    cm = c_ref[...]
    cb = lax.dot_general(cm, bm, (((1,), (1,)), ((), ())), preferred_element_type=F32)
    lo_q = lax.broadcasted_iota(jnp.int32, (q, LANES), 1) < SSD_HEAD_DIM
    lo_1 = lax.broadcasted_iota(jnp.int32, (1, LANES), 1) < SSD_HEAD_DIM
    ht = st_ref[...]
    y_off = jnp.dot(cm, ht.astype(BF16), preferred_element_type=F32)
    e_c = jnp.exp(cs_c)
    w_c = jnp.exp(tot_c - cs_c) * dtc
    xw = []
    dec = []
    for k in range(SSD_HPG // 2):
        ms = []
        for r in (2 * k, 2 * k + 1):
            seg = cs_c[:, r:r + 1] - cs_r[r:r + 1, :]
            lmat = jnp.exp(jnp.where(mask, seg, -1e30))
            ms.append((cb * lmat * dtr[r:r + 1, :]).astype(BF16))
        lhs = jnp.concatenate(ms, axis=1)
        xp = xs_ref[:, k * LANES:(k + 1) * LANES]
        rhs = jnp.concatenate([jnp.where(lo_q, xp, 0.0).astype(BF16),
                               jnp.where(lo_q, 0.0, xp).astype(BF16)], axis=0)
        y_diag = jnp.dot(lhs, rhs, preferred_element_type=F32)
        y_ref[:, k * LANES:(k + 1) * LANES] = (
            y_diag + y_off[:, k * LANES:(k + 1) * LANES] * _pair(lo_q, e_c, k))
        xw.append((xp * _pair(lo_q, w_c, k)).astype(BF16))
        dec.append(jnp.exp(_pair(lo_1, tot_c, k)))
    xw = jnp.concatenate(xw, axis=1)
    s_t = lax.dot_general(bm, xw, (((0,), (0,)), ((), ())), preferred_element_type=F32)
    st_ref[...] = ht * jnp.concatenate(dec, axis=1) + s_t

    @pl.when(step == pl.num_programs(3) - 1)
    def _():
        hfin_ref[...] = st_ref[...]


def _ssd_scan(xs, bc, dt, a_neg, tri, h0):
    bsz, length, _ = xs.shape
    nc = length // CHUNK
    g, r, q, n = SSD_GROUPS, SSD_HPG, CHUNK, SSD_STATE
    dt5 = dt[:, :, :2 * SSD_HEADS].reshape(bsz, length, 2, g, r)
    dt_col = jnp.transpose(dt5, (2, 3, 0, 1, 4))
    dt_row = jnp.transpose(dt5, (2, 3, 0, 4, 1))
    a_row = a_neg.reshape(2, g, 1, r)
    a_col = a_neg.reshape(2, g, r, 1)

    def chunk(d, c):
        return c + d * (nc - 1 - 2 * c)

    return pl.pallas_call(
        _ssd_kernel,
        grid=(2, bsz, g, nc),
        in_specs=[
            pl.BlockSpec((None, q, GROUP_W), lambda d, b, gi, c: (b, chunk(d, c), gi)),
            pl.BlockSpec((None, q, n), lambda d, b, gi, c: (b, chunk(d, c), gi)),
            pl.BlockSpec((None, q, n), lambda d, b, gi, c: (b, chunk(d, c), g + gi)),
            pl.BlockSpec((None, None, None, q, r), lambda d, b, gi, c: (d, gi, b, chunk(d, c), 0)),
            pl.BlockSpec((None, None, None, r, q), lambda d, b, gi, c: (d, gi, b, 0, chunk(d, c))),
            pl.BlockSpec((None, None, 1, r), lambda d, b, gi, c: (d, gi, 0, 0)),
            pl.BlockSpec((None, None, r, 1), lambda d, b, gi, c: (d, gi, 0, 0)),
            pl.BlockSpec((None, q, q), lambda d, b, gi, c: (d, 0, 0)),
            pl.BlockSpec((None, q, q), lambda d, b, gi, c: (1 - d, 0, 0)),
            pl.BlockSpec((None, None, None, n, GROUP_W), lambda d, b, gi, c: (d, b, gi, 0, 0)),
        ],
        out_specs=[
            pl.BlockSpec((None, None, q, GROUP_W), lambda d, b, gi, c: (d, b, chunk(d, c), gi)),
            pl.BlockSpec((None, None, None, n, GROUP_W), lambda d, b, gi, c: (d, b, gi, 0, 0)),
        ],
        out_shape=[jax.ShapeDtypeStruct((2, bsz, length, SSD_INNER), F32),
                   jax.ShapeDtypeStruct((2, bsz, g, n, GROUP_W), F32)],
        scratch_shapes=[pltpu.VMEM((n, GROUP_W), F32)],
        compiler_params=_params(4),
        name="ssd_scan",
    )(xs, bc, bc, dt_col, dt_row, a_row, a_col, tri, tri, h0)


def _ssd_post_kernel(yf_ref, yb_ref, xs_ref, z_ref, d_ref, w_ref, o_ref):
    y = (yf_ref[...] + yb_ref[...] + d_ref[...] * xs_ref[...]) * _silu(z_ref[...])
    for g in range(SSD_GROUPS):
        sl = slice(g * GROUP_W, (g + 1) * GROUP_W)
        o_ref[:, sl] = (_rms(y[:, sl]) * w_ref[:, sl]).astype(o_ref.dtype)


def _ssd_post(y2, xs, z, d_cols, norm_w, tm):
    _, bsz, length, d = y2.shape
    nt = length // tm
    return pl.pallas_call(
        _ssd_post_kernel,
        grid=(bsz, nt),
        in_specs=[pl.BlockSpec((None, None, tm, d), lambda b, i: (0, b, i, 0)),
                  pl.BlockSpec((None, None, tm, d), lambda b, i: (1, b, i, 0)),
                  pl.BlockSpec((tm, d), lambda b, i: (b * nt + i, 0)),
                  pl.BlockSpec((tm, d), lambda b, i: (b * nt + i, 0)),
                  pl.BlockSpec((1, d), lambda b, i: (0, 0)),
                  pl.BlockSpec((1, d), lambda b, i: (0, 0))],
        out_specs=pl.BlockSpec((tm, d), lambda b, i: (b * nt + i, 0)),
        out_shape=jax.ShapeDtypeStruct((bsz * length, d), BF16),
        compiler_params=_params(2),
        name="ssd_post",
    )(y2, y2, xs, z, d_cols, norm_w)


def _mix_out_kernel(m_ref, wo_ref, x_ref, gx_ref, scf_ref, shf_ref, npost_ref, npre_ref, wr_ref,
                    x1_ref, h2_ref, aff_ref):
    out = jnp.dot(m_ref[...], wo_ref[...], preferred_element_type=F32)
    x1 = x_ref[...] + gx_ref[...] * (_rms(out) * npost_ref[...])
    x1_ref[...] = x1
    h2 = ((_rms(x1) * npre_ref[...]) * (1.0 + scf_ref[...]) + shf_ref[...]).astype(BF16)
    h2_ref[...] = h2
    logits = jnp.dot(h2, wr_ref[...], preferred_element_type=F32)
    lane = lax.broadcasted_iota(jnp.int32, logits.shape, 1)
    logits = jnp.where(lane < N_EXPERTS, logits, -1e30)
    e = jnp.exp(logits - jnp.max(logits, axis=-1, keepdims=True))
    aff_ref[...] = e / jnp.sum(e, axis=-1, keepdims=True)


def _mix_out(merged, w_o, x2d, mod3, npost, npre, w_router_pad, length, tm):
    m, d = merged.shape
    nt = length // tm
    row = lambda kind: (lambda i: ((i // nt) * 6 + kind, 0, 0))
    return pl.pallas_call(
        _mix_out_kernel,
        grid=(m // tm,),
        in_specs=[pl.BlockSpec((tm, d), lambda i: (i, 0)),
                  pl.BlockSpec((d, d), lambda i: (0, 0)),
                  pl.BlockSpec((tm, d), lambda i: (i, 0)),
                  pl.BlockSpec((None, 1, d), row(2)),
                  pl.BlockSpec((None, 1, d), row(4)),
                  pl.BlockSpec((None, 1, d), row(3)),
                  pl.BlockSpec((1, d), lambda i: (0, 0)),
                  pl.BlockSpec((1, d), lambda i: (0, 0)),
                  pl.BlockSpec((d, LANES), lambda i: (0, 0))],
        out_specs=[pl.BlockSpec((tm, d), lambda i: (i, 0)),
                   pl.BlockSpec((tm, d), lambda i: (i, 0)),
                   pl.BlockSpec((tm, LANES), lambda i: (i, 0))],
        out_shape=[jax.ShapeDtypeStruct((m, d), F32),
                   jax.ShapeDtypeStruct((m, d), BF16),
                   jax.ShapeDtypeStruct((m, LANES), F32)],
        compiler_params=_params(1),
        name="mix_out_router",
    )(merged, w_o, x2d, mod3, mod3, mod3, npost, npre, w_router_pad)


def _route_kernel(cap, aff_ref, triu_ref, pos_ref):
    aff = aff_ref[...]
    e_cnt, n_tok = aff.shape

    def count(m):
        return jnp.sum(jnp.where(m, 1.0, 0.0), axis=1, keepdims=True)

    def search_bits(_, carry):
        lo, hi = carry
        mid = lo + ((hi - lo + 1) >> 1)
        ok = count(aff >= pltpu.bitcast(mid, F32)) >= cap
        return jnp.where(ok, mid, lo), jnp.where(ok, hi, mid - 1)

    lo0 = jnp.zeros((e_cnt, 1), jnp.int32)
    hi0 = jnp.full((e_cnt, 1), 0x7F800000, jnp.int32)
    lo_b, _ = lax.fori_loop(0, 32, search_bits, (lo0, hi0))

    def search_mid(_, carry):
        lo, hi = carry
        mid = 0.5 * (lo + hi)
        ok = count(aff >= mid) >= cap
        return jnp.where(ok, mid, lo), jnp.where(ok, hi, mid)

    lo_f, _ = lax.fori_loop(0, 24, search_mid, (pltpu.bitcast(lo_b, F32), pltpu.bitcast(lo_b + 1, F32)))
    thr = jnp.min(jnp.where(aff >= lo_f, aff, jnp.inf), axis=1, keepdims=True)
    need_f = cap - count(aff > thr)
    triu = triu_ref[...]
    carry_eq = jnp.zeros((e_cnt, 1), F32)
    carry_sel = jnp.zeros((e_cnt, 1), F32)
    for k in range(n_tok // LANES):
        sl = slice(k * LANES, (k + 1) * LANES)
        aff_k = aff[:, sl]
        eq_f = jnp.where(aff_k == thr, 1.0, 0.0)
        eq_incl = jnp.dot(eq_f.astype(BF16), triu, preferred_element_type=F32) + carry_eq
        sel_k = (aff_k > thr) | ((aff_k == thr) & (eq_incl <= need_f))
        sel_f = jnp.where(sel_k, 1.0, 0.0)
        sel_incl = jnp.dot(sel_f.astype(BF16), triu, preferred_element_type=F32) + carry_sel
        pos_ref[:, sl] = jnp.where(sel_k, (sel_incl - sel_f).astype(jnp.int32), -1)
        carry_eq = eq_incl[:, LANES - 1:LANES]
        carry_sel = sel_incl[:, LANES - 1:LANES]


def _route(aff_t, triu, cap):
    bsz, e_cnt, n_tok = aff_t.shape
    return pl.pallas_call(
        functools.partial(_route_kernel, cap),
        grid=(bsz,),
        in_specs=[pl.BlockSpec((None, e_cnt, n_tok), lambda b: (b, 0, 0)),
                  pl.BlockSpec((LANES, LANES), lambda b: (0, 0))],
        out_specs=pl.BlockSpec((None, e_cnt, n_tok), lambda b: (b, 0, 0)),
        out_shape=jax.ShapeDtypeStruct((bsz, e_cnt, n_tok), jnp.int32),
        compiler_params=_params(1),
        name="route_topk",
    )(aff_t, triu)


def _gather_kernel(pos_ref, aff_ref, h_ref, xg_ref, val_ref, acc_ref, vacc_ref):
    t = pl.program_id(2)

    @pl.when(t == 0)
    def _():
        acc_ref[...] = jnp.zeros_like(acc_ref)
        vacc_ref[...] = jnp.zeros_like(vacc_ref)

    cap = acc_ref.shape[0]
    pos = pos_ref[...]
    hit = lax.broadcasted_iota(jnp.int32, (cap, pos.shape[1]), 0) == pos
    acc_ref[...] += jnp.dot(jnp.where(hit, 1.0, 0.0).astype(BF16), h_ref[...], preferred_element_type=F32)
    vacc_ref[...] += jnp.sum(jnp.where(hit, aff_ref[...], 0.0), axis=1, keepdims=True)

    @pl.when(t == pl.num_programs(2) - 1)
    def _():
        xg_ref[...] = acc_ref[...].astype(xg_ref.dtype)
        val_ref[...] = jnp.broadcast_to(vacc_ref[...], val_ref.shape)


def _gather(pos4, aff4, h2, cap, tt):
    bsz, e_cnt, _, n_tok = pos4.shape
    d = h2.shape[-1]
    return pl.pallas_call(
        _gather_kernel,
        grid=(bsz, e_cnt, n_tok // tt),
        in_specs=[pl.BlockSpec((None, None, 1, tt), lambda b, e, t: (b, e, 0, t)),
                  pl.BlockSpec((None, None, 1, tt), lambda b, e, t: (b, e, 0, t)),
                  pl.BlockSpec((None, tt, d), lambda b, e, t: (b, t, 0))],
        out_specs=[pl.BlockSpec((None, None, cap, d), lambda b, e, t: (b, e, 0, 0)),
                   pl.BlockSpec((None, None, cap, LANES), lambda b, e, t: (b, e, 0, 0))],
        out_shape=[jax.ShapeDtypeStruct((bsz, e_cnt, cap, d), BF16),
                   jax.ShapeDtypeStruct((bsz, e_cnt, cap, LANES), F32)],
        scratch_shapes=[pltpu.VMEM((cap, d), F32), pltpu.VMEM((cap, 1), F32)],
        compiler_params=_params(3),
        name="moe_gather",
    )(pos4, aff4, h2)


def _expert_kernel(xg_ref, w1_ref, w3_ref, w2_ref, val_ref, o_ref, acc_ref):
    f = pl.program_id(2)

    @pl.when(f == 0)
    def _():
        acc_ref[...] = jnp.zeros_like(acc_ref)

    xg = xg_ref[...]
    g = jnp.dot(xg, w1_ref[...], preferred_element_type=F32)
    u = jnp.dot(xg, w3_ref[...], preferred_element_type=F32)
    acc_ref[...] += jnp.dot((_silu(g) * u).astype(BF16), w2_ref[...], preferred_element_type=F32)

    @pl.when(f == pl.num_programs(2) - 1)
    def _():
        o_ref[...] = (acc_ref[...] * val_ref[:, 0:1]).astype(o_ref.dtype)


def _experts(xg, w1, w3, w2, vals, tf):
    bsz, e_cnt, cap, d = xg.shape
    ff = w1.shape[-1]
    return pl.pallas_call(
        _expert_kernel,
        grid=(bsz, e_cnt, ff // tf),
        in_specs=[pl.BlockSpec((None, None, cap, d), lambda b, e, f: (b, e, 0, 0)),
                  pl.BlockSpec((None, d, tf), lambda b, e, f: (e, 0, f)),
                  pl.BlockSpec((None, d, tf), lambda b, e, f: (e, 0, f)),
                  pl.BlockSpec((None, tf, d), lambda b, e, f: (e, f, 0)),
                  pl.BlockSpec((None, None, cap, LANES), lambda b, e, f: (b, e, 0, 0))],
        out_specs=pl.BlockSpec((None, None, cap, d), lambda b, e, f: (b, e, 0, 0)),
        out_shape=jax.ShapeDtypeStruct((bsz, e_cnt, cap, d), BF16),
        scratch_shapes=[pltpu.VMEM((cap, d), F32)],
        compiler_params=_params(3),
        name="moe_experts",
    )(xg, w1, w3, w2, vals)


def _combine_kernel(pos_ref, out_ref, x1_ref, g_ref, w_ref, o_ref, acc_ref):
    e = pl.program_id(2)

    @pl.when(e == 0)
    def _():
        acc_ref[...] = jnp.zeros_like(acc_ref)

    pos = pos_ref[...].astype(F32)
    lane = lax.broadcasted_iota(jnp.int32, pos.shape, 1)
    col = jnp.sum(jnp.where(lane == e, pos, 0.0), axis=1, keepdims=True)
    cap = out_ref.shape[0]
    hit = lax.broadcasted_iota(jnp.int32, (pos.shape[0], cap), 1).astype(F32) == col
    acc_ref[...] += jnp.dot(jnp.where(hit, 1.0, 0.0).astype(BF16), out_ref[...], preferred_element_type=F32)

    @pl.when(e == pl.num_programs(2) - 1)
    def _():
        o_ref[...] = x1_ref[...] + g_ref[...] * (_rms(acc_ref[...]) * w_ref[...])


def _combine(pos_col, eout, x1, mod3, npost_ffn, tt):
    bsz, n_tok, e_cnt = pos_col.shape
    cap, d = eout.shape[2:]
    return pl.pallas_call(
        _combine_kernel,
        grid=(bsz, n_tok // tt, e_cnt),
        in_specs=[pl.BlockSpec((None, tt, e_cnt), lambda b, t, e: (b, t, 0)),
                  pl.BlockSpec((None, None, cap, d), lambda b, t, e: (b, e, 0, 0)),
                  pl.BlockSpec((None, tt, d), lambda b, t, e: (b, t, 0)),
                  pl.BlockSpec((None, 1, d), lambda b, t, e: (b * 6 + 5, 0, 0)),
                  pl.BlockSpec((1, d), lambda b, t, e: (0, 0))],
        out_specs=pl.BlockSpec((None, tt, d), lambda b, t, e: (b, t, 0)),
        out_shape=jax.ShapeDtypeStruct((bsz, n_tok, d), F32),
        scratch_shapes=[pltpu.VMEM((tt, d), F32)],
        compiler_params=_params(3),
        name="moe_combine",
    )(pos_col, eout, x1, mod3, npost_ffn)


def _ssd_inputs(h, w_xs, w_bc, w_dt, conv_w, conv_b, dt_bias_pad, seg, tm):
    cw_x, cw_bc = conv_w[:, :SSD_INNER], conv_w[:, SSD_INNER:]
    cb_x, cb_bc = conv_b[:, :SSD_INNER], conv_b[:, SSD_INNER:]
    epi = functools.partial(_epi_conv_silu, seg)
    (xs,) = _fused_matmul([h], [(w_xs, 0)], [0], [cw_x, cb_x], [], epi, [F32], SSD_INNER, tm, 1024, "proj_xs")
    (bc,) = _fused_matmul([h], [(w_bc, 0)], [0], [cw_bc, cb_bc], [], epi, [BF16],
                          2 * SSD_GROUPS * SSD_STATE, tm, 1024, "proj_bc")
    (dt,) = _fused_matmul([h], [(w_dt, 0)], [0], [dt_bias_pad], [], _epi_softplus, [F32], LANES, tm, LANES,
                          "proj_dt")
    return xs, bc, dt


def kernel(x, c, ctx, c_ctx, w_mod, b_mod, norm_pre_mix, norm_post_mix, w_in, ssd_conv_w, ssd_conv_b,
           dt_bias, a_log, ssd_d, ssd_norm_w, w_ssd_out, sc_conv_w, w_sc_out, b_gate, w_o, norm_pre_ffn,
           norm_post_ffn, w_router, w_e1, w_e3, w_e2):
    assert w_mod.shape[0] == 1, "single-layer block"
    bsz, length, d = x.shape
    ctx_len = ctx.shape[1]
    m = bsz * length
    cap = CAPACITY_FACTOR * length // N_EXPERTS
    row = lambda v: v.reshape(1, -1)

    o_z, o_xbc, o_dt, o_scb, o_scc, o_scv = (SSD_INNER, SSD_INNER + SSD_XBC,
                                             SSD_INNER + SSD_XBC + 2 * SSD_HEADS,
                                             SSD_INNER + SSD_XBC + 2 * SSD_HEADS + SC_DIM,
                                             SSD_INNER + SSD_XBC + 2 * SSD_HEADS + 2 * SC_DIM,
                                             SSD_INNER + SSD_XBC + 2 * SSD_HEADS + 3 * SC_DIM)
    w_in0 = w_in[0]
    w_z = w_in0[:, :o_z].astype(BF16)
    w_xs = w_in0[:, o_z:o_z + SSD_INNER].astype(BF16)
    w_bc = w_in0[:, o_z + SSD_INNER:o_xbc].astype(BF16)
    w_dt = jnp.pad(w_in0[:, o_xbc:o_dt], ((0, 0), (0, LANES - 2 * SSD_HEADS))).astype(BF16)
    w_scb = w_in0[:, o_dt:o_scb].astype(BF16)
    w_scc = w_in0[:, o_scb:o_scc].astype(BF16)
    w_scv = w_in0[:, o_scc:o_scv].astype(BF16)
    w_gate = w_in0[:, o_scv:].astype(BF16)
    dt_bias_pad = jnp.pad(dt_bias[0].reshape(1, -1), ((0, 0), (0, LANES - 2 * SSD_HEADS)))
    a_neg = -jnp.exp(a_log[0])
    ii = lax.broadcasted_iota(jnp.int32, (CHUNK, CHUNK), 0)
    jj = lax.broadcasted_iota(jnp.int32, (CHUNK, CHUNK), 1)
    tri = jnp.stack([ii >= jj, ii <= jj]).astype(BF16)
    conv_w, conv_b = ssd_conv_w[0], row(ssd_conv_b[0])

    cs = jnp.concatenate([c, c_ctx[None, :], jnp.zeros((8 - bsz - 1, d), F32)], axis=0)
    mod3 = _modulation(cs, w_mod[0], b_mod[0]).reshape(8 * 6, 1, d)

    hc = _norm_mod(ctx, norm_pre_mix[0], mod3, lambda b: bsz * 6, lambda b: bsz * 6 + 1, ctx_len)
    xs_c, bc_c, dt_c = _ssd_inputs(hc.reshape(bsz * ctx_len, d), w_xs, w_bc, w_dt, conv_w, conv_b,
                                   dt_bias_pad, ctx_len, ctx_len)
    h_zero = jnp.zeros((2, bsz, SSD_GROUPS, SSD_STATE, GROUP_W), F32)
    _, states = _ssd_scan(xs_c.reshape(bsz, ctx_len, -1), bc_c.reshape(bsz, ctx_len, -1),
                          dt_c.reshape(bsz, ctx_len, -1), a_neg, tri, h_zero)

    tm = 1024
    hx = _norm_mod(x, norm_pre_mix[0], mod3, lambda b: b * 6, lambda b: b * 6 + 1, tm).reshape(m, d)
    (z,) = _fused_matmul([hx], [(w_z, 0)], [0], [], [], _epi_identity, [F32], SSD_INNER, tm, 1024, "proj_z")
    xs, bc, dt = _ssd_inputs(hx, w_xs, w_bc, w_dt, conv_w, conv_b, dt_bias_pad, GRID_W, tm)
    (sc_y,) = _fused_matmul([hx], [(w_scb, 0), (w_scc, 0), (w_scv, 0)], [0, 0, 0], [sc_conv_w[0]], [],
                            functools.partial(_epi_shortconv, GRID_W), [BF16], SC_DIM, tm, 512, "proj_sc")
    (gates,) = _fused_matmul([hx], [(w_gate, 0)], [0], [row(b_gate[0])], [], _epi_sigmoid, [F32],
                             2 * d, tm, 1024, "proj_gate")
    y2, _ = _ssd_scan(xs.reshape(bsz, length, -1), bc.reshape(bsz, length, -1), dt.reshape(bsz, length, -1),
                      a_neg, tri, states)
    d_cols = jnp.repeat(ssd_d[0], SSD_HEAD_DIM).reshape(1, -1)
    y_norm = _ssd_post(y2, xs, z, d_cols, row(ssd_norm_w[0]), 256)
    tn = 1024
    (merged,) = _fused_matmul([y_norm, sc_y], [(w_ssd_out[0].astype(BF16), 0), (w_sc_out[0].astype(BF16), 0)],
                              [0, 1], [], [(gates, 0), (gates, d // tn)], _epi_gate_merge, [BF16], d, 512, tn,
                              "mix_merge")
    w_router_pad = jnp.pad(w_router[0], ((0, 0), (0, LANES - N_EXPERTS))).astype(BF16)
    x1, h2, aff = _mix_out(merged, w_o[0].astype(BF16), x.reshape(m, d), mod3, row(norm_post_mix[0]),
                           row(norm_pre_ffn[0]), w_router_pad, length, 256)

    aff_t = jnp.transpose(aff[:, :N_EXPERTS].reshape(bsz, length, N_EXPERTS), (0, 2, 1))
    pos = _route(aff_t, tri[1], cap)
    xg, vals = _gather(pos[:, :, None, :], aff_t[:, :, None, :], h2.reshape(bsz, length, d), cap, 512)
    eout = _experts(xg, w_e1[0].astype(BF16), w_e3[0].astype(BF16), w_e2[0].astype(BF16), vals, 512)
    return _combine(jnp.transpose(pos, (0, 2, 1)), eout, x1.reshape(bsz, length, d), mod3,
                    row(norm_post_ffn[0]), 512)
```

```python
import functools
import math

import jax
import jax.numpy as jnp
from jax import lax
from jax.experimental import pallas as pl
from jax.experimental.pallas import tpu as pltpu

F32 = jnp.float32
BF16 = jnp.bfloat16

D_MODEL = 2048
GRID_W = 64
SSD_HEADS = 32
SSD_HEAD_DIM = 64
SSD_INNER = SSD_HEADS * SSD_HEAD_DIM
SSD_GROUPS = 4
SSD_HPG = SSD_HEADS // SSD_GROUPS
SSD_STATE = 128
CHUNK = 128
SSD_XBC = SSD_INNER + 2 * SSD_GROUPS * SSD_STATE
SC_DIM = 2048
N_EXPERTS = 16
EXPERT_FF = 2048
CAPACITY_FACTOR = 2
EPS = 1e-6
GROUP_W = SSD_HPG * SSD_HEAD_DIM

LANES = 128
V7X_VMEM_BYTES = 64 * 1024 * 1024
VMEM_LIMIT = V7X_VMEM_BYTES - 8 * 1024 * 1024


def _params(n_grid):
    return pltpu.CompilerParams(dimension_semantics=("arbitrary",) * n_grid,
                                vmem_limit_bytes=VMEM_LIMIT)


def _silu(x):
    return x * (1.0 / (1.0 + jnp.exp(-x)))


def _sigmoid(x):
    return 1.0 / (1.0 + jnp.exp(-x))


def _softplus(x):
    return jnp.maximum(x, 0.0) + jnp.log(1.0 + jnp.exp(-jnp.abs(x)))


def _rms(x):
    return x * lax.rsqrt(jnp.mean(x * x, axis=-1, keepdims=True) + EPS)


def _mod_kernel(c_ref, w_ref, b_ref, o_ref):
    a = _silu(c_ref[...]).astype(BF16)
    o_ref[...] = jnp.dot(a, w_ref[...].astype(BF16), preferred_element_type=F32) + b_ref[...]


def _modulation(cs, w_mod, b_mod):
    rows, d = cs.shape
    n = w_mod.shape[1]
    tn = 1024
    return pl.pallas_call(
        _mod_kernel,
        grid=(n // tn,),
        in_specs=[pl.BlockSpec((rows, d), lambda j: (0, 0)),
                  pl.BlockSpec((d, tn), lambda j: (0, j)),
                  pl.BlockSpec((1, tn), lambda j: (0, j))],
        out_specs=pl.BlockSpec((rows, tn), lambda j: (0, j)),
        out_shape=jax.ShapeDtypeStruct((rows, n), F32),
        compiler_params=_params(1),
        name="modulation",
    )(cs, w_mod, b_mod.reshape(1, n))


def _norm_mod_kernel(x_ref, w_ref, sc_ref, sh_ref, o_ref):
    y = _rms(x_ref[...]) * w_ref[...]
    o_ref[...] = (y * (1.0 + sc_ref[...]) + sh_ref[...]).astype(o_ref.dtype)


def _norm_mod(x, w, mod3, sh_row, sc_row, tm):
    bsz, length, d = x.shape
    return pl.pallas_call(
        _norm_mod_kernel,
        grid=(bsz, length // tm),
        in_specs=[pl.BlockSpec((None, tm, d), lambda b, i: (b, i, 0)),
                  pl.BlockSpec((1, d), lambda b, i: (0, 0)),
                  pl.BlockSpec((None, 1, d), lambda b, i: (sc_row(b), 0, 0)),
                  pl.BlockSpec((None, 1, d), lambda b, i: (sh_row(b), 0, 0))],
        out_specs=pl.BlockSpec((None, tm, d), lambda b, i: (b, i, 0)),
        out_shape=jax.ShapeDtypeStruct((bsz, length, d), BF16),
        compiler_params=_params(2),
        name="norm_mod",
    )(x, w.reshape(1, d), mod3, mod3)


def _fused_matmul(a_list, w_list, w_to_a, col_list, tile_list, epilogue, out_dtypes, n, tm, tn, name):
    m, k = a_list[0].shape
    na, nw, nc, nt = len(a_list), len(w_list), len(col_list), len(tile_list)

    def body(*refs):
        a_refs = refs[:na]
        w_refs = refs[na:na + nw]
        c_refs = refs[na + nw:na + nw + nc]
        t_refs = refs[na + nw + nc:na + nw + nc + nt]
        o_refs = refs[na + nw + nc + nt:]
        prods = [jnp.dot(a_refs[w_to_a[i]][...], w_refs[i][...], preferred_element_type=F32)
                 for i in range(nw)]
        outs = epilogue(prods, [c[...] for c in c_refs], [t[...] for t in t_refs])
        for o_ref, o in zip(o_refs, outs):
            o_ref[...] = o.astype(o_ref.dtype)

    in_specs = [pl.BlockSpec((tm, k), lambda i, j: (i, 0)) for _ in a_list]
    args = list(a_list)
    for w, off in w_list:
        in_specs.append(pl.BlockSpec((k, tn), functools.partial(lambda i, j, off: (0, j + off), off=off)))
        args.append(w)
    for c in col_list:
        in_specs.append(pl.BlockSpec((c.shape[0], tn), lambda i, j: (0, j)))
        args.append(c)
    for t, off in tile_list:
        in_specs.append(pl.BlockSpec((tm, tn), functools.partial(lambda i, j, off: (i, j + off), off=off)))
        args.append(t)
    return pl.pallas_call(
        body,
        grid=(m // tm, n // tn),
        in_specs=in_specs,
        out_specs=[pl.BlockSpec((tm, tn), lambda i, j: (i, j)) for _ in out_dtypes],
        out_shape=[jax.ShapeDtypeStruct((m, n), dt) for dt in out_dtypes],
        compiler_params=_params(2),
        name=name,
    )(*args)


def _conv3_rows(x, cw, seg):
    tm = x.shape[0]
    r = lax.broadcasted_iota(jnp.int32, x.shape, 0) & (seg - 1)
    xp = jnp.where(r == 0, 0.0, pltpu.roll(x, 1, 0))
    xn = jnp.where(r == seg - 1, 0.0, pltpu.roll(x, tm - 1, 0))
    return xp * cw[0:1, :] + x * cw[1:2, :] + xn * cw[2:3, :]


def _epi_identity(prods, cols, tiles):
    return [prods[0]]


def _epi_conv_silu(seg, prods, cols, tiles):
    return [_silu(_conv3_rows(prods[0], cols[0], seg) + cols[1])]


def _epi_softplus(prods, cols, tiles):
    return [_softplus(prods[0] + cols[0])]


def _epi_shortconv(seg, prods, cols, tiles):
    sc_b, sc_c, sc_v = prods
    return [sc_b * _conv3_rows(sc_c * sc_v, cols[0], seg)]


def _epi_sigmoid(prods, cols, tiles):
    return [_sigmoid(prods[0] + cols[0])]


def _epi_gate_merge(prods, cols, tiles):
    return [tiles[0] * prods[0] + tiles[1] * prods[1]]


def _split3(x):
    hi = x.astype(BF16)
    r1 = x - hi.astype(F32)
    mid = r1.astype(BF16)
    lo = (r1 - mid.astype(F32)).astype(BF16)
    return hi, mid, lo


def _cumsum_rows(tri, x):
    return sum(jnp.dot(tri, p, preferred_element_type=F32) for p in _split3(x))


def _cumsum_cols(x, tri):
    return sum(jnp.dot(p, tri, preferred_element_type=F32) for p in _split3(x))


def _pair(lo_mask, v, h):
    return jnp.where(lo_mask, v[:, h:h + 1], v[:, h + 1:h + 2])


def _ssd_kernel(xs_ref, bc_ref, dtc_ref, dtr_ref, arow_ref, acol_ref, tri_ref, trit_ref, h0_ref,
                y_ref, hfin_ref, st_ref):
    q, n = CHUNK, SSD_STATE
    step = pl.program_id(2)

    @pl.when(step == 0)
    def _():
        st_ref[...] = h0_ref[...]

    tri = tri_ref[...]
    dtc = dtc_ref[...]
    dtr = dtr_ref[...]
    al_c = dtc * arow_ref[...]
    al_r = dtr * acol_ref[...]
    cs_c = _cumsum_rows(tri, al_c)
    cs_r = _cumsum_cols(al_r, trit_ref[...])
    tot_c = jnp.sum(al_c, axis=0, keepdims=True)
    e_c = jnp.exp(cs_c)
    w_c = jnp.exp(tot_c - cs_c) * dtc
    mask = tri.astype(F32) > 0.5
    lo_q = lax.broadcasted_iota(jnp.int32, (q, LANES), 1) < SSD_HEAD_DIM
    lo_1 = lax.broadcasted_iota(jnp.int32, (1, LANES), 1) < SSD_HEAD_DIM
    for gi in range(SSD_GROUPS):
        bm = bc_ref[:, gi * n:(gi + 1) * n]
        cm = bc_ref[:, (SSD_GROUPS + gi) * n:(SSD_GROUPS + gi + 1) * n]
        cb = lax.dot_general(cm, bm, (((1,), (1,)), ((), ())), preferred_element_type=F32)
        ht = st_ref[gi]
        y_off = jnp.dot(cm, ht.astype(BF16), preferred_element_type=F32)
        xw = []
        dec = []
        for k in range(SSD_HPG // 2):
            h0 = gi * SSD_HPG + 2 * k
            col = gi * GROUP_W + k * LANES
            ms = []
            for h in (h0, h0 + 1):
                seg = cs_c[:, h:h + 1] - cs_r[h:h + 1, :]
                lmat = jnp.exp(jnp.where(mask, seg, -1e30))
                ms.append((cb * lmat * dtr[h:h + 1, :]).astype(BF16))
            lhs = jnp.concatenate(ms, axis=1)
            xp = xs_ref[:, col:col + LANES]
            rhs = jnp.concatenate([jnp.where(lo_q, xp, 0.0).astype(BF16),
                                   jnp.where(lo_q, 0.0, xp).astype(BF16)], axis=0)
            y_diag = jnp.dot(lhs, rhs, preferred_element_type=F32)
            y_ref[:, col:col + LANES] = y_diag + y_off[:, k * LANES:(k + 1) * LANES] * _pair(lo_q, e_c, h0)
            xw.append((xp * _pair(lo_q, w_c, h0)).astype(BF16))
            dec.append(jnp.exp(_pair(lo_1, tot_c, h0)))
        xw = jnp.concatenate(xw, axis=1)
        s_t = lax.dot_general(bm, xw, (((0,), (0,)), ((), ())), preferred_element_type=F32)
        st_ref[gi] = ht * jnp.concatenate(dec, axis=1) + s_t

    @pl.when(step == pl.num_programs(2) - 1)
    def _():
        hfin_ref[...] = st_ref[...]


def _ssd_scan(xs, bc, dt, a_neg, tri, h0):
    bsz, length, _ = xs.shape
    nc = length // CHUNK
    g, q, n, nh = SSD_GROUPS, CHUNK, SSD_STATE, SSD_HEADS
    dt4 = dt[:, :, :2 * nh].reshape(bsz, length, 2, nh)
    dt_col = jnp.transpose(dt4, (2, 0, 1, 3))
    dt_row = jnp.transpose(dt4, (2, 0, 3, 1))
    a_row = a_neg.reshape(2, 1, nh)
    a_col = a_neg.reshape(2, nh, 1)

    def chunk(d, c):
        return c + d * (nc - 1 - 2 * c)

    return pl.pallas_call(
        _ssd_kernel,
        grid=(2, bsz, nc),
        in_specs=[
            pl.BlockSpec((None, q, SSD_INNER), lambda d, b, c: (b, chunk(d, c), 0)),
            pl.BlockSpec((None, q, 2 * g * n), lambda d, b, c: (b, chunk(d, c), 0)),
            pl.BlockSpec((None, None, q, nh), lambda d, b, c: (d, b, chunk(d, c), 0)),
            pl.BlockSpec((None, None, nh, q), lambda d, b, c: (d, b, 0, chunk(d, c))),
            pl.BlockSpec((None, 1, nh), lambda d, b, c: (d, 0, 0)),
            pl.BlockSpec((None, nh, 1), lambda d, b, c: (d, 0, 0)),
            pl.BlockSpec((None, q, q), lambda d, b, c: (d, 0, 0)),
            pl.BlockSpec((None, q, q), lambda d, b, c: (1 - d, 0, 0)),
            pl.BlockSpec((None, None, g, n, GROUP_W), lambda d, b, c: (d, b, 0, 0, 0)),
        ],
        out_specs=[
            pl.BlockSpec((None, None, q, SSD_INNER), lambda d, b, c: (d, b, chunk(d, c), 0)),
            pl.BlockSpec((None, None, g, n, GROUP_W), lambda d, b, c: (d, b, 0, 0, 0)),
        ],
        out_shape=[jax.ShapeDtypeStruct((2, bsz, length, SSD_INNER), F32),
                   jax.ShapeDtypeStruct((2, bsz, g, n, GROUP_W), F32)],
        scratch_shapes=[pltpu.VMEM((g, n, GROUP_W), F32)],
        compiler_params=_params(3),
        name="ssd_scan",
    )(xs, bc, dt_col, dt_row, a_row, a_col, tri, tri, h0)


def _ssd_post_kernel(yf_ref, yb_ref, xs_ref, z_ref, d_ref, w_ref, o_ref):
    y = (yf_ref[...] + yb_ref[...] + d_ref[...] * xs_ref[...]) * _silu(z_ref[...])
    for g in range(SSD_GROUPS):
        sl = slice(g * GROUP_W, (g + 1) * GROUP_W)
        o_ref[:, sl] = (_rms(y[:, sl]) * w_ref[:, sl]).astype(o_ref.dtype)


def _ssd_post(y2, xs, z, d_cols, norm_w, tm):
    _, bsz, length, d = y2.shape
    nt = length // tm
    return pl.pallas_call(
        _ssd_post_kernel,
        grid=(bsz, nt),
        in_specs=[pl.BlockSpec((None, None, tm, d), lambda b, i: (0, b, i, 0)),
                  pl.BlockSpec((None, None, tm, d), lambda b, i: (1, b, i, 0)),
                  pl.BlockSpec((tm, d), lambda b, i: (b * nt + i, 0)),
                  pl.BlockSpec((tm, d), lambda b, i: (b * nt + i, 0)),
                  pl.BlockSpec((1, d), lambda b, i: (0, 0)),
                  pl.BlockSpec((1, d), lambda b, i: (0, 0))],
        out_specs=pl.BlockSpec((tm, d), lambda b, i: (b * nt + i, 0)),
        out_shape=jax.ShapeDtypeStruct((bsz * length, d), BF16),
        compiler_params=_params(2),
        name="ssd_post",
    )(y2, y2, xs, z, d_cols, norm_w)


def _mix_out_kernel(m_ref, wo_ref, x_ref, gx_ref, scf_ref, shf_ref, npost_ref, npre_ref, wr_ref,
                    x1_ref, h2_ref, aff_ref):
    out = jnp.dot(m_ref[...], wo_ref[...], preferred_element_type=F32)
    x1 = x_ref[...] + gx_ref[...] * (_rms(out) * npost_ref[...])
    x1_ref[...] = x1
    h2 = (_rms(x1) * npre_ref[...]) * (1.0 + scf_ref[...]) + shf_ref[...]
    h2_ref[...] = h2
    logits = jnp.dot(h2.astype(BF16), wr_ref[...], preferred_element_type=F32)
    lane = lax.broadcasted_iota(jnp.int32, logits.shape, 1)
    logits = jnp.where(lane < N_EXPERTS, logits, -1e30)
    e = jnp.exp(logits - jnp.max(logits, axis=-1, keepdims=True))
    aff_ref[...] = e / jnp.sum(e, axis=-1, keepdims=True)


def _mix_out(merged, w_o, x2d, mod3, npost, npre, w_router_pad, length, tm):
    m, d = merged.shape
    nt = length // tm
    row = lambda kind: (lambda i: ((i // nt) * 6 + kind, 0, 0))
    return pl.pallas_call(
        _mix_out_kernel,
        grid=(m // tm,),
        in_specs=[pl.BlockSpec((tm, d), lambda i: (i, 0)),
                  pl.BlockSpec((d, d), lambda i: (0, 0)),
                  pl.BlockSpec((tm, d), lambda i: (i, 0)),
                  pl.BlockSpec((None, 1, d), row(2)),
                  pl.BlockSpec((None, 1, d), row(4)),
                  pl.BlockSpec((None, 1, d), row(3)),
                  pl.BlockSpec((1, d), lambda i: (0, 0)),
                  pl.BlockSpec((1, d), lambda i: (0, 0)),
                  pl.BlockSpec((d, LANES), lambda i: (0, 0))],
        out_specs=[pl.BlockSpec((tm, d), lambda i: (i, 0)),
                   pl.BlockSpec((tm, d), lambda i: (i, 0)),
                   pl.BlockSpec((tm, LANES), lambda i: (i, 0))],
        out_shape=[jax.ShapeDtypeStruct((m, d), F32),
                   jax.ShapeDtypeStruct((m, d), F32),
                   jax.ShapeDtypeStruct((m, LANES), F32)],
        compiler_params=_params(1),
        name="mix_out_router",
    )(merged, w_o, x2d, mod3, mod3, mod3, npost, npre, w_router_pad)


def _route_select(cap, aff_ref, triu_ref, low_ref, pos_ref, slot_ref, ts_ref):
    aff = aff_ref[...]
    e_cnt, n_tok = aff.shape

    def count(m):
        return jnp.sum(jnp.where(m, 1.0, 0.0), axis=1, keepdims=True)

    def search_bits(_, carry):
        lo, hi = carry
        mid = lo + ((hi - lo + 1) >> 1)
        ok = count(aff >= pltpu.bitcast(mid, F32)) >= cap
        return jnp.where(ok, mid, lo), jnp.where(ok, hi, mid - 1)

    lo0 = jnp.zeros((e_cnt, 1), jnp.int32)
    hi0 = jnp.full((e_cnt, 1), 0x7F800000, jnp.int32)
    lo_b, _ = lax.fori_loop(0, 32, search_bits, (lo0, hi0))

    def search_mid(_, carry):
        lo, hi = carry
        mid = 0.5 * (lo + hi)
        ok = count(aff >= mid) >= cap
        return jnp.where(ok, mid, lo), jnp.where(ok, hi, mid)

    lo_f, _ = lax.fori_loop(0, 24, search_mid, (pltpu.bitcast(lo_b, F32), pltpu.bitcast(lo_b + 1, F32)))
    thr = jnp.min(jnp.where(aff >= lo_f, aff, jnp.inf), axis=1, keepdims=True)
    need_f = cap - count(aff > thr)
    triu = triu_ref[...]
    carry_eq = jnp.zeros((e_cnt, 1), F32)
    carry_sel = jnp.zeros((e_cnt, 1), F32)
    for k in range(n_tok // LANES):
        sl = slice(k * LANES, (k + 1) * LANES)
        aff_k = aff[:, sl]
        eq_f = jnp.where(aff_k == thr, 1.0, 0.0)
        eq_incl = jnp.dot(eq_f.astype(BF16), triu, preferred_element_type=F32) + carry_eq
        sel_k = (aff_k > thr) | ((aff_k == thr) & (eq_incl <= need_f))
        sel_f = jnp.where(sel_k, 1.0, 0.0)
        sel_b = sel_f.astype(BF16)
        sel_incl = jnp.dot(sel_b, triu, preferred_element_type=F32) + carry_sel
        excl = sel_incl - sel_f
        pos_ref[:, sl] = jnp.where(sel_k, excl.astype(jnp.int32), -1)
        tok_start = jnp.sum(excl, axis=0, keepdims=True)
        ts_ref[:, sl] = tok_start
        slot_ref[:, sl] = tok_start + jnp.dot(low_ref[...], sel_b, preferred_element_type=F32)
        carry_eq = eq_incl[:, LANES - 1:LANES]
        carry_sel = sel_incl[:, LANES - 1:LANES]


def _route_kernel(cap, aff_ref, triu_ref, low_ref, list_ref, ts_ref, pos_ref, slot_ref):
    e = pl.program_id(1)

    @pl.when(e == 0)
    def _():
        _route_select(cap, aff_ref, triu_ref, low_ref, pos_ref, slot_ref, ts_ref)

    n_tok = aff_ref.shape[1]
    pos = pos_ref[pl.ds(e, 1), :]
    slot = slot_ref[pl.ds(e, 1), :]
    aff = aff_ref[pl.ds(e, 1), :]
    hit = lax.broadcasted_iota(jnp.int32, (cap, n_tok), 0) == pos
    tok = lax.broadcasted_iota(jnp.int32, (1, n_tok), 1)
    slot_hi = jnp.floor(slot * (1.0 / LANES))
    a_hi = aff.astype(BF16).astype(F32)
    a_mid = (aff - a_hi).astype(BF16).astype(F32)
    rows = [(tok >> 6).astype(F32), (tok & 63).astype(F32), slot_hi, slot - slot_hi * LANES,
            a_hi, a_mid, aff - a_hi - a_mid]
    sub = lax.broadcasted_iota(jnp.int32, (8, n_tok), 0)
    vals = jnp.zeros((8, n_tok), F32)
    for i, r in enumerate(rows):
        vals = jnp.where(sub == i, r, vals)
    list_ref[...] = lax.dot_general(jnp.where(hit, 1.0, 0.0).astype(BF16), vals.astype(BF16),
                                    (((1,), (1,)), ((), ())), preferred_element_type=F32)


def _route(aff_t, triu, low, cap):
    bsz, e_cnt, n_tok = aff_t.shape
    return pl.pallas_call(
        functools.partial(_route_kernel, cap),
        grid=(bsz, e_cnt),
        in_specs=[pl.BlockSpec((None, e_cnt, n_tok), lambda b, e: (b, 0, 0)),
                  pl.BlockSpec((LANES, LANES), lambda b, e: (0, 0)),
                  pl.BlockSpec((e_cnt, e_cnt), lambda b, e: (0, 0))],
        out_specs=[pl.BlockSpec((None, None, cap, 8), lambda b, e: (b, e, 0, 0)),
                   pl.BlockSpec((None, 1, n_tok), lambda b, e: (b, 0, 0))],
        out_shape=[jax.ShapeDtypeStruct((bsz, e_cnt, cap, 8), F32),
                   jax.ShapeDtypeStruct((bsz, 1, n_tok), F32)],
        scratch_shapes=[pltpu.VMEM((e_cnt, n_tok), jnp.int32), pltpu.VMEM((e_cnt, n_tok), F32)],
        compiler_params=_params(2),
        name="route_topk",
    )(aff_t, triu, low)


MXU_COLS = 256


def _row_copy(src, src_row, dst, dst_row, sem):
    return pltpu.make_async_copy(src.at[pl.ds(src_row, 1), :], dst.at[pl.ds(dst_row, 1), :], sem)


def _expert_kernel(cap, n_e, n_f, n_tiles, idx_ref, dst_ref, h_hbm, w1_ref, w3_ref, w2_ref, val_ref, z_hbm,
                   gbuf, obuf, xg_ref, acc_ref, gsem, ssem):
    f = pl.program_id(2)
    i = pl.program_id(0) * n_e + pl.program_id(1)
    slot = lax.rem(i, 2)
    other = 1 - slot
    nxt = jnp.minimum(i + 1, n_tiles - 1)
    tf, d = w2_ref.shape
    rows = cap // n_f

    def wait_all(buf, sem, s):
        pltpu.make_async_copy(buf.at[s], buf.at[s], sem.at[s]).wait()

    @pl.when(jnp.logical_and(i == 0, f == 0))
    def _():
        obuf[1] = jnp.zeros((cap, d), F32)

        def issue(r, carry):
            _row_copy(h_hbm, idx_ref[0, r], gbuf.at[0], r, gsem.at[0]).start()
            return carry

        lax.fori_loop(0, cap, issue, 0)

    @pl.when(f == 0)
    def _():
        wait_all(gbuf, gsem, slot)
        xg_ref[...] = gbuf[slot].astype(BF16)
        acc_ref[...] = jnp.zeros_like(acc_ref)

    n_groups = tf // MXU_COLS + d // MXU_COLS

    def issue_rows(k):
        for r in range((rows * k) // n_groups, (rows * (k + 1)) // n_groups):
            row = f * rows + r
            _row_copy(h_hbm, idx_ref[nxt, row], gbuf.at[other], row, gsem.at[other]).start()
            _row_copy(obuf.at[other], row, z_hbm, dst_ref[i, row], ssem.at[other]).start()

    xg = xg_ref[...]
    hs = []
    for c in range(tf // MXU_COLS):
        sl = slice(c * MXU_COLS, (c + 1) * MXU_COLS)
        g = jnp.dot(xg, w1_ref[:, sl], preferred_element_type=F32)
        u = jnp.dot(xg, w3_ref[:, sl], preferred_element_type=F32)
        hs.append((_silu(g) * u).astype(BF16))
        issue_rows(c)
    h = jnp.concatenate(hs, axis=1)
    for c in range(d // MXU_COLS):
        sl = slice(c * MXU_COLS, (c + 1) * MXU_COLS)
        acc_ref[:, sl] += jnp.dot(h, w2_ref[:, sl], preferred_element_type=F32)
        issue_rows(tf // MXU_COLS + c)

    @pl.when(f == n_f - 1)
    def _():
        @pl.when(i >= 1)
        def _():
            wait_all(obuf, ssem, slot)

        obuf[slot] = acc_ref[...] * val_ref[...]

        @pl.when(i == n_tiles - 1)
        def _():
            def issue(r, carry):
                _row_copy(obuf.at[slot], r, z_hbm, dst_ref[n_tiles, r], ssem.at[slot]).start()
                return carry

            lax.fori_loop(0, cap, issue, 0)
            wait_all(obuf, ssem, slot)
            wait_all(obuf, ssem, other)
            wait_all(gbuf, gsem, other)


def _experts(idx, dst_ext, h2, w1, w3, w2, vals, n_rows_out, tf):
    bsz, e_cnt, cap, _ = vals.shape
    d = h2.shape[-1]
    n_f = w1.shape[-1] // tf
    n_tiles = bsz * e_cnt
    grid_spec = pltpu.PrefetchScalarGridSpec(
        num_scalar_prefetch=2,
        grid=(bsz, e_cnt, n_f),
        in_specs=[pl.BlockSpec(memory_space=pl.ANY),
                  pl.BlockSpec((None, d, tf), lambda b, e, f, *_: (e, 0, f)),
                  pl.BlockSpec((None, d, tf), lambda b, e, f, *_: (e, 0, f)),
                  pl.BlockSpec((None, tf, d), lambda b, e, f, *_: (e, f, 0)),
                  pl.BlockSpec((None, None, cap, 1), lambda b, e, f, *_: (b, e, 0, 0))],
        out_specs=pl.BlockSpec(memory_space=pl.ANY),
        scratch_shapes=[pltpu.VMEM((2, cap, d), F32), pltpu.VMEM((2, cap, d), F32),
                        pltpu.VMEM((cap, d), BF16), pltpu.VMEM((cap, d), F32),
                        pltpu.SemaphoreType.DMA((2,)), pltpu.SemaphoreType.DMA((2,))],
    )
    return pl.pallas_call(
        functools.partial(_expert_kernel, cap, e_cnt, n_f, n_tiles),
        grid_spec=grid_spec,
        out_shape=jax.ShapeDtypeStruct((n_rows_out, d), F32),
        compiler_params=_params(3),
        name="moe_experts",
    )(idx, dst_ext, h2, w1, w3, w2, vals)


COMBINE_ROWS = 256


def _combine_kernel(n_slot, starts_ref, ts_ref, x1_ref, g_ref, w_ref, z_hbm, o_ref, zbuf, acc_ref, sem):
    b, t = pl.program_id(0), pl.program_id(1)
    nt = pl.num_programs(1)
    s0 = starts_ref[b * (nt + 1) + t]
    s1 = starts_ref[b * (nt + 1) + t + 1]
    start0 = lax.shift_left(lax.shift_right_logical(s0, 3), 3)
    n_chunks = lax.shift_right_logical(s1 - start0 + COMBINE_ROWS - 1, COMBINE_ROWS.bit_length() - 1)

    def copy(c, slot):
        row = pl.multiple_of(b * n_slot + start0 + c * COMBINE_ROWS, 8)
        return pltpu.make_async_copy(z_hbm.at[pl.ds(row, COMBINE_ROWS), :], zbuf.at[slot], sem.at[slot])

    acc_ref[...] = jnp.zeros_like(acc_ref)

    @pl.when(n_chunks > 0)
    def _():
        copy(0, 0).start()

    first = ts_ref[:, 0:1]
    last = ts_ref[:, 1:2]

    def body(c, carry):
        slot = lax.rem(c, 2)
        copy(c, slot).wait()

        @pl.when(c + 1 < n_chunks)
        def _():
            copy(c + 1, 1 - slot).start()

        rows = zbuf[slot]
        j = (start0 + c * COMBINE_ROWS + lax.broadcasted_iota(jnp.int32, (1, COMBINE_ROWS), 1)).astype(F32)
        seg = jnp.where(j >= first, jnp.where(j < last, 1.0, 0.0), 0.0).astype(BF16)
        hi = rows.astype(BF16)
        lo = (rows - hi.astype(F32)).astype(BF16)
        acc_ref[...] += (jnp.dot(seg, hi, preferred_element_type=F32)
                         + jnp.dot(seg, lo, preferred_element_type=F32))
        return carry

    lax.fori_loop(0, n_chunks, body, 0)
    o_ref[...] = x1_ref[...] + g_ref[...] * (_rms(acc_ref[...]) * w_ref[...])


def _combine(starts, ts2, z, x1, mod3, npost_ffn, n_slot, tm):
    bsz, n_tok, d = x1.shape
    grid_spec = pltpu.PrefetchScalarGridSpec(
        num_scalar_prefetch=1,
        grid=(bsz, n_tok // tm),
        in_specs=[pl.BlockSpec((None, tm, 2), lambda b, t, *_: (b, t, 0)),
                  pl.BlockSpec((None, tm, d), lambda b, t, *_: (b, t, 0)),
                  pl.BlockSpec((None, 1, d), lambda b, t, *_: (b * 6 + 5, 0, 0)),
                  pl.BlockSpec((1, d), lambda b, t, *_: (0, 0)),
                  pl.BlockSpec(memory_space=pl.ANY)],
        out_specs=pl.BlockSpec((None, tm, d), lambda b, t, *_: (b, t, 0)),
        scratch_shapes=[pltpu.VMEM((2, COMBINE_ROWS, d), F32), pltpu.VMEM((tm, d), F32),
                        pltpu.SemaphoreType.DMA((2,))],
    )
    return pl.pallas_call(
        functools.partial(_combine_kernel, n_slot),
        grid_spec=grid_spec,
        out_shape=jax.ShapeDtypeStruct((bsz, n_tok, d), F32),
        compiler_params=_params(2),
        name="moe_combine",
    )(starts, ts2, x1, mod3, npost_ffn, z)


def _ssd_inputs(h, w_xs, w_bc, w_dt, conv_w, conv_b, dt_bias_pad, seg, tm):
    cw_x, cw_bc = conv_w[:, :SSD_INNER], conv_w[:, SSD_INNER:]
    cb_x, cb_bc = conv_b[:, :SSD_INNER], conv_b[:, SSD_INNER:]
    epi = functools.partial(_epi_conv_silu, seg)
    (xs,) = _fused_matmul([h], [(w_xs, 0)], [0], [cw_x, cb_x], [], epi, [F32], SSD_INNER, tm, 1024, "proj_xs")
    (bc,) = _fused_matmul([h], [(w_bc, 0)], [0], [cw_bc, cb_bc], [], epi, [BF16],
                          2 * SSD_GROUPS * SSD_STATE, tm, 1024, "proj_bc")
    (dt,) = _fused_matmul([h], [(w_dt, 0)], [0], [dt_bias_pad], [], _epi_softplus, [F32], LANES, tm, LANES,
                          "proj_dt")
    return xs, bc, dt


def kernel(x, c, ctx, c_ctx, w_mod, b_mod, norm_pre_mix, norm_post_mix, w_in, ssd_conv_w, ssd_conv_b,
           dt_bias, a_log, ssd_d, ssd_norm_w, w_ssd_out, sc_conv_w, w_sc_out, b_gate, w_o, norm_pre_ffn,
           norm_post_ffn, w_router, w_e1, w_e3, w_e2):
    assert w_mod.shape[0] == 1, "single-layer block"
    bsz, length, d = x.shape
    ctx_len = ctx.shape[1]
    m = bsz * length
    cap = CAPACITY_FACTOR * length // N_EXPERTS
    row = lambda v: v.reshape(1, -1)

    o_z, o_xbc, o_dt, o_scb, o_scc, o_scv = (SSD_INNER, SSD_INNER + SSD_XBC,
                                             SSD_INNER + SSD_XBC + 2 * SSD_HEADS,
                                             SSD_INNER + SSD_XBC + 2 * SSD_HEADS + SC_DIM,
                                             SSD_INNER + SSD_XBC + 2 * SSD_HEADS + 2 * SC_DIM,
                                             SSD_INNER + SSD_XBC + 2 * SSD_HEADS + 3 * SC_DIM)
    w_in0 = w_in[0]
    w_z = w_in0[:, :o_z].astype(BF16)
    w_xs = w_in0[:, o_z:o_z + SSD_INNER].astype(BF16)
    w_bc = w_in0[:, o_z + SSD_INNER:o_xbc].astype(BF16)
    w_dt = jnp.pad(w_in0[:, o_xbc:o_dt], ((0, 0), (0, LANES - 2 * SSD_HEADS))).astype(BF16)
    w_scb = w_in0[:, o_dt:o_scb].astype(BF16)
    w_scc = w_in0[:, o_scb:o_scc].astype(BF16)
    w_scv = w_in0[:, o_scc:o_scv].astype(BF16)
    w_gate = w_in0[:, o_scv:].astype(BF16)
    dt_bias_pad = jnp.pad(dt_bias[0].reshape(1, -1), ((0, 0), (0, LANES - 2 * SSD_HEADS)))
    a_neg = -jnp.exp(a_log[0])
    ii = lax.broadcasted_iota(jnp.int32, (CHUNK, CHUNK), 0)
    jj = lax.broadcasted_iota(jnp.int32, (CHUNK, CHUNK), 1)
    tri = jnp.stack([ii >= jj, ii <= jj]).astype(BF16)
    conv_w, conv_b = ssd_conv_w[0], row(ssd_conv_b[0])

    cs = jnp.concatenate([c, c_ctx[None, :], jnp.zeros((8 - bsz - 1, d), F32)], axis=0)
    mod3 = _modulation(cs, w_mod[0], b_mod[0]).reshape(8 * 6, 1, d)

    hc = _norm_mod(ctx, norm_pre_mix[0], mod3, lambda b: bsz * 6, lambda b: bsz * 6 + 1, ctx_len)
    xs_c, bc_c, dt_c = _ssd_inputs(hc.reshape(bsz * ctx_len, d), w_xs, w_bc, w_dt, conv_w, conv_b,
                                   dt_bias_pad, ctx_len, ctx_len)
    h_zero = jnp.zeros((2, bsz, SSD_GROUPS, SSD_STATE, GROUP_W), F32)
    _, states = _ssd_scan(xs_c.reshape(bsz, ctx_len, -1), bc_c.reshape(bsz, ctx_len, -1),
                          dt_c.reshape(bsz, ctx_len, -1), a_neg, tri, h_zero)

    tm = 1024
    hx = _norm_mod(x, norm_pre_mix[0], mod3, lambda b: b * 6, lambda b: b * 6 + 1, tm).reshape(m, d)
    (z,) = _fused_matmul([hx], [(w_z, 0)], [0], [], [], _epi_identity, [F32], SSD_INNER, tm, 1024, "proj_z")
    xs, bc, dt = _ssd_inputs(hx, w_xs, w_bc, w_dt, conv_w, conv_b, dt_bias_pad, GRID_W, tm)
    (sc_y,) = _fused_matmul([hx], [(w_scb, 0), (w_scc, 0), (w_scv, 0)], [0, 0, 0], [sc_conv_w[0]], [],
                            functools.partial(_epi_shortconv, GRID_W), [BF16], SC_DIM, tm, 512, "proj_sc")
    (gates,) = _fused_matmul([hx], [(w_gate, 0)], [0], [row(b_gate[0])], [], _epi_sigmoid, [F32],
                             2 * d, tm, 1024, "proj_gate")
    y2, _ = _ssd_scan(xs.reshape(bsz, length, -1), bc.reshape(bsz, length, -1), dt.reshape(bsz, length, -1),
                      a_neg, tri, states)
    d_cols = jnp.repeat(ssd_d[0], SSD_HEAD_DIM).reshape(1, -1)
    y_norm = _ssd_post(y2, xs, z, d_cols, row(ssd_norm_w[0]), 256)
    tn = 1024
    (merged,) = _fused_matmul([y_norm, sc_y], [(w_ssd_out[0].astype(BF16), 0), (w_sc_out[0].astype(BF16), 0)],
                              [0, 1], [], [(gates, 0), (gates, d // tn)], _epi_gate_merge, [BF16], d, 512, tn,
                              "mix_merge")
    w_router_pad = jnp.pad(w_router[0], ((0, 0), (0, LANES - N_EXPERTS))).astype(BF16)
    x1, h2, aff = _mix_out(merged, w_o[0].astype(BF16), x.reshape(m, d), mod3, row(norm_post_mix[0]),
                           row(norm_pre_ffn[0]), w_router_pad, length, 256)

    aff_t = jnp.transpose(aff[:, :N_EXPERTS].reshape(bsz, length, N_EXPERTS), (0, 2, 1))
    e_i = lax.broadcasted_iota(jnp.int32, (N_EXPERTS, N_EXPERTS), 0)
    e_j = lax.broadcasted_iota(jnp.int32, (N_EXPERTS, N_EXPERTS), 1)
    lists, ts = _route(aff_t, tri[1], (e_j < e_i).astype(BF16), cap)
    n_slot = N_EXPERTS * cap
    li = lists[..., :4].astype(jnp.int32)
    b_i = jnp.arange(bsz, dtype=jnp.int32)[:, None, None]
    idx = (li[..., 0] * 64 + li[..., 1] + b_i * length).reshape(bsz * N_EXPERTS, cap)
    dst = (li[..., 2] * LANES + li[..., 3] + b_i * n_slot).reshape(bsz * N_EXPERTS, cap)
    dst_ext = jnp.concatenate([bsz * n_slot + jnp.arange(cap, dtype=jnp.int32)[None, :], dst], axis=0)
    vals = (lists[..., 4] + lists[..., 5] + lists[..., 6])[..., None]
    z = _experts(idx, dst_ext, h2, w_e1[0].astype(BF16), w_e3[0].astype(BF16), w_e2[0].astype(BF16), vals,
                 bsz * n_slot + cap, 512)
    tm_c = 256
    ts = ts.reshape(bsz, length)
    end = jnp.full((bsz, 1), n_slot, F32)
    ts2 = jnp.stack([ts, jnp.concatenate([ts[:, 1:], end], axis=1)], axis=-1)
    starts = jnp.concatenate([ts[:, ::tm_c], end], axis=1).astype(jnp.int32).reshape(-1)
    return _combine(starts, ts2, z, x1.reshape(bsz, length, d), mod3, row(norm_post_ffn[0]), n_slot, tm_c)
```

```python
import functools
import math

import jax
import jax.numpy as jnp
from jax import lax
from jax.experimental import pallas as pl
from jax.experimental.pallas import tpu as pltpu

F32 = jnp.float32
BF16 = jnp.bfloat16

D_MODEL = 2048
GRID_W = 64
SSD_HEADS = 32
SSD_HEAD_DIM = 64
SSD_INNER = SSD_HEADS * SSD_HEAD_DIM
SSD_GROUPS = 4
SSD_HPG = SSD_HEADS // SSD_GROUPS
SSD_STATE = 128
CHUNK = 128
SSD_XBC = SSD_INNER + 2 * SSD_GROUPS * SSD_STATE
SC_DIM = 2048
N_EXPERTS = 16
EXPERT_FF = 2048
CAPACITY_FACTOR = 2
EPS = 1e-6
GROUP_W = SSD_HPG * SSD_HEAD_DIM

LANES = 128
V7X_VMEM_BYTES = 64 * 1024 * 1024
VMEM_LIMIT = V7X_VMEM_BYTES - 8 * 1024 * 1024


def _params(n_grid):
    return pltpu.CompilerParams(dimension_semantics=("arbitrary",) * n_grid,
                                vmem_limit_bytes=VMEM_LIMIT)


def _silu(x):
    return x * (1.0 / (1.0 + jnp.exp(-x)))


def _sigmoid(x):
    return 1.0 / (1.0 + jnp.exp(-x))


def _softplus(x):
    return jnp.maximum(x, 0.0) + jnp.log(1.0 + jnp.exp(-jnp.abs(x)))


def _rms(x):
    return x * lax.rsqrt(jnp.mean(x * x, axis=-1, keepdims=True) + EPS)


def _mod_kernel(c_ref, w_ref, b_ref, o_ref):
    a = _silu(c_ref[...]).astype(BF16)
    o_ref[...] = jnp.dot(a, w_ref[...].astype(BF16), preferred_element_type=F32) + b_ref[...]


def _modulation(cs, w_mod, b_mod):
    rows, d = cs.shape
    n = w_mod.shape[1]
    tn = 1024
    return pl.pallas_call(
        _mod_kernel,
        grid=(n // tn,),
        in_specs=[pl.BlockSpec((rows, d), lambda j: (0, 0)),
                  pl.BlockSpec((d, tn), lambda j: (0, j)),
                  pl.BlockSpec((1, tn), lambda j: (0, j))],
        out_specs=pl.BlockSpec((rows, tn), lambda j: (0, j)),
        out_shape=jax.ShapeDtypeStruct((rows, n), F32),
        compiler_params=_params(1),
        name="modulation",
    )(cs, w_mod, b_mod.reshape(1, n))


def _norm_mod_kernel(x_ref, w_ref, sc_ref, sh_ref, o_ref):
    y = _rms(x_ref[...]) * w_ref[...]
    o_ref[...] = (y * (1.0 + sc_ref[...]) + sh_ref[...]).astype(o_ref.dtype)


def _norm_mod(x, w, mod3, sh_row, sc_row, tm):
    bsz, length, d = x.shape
    return pl.pallas_call(
        _norm_mod_kernel,
        grid=(bsz, length // tm),
        in_specs=[pl.BlockSpec((None, tm, d), lambda b, i: (b, i, 0)),
                  pl.BlockSpec((1, d), lambda b, i: (0, 0)),
                  pl.BlockSpec((None, 1, d), lambda b, i: (sc_row(b), 0, 0)),
                  pl.BlockSpec((None, 1, d), lambda b, i: (sh_row(b), 0, 0))],
        out_specs=pl.BlockSpec((None, tm, d), lambda b, i: (b, i, 0)),
        out_shape=jax.ShapeDtypeStruct((bsz, length, d), BF16),
        compiler_params=_params(2),
        name="norm_mod",
    )(x, w.reshape(1, d), mod3, mod3)


def _fused_matmul(a_list, w_list, w_to_a, col_list, tile_list, epilogue, out_dtypes, n, tm, tn, name):
    m, k = a_list[0].shape
    na, nw, nc, nt = len(a_list), len(w_list), len(col_list), len(tile_list)

    def body(*refs):
        a_refs = refs[:na]
        w_refs = refs[na:na + nw]
        c_refs = refs[na + nw:na + nw + nc]
        t_refs = refs[na + nw + nc:na + nw + nc + nt]
        o_refs = refs[na + nw + nc + nt:]
        prods = [jnp.dot(a_refs[w_to_a[i]][...], w_refs[i][...], preferred_element_type=F32)
                 for i in range(nw)]
        outs = epilogue(prods, [c[...] for c in c_refs], [t[...] for t in t_refs])
        for o_ref, o in zip(o_refs, outs):
            o_ref[...] = o.astype(o_ref.dtype)

    in_specs = [pl.BlockSpec((tm, k), lambda i, j: (i, 0)) for _ in a_list]
    args = list(a_list)
    for w, off in w_list:
        in_specs.append(pl.BlockSpec((k, tn), functools.partial(lambda i, j, off: (0, j + off), off=off)))
        args.append(w)
    for c in col_list:
        in_specs.append(pl.BlockSpec((c.shape[0], tn), lambda i, j: (0, j)))
        args.append(c)
    for t, off in tile_list:
        in_specs.append(pl.BlockSpec((tm, tn), functools.partial(lambda i, j, off: (i, j + off), off=off)))
        args.append(t)
    return pl.pallas_call(
        body,
        grid=(m // tm, n // tn),
        in_specs=in_specs,
        out_specs=[pl.BlockSpec((tm, tn), lambda i, j: (i, j)) for _ in out_dtypes],
        out_shape=[jax.ShapeDtypeStruct((m, n), dt) for dt in out_dtypes],
        compiler_params=_params(2),
        name=name,
    )(*args)


def _conv3_rows(x, cw, seg):
    tm = x.shape[0]
    r = lax.broadcasted_iota(jnp.int32, x.shape, 0) & (seg - 1)
    xp = jnp.where(r == 0, 0.0, pltpu.roll(x, 1, 0))
    xn = jnp.where(r == seg - 1, 0.0, pltpu.roll(x, tm - 1, 0))
    return xp * cw[0:1, :] + x * cw[1:2, :] + xn * cw[2:3, :]


def _epi_identity(prods, cols, tiles):
    return [prods[0]]


def _epi_conv_silu(seg, prods, cols, tiles):
    return [_silu(_conv3_rows(prods[0], cols[0], seg) + cols[1])]


def _epi_softplus(prods, cols, tiles):
    return [_softplus(prods[0] + cols[0])]


def _epi_shortconv(seg, prods, cols, tiles):
    sc_b, sc_c, sc_v = prods
    return [sc_b * _conv3_rows(sc_c * sc_v, cols[0], seg)]


def _epi_sigmoid(prods, cols, tiles):
    return [_sigmoid(prods[0] + cols[0])]


def _epi_gate_merge(prods, cols, tiles):
    return [tiles[0] * prods[0] + tiles[1] * prods[1]]


def _split3(x):
    hi = x.astype(BF16)
    r1 = x - hi.astype(F32)
    mid = r1.astype(BF16)
    lo = (r1 - mid.astype(F32)).astype(BF16)
    return hi, mid, lo


def _cumsum_rows(tri, x):
    return sum(jnp.dot(tri, p, preferred_element_type=F32) for p in _split3(x))


def _cumsum_cols(x, tri):
    return sum(jnp.dot(p, tri, preferred_element_type=F32) for p in _split3(x))


def _pair(lo_mask, v, h):
    return jnp.where(lo_mask, v[:, h:h + 1], v[:, h + 1:h + 2])


def _spread(x, sel):
    return jnp.dot(jnp.concatenate(_split3(x), axis=1), sel, preferred_element_type=F32)


def _ssd_kernel(xs_ref, bc_ref, dtc_ref, dtr_ref, arow_ref, acol_ref, tri_ref, trit_ref, sel128_ref, sel64_ref,
                h0_ref, y_ref, hfin_ref, st_ref):
    q, n = CHUNK, SSD_STATE
    step = pl.program_id(2)

    @pl.when(step == 0)
    def _():
        st_ref[...] = h0_ref[...]

    tri = tri_ref[...]
    dtc = dtc_ref[...]
    dtr = dtr_ref[...]
    al_c = dtc * arow_ref[...]
    al_r = dtr * acol_ref[...]
    cs_c = _cumsum_rows(tri, al_c)
    cs_r = _cumsum_cols(al_r, trit_ref[...])
    tot_c = jnp.sum(al_c, axis=0, keepdims=True)
    cs_b = _spread(cs_c, sel128_ref[...])
    dt_b = _spread(dtc, sel64_ref[...])
    mask = tri.astype(F32) > 0.5
    lo_q = lax.broadcasted_iota(jnp.int32, (q, LANES), 1) < SSD_HEAD_DIM
    lo_1 = lax.broadcasted_iota(jnp.int32, (1, LANES), 1) < SSD_HEAD_DIM
    for gi in range(SSD_GROUPS):
        bm = bc_ref[:, gi * n:(gi + 1) * n]
        cm = bc_ref[:, (SSD_GROUPS + gi) * n:(SSD_GROUPS + gi + 1) * n]
        cb = lax.dot_general(cm, bm, (((1,), (1,)), ((), ())), preferred_element_type=F32)
        ht = st_ref[gi]
        y_off = jnp.dot(cm, ht.astype(BF16), preferred_element_type=F32)
        xw = []
        dec = []
        for k in range(SSD_HPG // 2):
            h0 = gi * SSD_HPG + 2 * k
            col = gi * GROUP_W + k * LANES
            ms = []
            for h in (h0, h0 + 1):
                seg = cs_b[:, h * LANES:(h + 1) * LANES] - cs_r[h:h + 1, :]
                lmat = jnp.exp(jnp.where(mask, seg, -1e30))
                ms.append((cb * lmat * dtr[h:h + 1, :]).astype(BF16))
            lhs = jnp.concatenate(ms, axis=1)
            xp = xs_ref[:, col:col + LANES]
            rhs = jnp.concatenate([jnp.where(lo_q, xp, 0.0).astype(BF16),
                                   jnp.where(lo_q, 0.0, xp).astype(BF16)], axis=0)
            y_diag = jnp.dot(lhs, rhs, preferred_element_type=F32)
            cs_p = jnp.where(lo_q, cs_b[:, h0 * LANES:(h0 + 1) * LANES], cs_b[:, (h0 + 1) * LANES:(h0 + 2) * LANES])
            tot_p = _pair(lo_1, tot_c, h0)
            dt_p = dt_b[:, h0 * SSD_HEAD_DIM:(h0 + 2) * SSD_HEAD_DIM]
            y_ref[:, col:col + LANES] = y_diag + y_off[:, k * LANES:(k + 1) * LANES] * jnp.exp(cs_p)
            xw.append((xp * (jnp.exp(tot_p - cs_p) * dt_p)).astype(BF16))
            dec.append(jnp.exp(tot_p))
        xw = jnp.concatenate(xw, axis=1)
        s_t = lax.dot_general(bm, xw, (((0,), (0,)), ((), ())), preferred_element_type=F32)
        st_ref[gi] = ht * jnp.concatenate(dec, axis=1) + s_t

    @pl.when(step == pl.num_programs(2) - 1)
    def _():
        hfin_ref[...] = st_ref[...]


def _ssd_scan(xs, bc, dt, a_neg, tri, h0):
    bsz, length, _ = xs.shape
    nc = length // CHUNK
    g, q, n, nh = SSD_GROUPS, CHUNK, SSD_STATE, SSD_HEADS
    dt4 = dt[:, :, :2 * nh].reshape(bsz, length, 2, nh)
    dt_col = jnp.transpose(dt4, (2, 0, 1, 3))
    dt_row = jnp.transpose(dt4, (2, 0, 3, 1))
    a_row = a_neg.reshape(2, 1, nh)
    a_col = a_neg.reshape(2, nh, 1)

    def head_select(width):
        rows = lax.broadcasted_iota(jnp.int32, (3 * nh, nh * width), 0) % nh
        cols = lax.broadcasted_iota(jnp.int32, (3 * nh, nh * width), 1) // width
        return (rows == cols).astype(BF16)

    sel128, sel64 = head_select(LANES), head_select(SSD_HEAD_DIM)

    def chunk(d, c):
        return c + d * (nc - 1 - 2 * c)

    return pl.pallas_call(
        _ssd_kernel,
        grid=(2, bsz, nc),
        in_specs=[
            pl.BlockSpec((None, q, SSD_INNER), lambda d, b, c: (b, chunk(d, c), 0)),
            pl.BlockSpec((None, q, 2 * g * n), lambda d, b, c: (b, chunk(d, c), 0)),
            pl.BlockSpec((None, None, q, nh), lambda d, b, c: (d, b, chunk(d, c), 0)),
            pl.BlockSpec((None, None, nh, q), lambda d, b, c: (d, b, 0, chunk(d, c))),
            pl.BlockSpec((None, 1, nh), lambda d, b, c: (d, 0, 0)),
            pl.BlockSpec((None, nh, 1), lambda d, b, c: (d, 0, 0)),
            pl.BlockSpec((None, q, q), lambda d, b, c: (d, 0, 0)),
            pl.BlockSpec((None, q, q), lambda d, b, c: (1 - d, 0, 0)),
            pl.BlockSpec(sel128.shape, lambda d, b, c: (0, 0)),
            pl.BlockSpec(sel64.shape, lambda d, b, c: (0, 0)),
            pl.BlockSpec((None, None, g, n, GROUP_W), lambda d, b, c: (d, b, 0, 0, 0)),
        ],
        out_specs=[
            pl.BlockSpec((None, None, q, SSD_INNER), lambda d, b, c: (d, b, chunk(d, c), 0)),
            pl.BlockSpec((None, None, g, n, GROUP_W), lambda d, b, c: (d, b, 0, 0, 0)),
        ],
        out_shape=[jax.ShapeDtypeStruct((2, bsz, length, SSD_INNER), F32),
                   jax.ShapeDtypeStruct((2, bsz, g, n, GROUP_W), F32)],
        scratch_shapes=[pltpu.VMEM((g, n, GROUP_W), F32)],
        compiler_params=_params(3),
        name="ssd_scan",
    )(xs, bc, dt_col, dt_row, a_row, a_col, tri, tri, sel128, sel64, h0)


def _ssd_post_kernel(yf_ref, yb_ref, xs_ref, z_ref, d_ref, w_ref, o_ref):
    y = (yf_ref[...] + yb_ref[...] + d_ref[...] * xs_ref[...]) * _silu(z_ref[...])
    for g in range(SSD_GROUPS):
        sl = slice(g * GROUP_W, (g + 1) * GROUP_W)
        o_ref[:, sl] = (_rms(y[:, sl]) * w_ref[:, sl]).astype(o_ref.dtype)


def _ssd_post(y2, xs, z, d_cols, norm_w, tm):
    _, bsz, length, d = y2.shape
    nt = length // tm
    return pl.pallas_call(
        _ssd_post_kernel,
        grid=(bsz, nt),
        in_specs=[pl.BlockSpec((None, None, tm, d), lambda b, i: (0, b, i, 0)),
                  pl.BlockSpec((None, None, tm, d), lambda b, i: (1, b, i, 0)),
                  pl.BlockSpec((tm, d), lambda b, i: (b * nt + i, 0)),
                  pl.BlockSpec((tm, d), lambda b, i: (b * nt + i, 0)),
                  pl.BlockSpec((1, d), lambda b, i: (0, 0)),
                  pl.BlockSpec((1, d), lambda b, i: (0, 0))],
        out_specs=pl.BlockSpec((tm, d), lambda b, i: (b * nt + i, 0)),
        out_shape=jax.ShapeDtypeStruct((bsz * length, d), BF16),
        compiler_params=_params(2),
        name="ssd_post",
    )(y2, y2, xs, z, d_cols, norm_w)


def _mix_out_kernel(m_ref, wo_ref, x_ref, gx_ref, scf_ref, shf_ref, npost_ref, npre_ref, wr_ref,
                    x1_ref, h2_ref, aff_ref):
    out = jnp.dot(m_ref[...], wo_ref[...], preferred_element_type=F32)
    x1 = x_ref[...] + gx_ref[...] * (_rms(out) * npost_ref[...])
    x1_ref[...] = x1
    h2 = (_rms(x1) * npre_ref[...]) * (1.0 + scf_ref[...]) + shf_ref[...]
    h2_ref[...] = h2
    logits = jnp.dot(h2.astype(BF16), wr_ref[...], preferred_element_type=F32)
    lane = lax.broadcasted_iota(jnp.int32, logits.shape, 1)
    logits = jnp.where(lane < N_EXPERTS, logits, -1e30)
    e = jnp.exp(logits - jnp.max(logits, axis=-1, keepdims=True))
    aff_ref[...] = e / jnp.sum(e, axis=-1, keepdims=True)


def _mix_out(merged, w_o, x2d, mod3, npost, npre, w_router_pad, length, tm):
    m, d = merged.shape
    nt = length // tm
    row = lambda kind: (lambda i: ((i // nt) * 6 + kind, 0, 0))
    return pl.pallas_call(
        _mix_out_kernel,
        grid=(m // tm,),
        in_specs=[pl.BlockSpec((tm, d), lambda i: (i, 0)),
                  pl.BlockSpec((d, d), lambda i: (0, 0)),
                  pl.BlockSpec((tm, d), lambda i: (i, 0)),
                  pl.BlockSpec((None, 1, d), row(2)),
                  pl.BlockSpec((None, 1, d), row(4)),
                  pl.BlockSpec((None, 1, d), row(3)),
                  pl.BlockSpec((1, d), lambda i: (0, 0)),
                  pl.BlockSpec((1, d), lambda i: (0, 0)),
                  pl.BlockSpec((d, LANES), lambda i: (0, 0))],
        out_specs=[pl.BlockSpec((tm, d), lambda i: (i, 0)),
                   pl.BlockSpec((tm, d), lambda i: (i, 0)),
                   pl.BlockSpec((tm, LANES), lambda i: (i, 0))],
        out_shape=[jax.ShapeDtypeStruct((m, d), F32),
                   jax.ShapeDtypeStruct((m, d), F32),
                   jax.ShapeDtypeStruct((m, LANES), F32)],
        compiler_params=_params(1),
        name="mix_out_router",
    )(merged, w_o, x2d, mod3, mod3, mod3, npost, npre, w_router_pad)


def _route_select(cap, aff_ref, triu_ref, low_ref, pos_ref, slot_ref, ts_ref):
    aff = aff_ref[...]
    e_cnt, n_tok = aff.shape

    def count(m):
        return jnp.sum(jnp.where(m, 1.0, 0.0), axis=1, keepdims=True)

    def search_bits(_, carry):
        lo, hi = carry
        mid = lo + ((hi - lo + 1) >> 1)
        ok = count(aff >= pltpu.bitcast(mid, F32)) >= cap
        return jnp.where(ok, mid, lo), jnp.where(ok, hi, mid - 1)

    lo0 = jnp.zeros((e_cnt, 1), jnp.int32)
    hi0 = jnp.full((e_cnt, 1), 0x7F800000, jnp.int32)
    lo_b, _ = lax.fori_loop(0, 32, search_bits, (lo0, hi0))

    def search_mid(_, carry):
        lo, hi = carry
        mid = 0.5 * (lo + hi)
        ok = count(aff >= mid) >= cap
        return jnp.where(ok, mid, lo), jnp.where(ok, hi, mid)

    lo_f, _ = lax.fori_loop(0, 24, search_mid, (pltpu.bitcast(lo_b, F32), pltpu.bitcast(lo_b + 1, F32)))
    thr = jnp.min(jnp.where(aff >= lo_f, aff, jnp.inf), axis=1, keepdims=True)
    need_f = cap - count(aff > thr)
    triu = triu_ref[...]
    carry_eq = jnp.zeros((e_cnt, 1), F32)
    carry_sel = jnp.zeros((e_cnt, 1), F32)
    for k in range(n_tok // LANES):
        sl = slice(k * LANES, (k + 1) * LANES)
        aff_k = aff[:, sl]
        eq_f = jnp.where(aff_k == thr, 1.0, 0.0)
        eq_incl = jnp.dot(eq_f.astype(BF16), triu, preferred_element_type=F32) + carry_eq
        sel_k = (aff_k > thr) | ((aff_k == thr) & (eq_incl <= need_f))
        sel_f = jnp.where(sel_k, 1.0, 0.0)
        sel_b = sel_f.astype(BF16)
        sel_incl = jnp.dot(sel_b, triu, preferred_element_type=F32) + carry_sel
        excl = sel_incl - sel_f
        pos_ref[:, sl] = jnp.where(sel_k, excl.astype(jnp.int32), -1)
        tok_start = jnp.sum(excl, axis=0, keepdims=True)
        ts_ref[:, sl] = tok_start
        slot_ref[:, sl] = tok_start + jnp.dot(low_ref[...], sel_b, preferred_element_type=F32)
        carry_eq = eq_incl[:, LANES - 1:LANES]
        carry_sel = sel_incl[:, LANES - 1:LANES]


def _route_kernel(cap, aff_ref, triu_ref, low_ref, list_ref, ts_ref, pos_ref, slot_ref):
    e = pl.program_id(1)

    @pl.when(e == 0)
    def _():
        _route_select(cap, aff_ref, triu_ref, low_ref, pos_ref, slot_ref, ts_ref)

    n_tok = aff_ref.shape[1]
    pos = pos_ref[pl.ds(e, 1), :]
    slot = slot_ref[pl.ds(e, 1), :]
    aff = aff_ref[pl.ds(e, 1), :]
    hit = lax.broadcasted_iota(jnp.int32, (cap, n_tok), 0) == pos
    tok = lax.broadcasted_iota(jnp.int32, (1, n_tok), 1)
    slot_hi = jnp.floor(slot * (1.0 / LANES))
    a_hi = aff.astype(BF16).astype(F32)
    a_mid = (aff - a_hi).astype(BF16).astype(F32)
    rows = [(tok >> 6).astype(F32), (tok & 63).astype(F32), slot_hi, slot - slot_hi * LANES,
            a_hi, a_mid, aff - a_hi - a_mid]
    sub = lax.broadcasted_iota(jnp.int32, (8, n_tok), 0)
    vals = jnp.zeros((8, n_tok), F32)
    for i, r in enumerate(rows):
        vals = jnp.where(sub == i, r, vals)
    list_ref[...] = lax.dot_general(jnp.where(hit, 1.0, 0.0).astype(BF16), vals.astype(BF16),
                                    (((1,), (1,)), ((), ())), preferred_element_type=F32)


def _route(aff_t, triu, low, cap):
    bsz, e_cnt, n_tok = aff_t.shape
    return pl.pallas_call(
        functools.partial(_route_kernel, cap),
        grid=(bsz, e_cnt),
        in_specs=[pl.BlockSpec((None, e_cnt, n_tok), lambda b, e: (b, 0, 0)),
                  pl.BlockSpec((LANES, LANES), lambda b, e: (0, 0)),
                  pl.BlockSpec((e_cnt, e_cnt), lambda b, e: (0, 0))],
        out_specs=[pl.BlockSpec((None, None, cap, 8), lambda b, e: (b, e, 0, 0)),
                   pl.BlockSpec((None, 1, n_tok), lambda b, e: (b, 0, 0))],
        out_shape=[jax.ShapeDtypeStruct((bsz, e_cnt, cap, 8), F32),
                   jax.ShapeDtypeStruct((bsz, 1, n_tok), F32)],
        scratch_shapes=[pltpu.VMEM((e_cnt, n_tok), jnp.int32), pltpu.VMEM((e_cnt, n_tok), F32)],
        compiler_params=_params(2),
        name="route_topk",
    )(aff_t, triu, low)


MXU_COLS = 256


def _expert_kernel(cap, n_e, n_f, n_tiles, idx_ref, dst_ref, h_hbm, w1_ref, w3_ref, w2_ref, val_ref, z_hbm,
                   gbuf, obuf, xg_ref, acc_ref, gsem, ssem):
    f = pl.program_id(2)
    i = pl.program_id(0) * n_e + pl.program_id(1)
    slot = lax.rem(i, 2)
    other = 1 - slot
    nxt = jnp.minimum(i + 1, n_tiles - 1)
    tf, d = w2_ref.shape
    rows = cap // n_f

    def gather(tile, s, q, r):
        src = idx_ref[tile * cap + q * rows + r]
        return pltpu.make_async_copy(h_hbm.at[pl.ds(src, 1), :], gbuf.at[s, q, pl.ds(r, 1), :], gsem.at[s])

    def scatter(pos, s, q, r):
        dst = dst_ref[pos * cap + q * rows + r]
        return pltpu.make_async_copy(obuf.at[s, q, pl.ds(r, 1), :], z_hbm.at[pl.ds(dst, 1), :], ssem.at[s])

    def wait_all(buf, sem, s):
        pltpu.make_async_copy(buf.at[s], buf.at[s], sem.at[s]).wait()

    def for_all_rows(fn):
        def body(r, carry):
            for q in range(n_f):
                fn(q, r)
            return carry

        lax.fori_loop(0, rows, body, 0)

    @pl.when(jnp.logical_and(i == 0, f == 0))
    def _():
        obuf[1] = jnp.zeros((n_f, rows, d), F32)
        for_all_rows(lambda q, r: gather(0, 0, q, r).start())

    @pl.when(f == 0)
    def _():
        wait_all(gbuf, gsem, slot)
        for q in range(n_f):
            xg_ref[q * rows:(q + 1) * rows, :] = gbuf[slot, q].astype(BF16)
        acc_ref[...] = jnp.zeros_like(acc_ref)

    for r in range(rows):
        gather(nxt, other, f, r).start()
        scatter(i, other, f, r).start()
    xg = xg_ref[...]
    hs = []
    for c in range(tf // MXU_COLS):
        sl = slice(c * MXU_COLS, (c + 1) * MXU_COLS)
        g = jnp.dot(xg, w1_ref[:, sl], preferred_element_type=F32)
        u = jnp.dot(xg, w3_ref[:, sl], preferred_element_type=F32)
        hs.append((_silu(g) * u).astype(BF16))
    h = jnp.concatenate(hs, axis=1)
    for c in range(d // MXU_COLS):
        sl = slice(c * MXU_COLS, (c + 1) * MXU_COLS)
        acc_ref[:, sl] += jnp.dot(h, w2_ref[:, sl], preferred_element_type=F32)

    @pl.when(f == n_f - 1)
    def _():
        @pl.when(i >= 1)
        def _():
            wait_all(obuf, ssem, slot)

        for q in range(n_f):
            sl = slice(q * rows, (q + 1) * rows)
            obuf[slot, q] = acc_ref[sl, :] * val_ref[sl, :]

        @pl.when(i == n_tiles - 1)
        def _():
            for_all_rows(lambda q, r: scatter(n_tiles, slot, q, r).start())
            wait_all(obuf, ssem, slot)
            wait_all(obuf, ssem, other)
            wait_all(gbuf, gsem, other)


def _experts(idx, dst_ext, h2, w1, w3, w2, vals, n_rows_out, tf):
    bsz, e_cnt, cap, _ = vals.shape
    d = h2.shape[-1]
    n_f = w1.shape[-1] // tf
    n_tiles = bsz * e_cnt
    grid_spec = pltpu.PrefetchScalarGridSpec(
        num_scalar_prefetch=2,
        grid=(bsz, e_cnt, n_f),
        in_specs=[pl.BlockSpec(memory_space=pl.ANY),
                  pl.BlockSpec((None, d, tf), lambda b, e, f, *_: (e, 0, f)),
                  pl.BlockSpec((None, d, tf), lambda b, e, f, *_: (e, 0, f)),
                  pl.BlockSpec((None, tf, d), lambda b, e, f, *_: (e, f, 0)),
                  pl.BlockSpec((None, None, cap, 1), lambda b, e, f, *_: (b, e, 0, 0))],
        out_specs=pl.BlockSpec(memory_space=pl.ANY),
        scratch_shapes=[pltpu.VMEM((2, n_f, cap // n_f, d), F32), pltpu.VMEM((2, n_f, cap // n_f, d), F32),
                        pltpu.VMEM((cap, d), BF16), pltpu.VMEM((cap, d), F32),
                        pltpu.SemaphoreType.DMA((2,)), pltpu.SemaphoreType.DMA((2,))],
    )
    return pl.pallas_call(
        functools.partial(_expert_kernel, cap, e_cnt, n_f, n_tiles),
        grid_spec=grid_spec,
        out_shape=jax.ShapeDtypeStruct((n_rows_out, d), F32),
        compiler_params=_params(3),
        name="moe_experts",
    )(idx.reshape(-1), dst_ext.reshape(-1), h2, w1, w3, w2, vals)


COMBINE_ROWS = 256


def _combine_kernel(n_slot, starts_ref, ts_ref, x1_ref, g_ref, w_ref, z_hbm, o_ref, zbuf, acc_ref, sem, cnt_ref):
    b, t = pl.program_id(0), pl.program_id(1)
    nb, nt = pl.num_programs(0), pl.num_programs(1)
    shift = COMBINE_ROWS.bit_length() - 1

    def tile_chunks(bb, tt):
        s0 = starts_ref[bb * (nt + 1) + tt]
        s1 = starts_ref[bb * (nt + 1) + tt + 1]
        start = lax.shift_left(lax.shift_right_logical(s0, 3), 3)
        return start, jnp.maximum(lax.shift_right_logical(s1 - start + COMBINE_ROWS - 1, shift), 1)

    def copy(bb, start, c, slot):
        row = pl.multiple_of(bb * n_slot + start + c * COMBINE_ROWS, 8)
        return pltpu.make_async_copy(z_hbm.at[pl.ds(row, COMBINE_ROWS), :], zbuf.at[slot], sem.at[slot])

    start0, n_chunks = tile_chunks(b, t)
    is_first = jnp.logical_and(b == 0, t == 0)
    is_last = jnp.logical_and(b == nb - 1, t == nt - 1)
    wrap = t == nt - 1
    nxt_b = jnp.where(is_last, b, jnp.where(wrap, b + 1, b))
    nxt_t = jnp.where(is_last, t, jnp.where(wrap, 0, t + 1))
    nxt_start, _ = tile_chunks(nxt_b, nxt_t)

    @pl.when(is_first)
    def _():
        cnt_ref[0] = 0
        copy(b, start0, 0, 0).start()

    acc_ref[...] = jnp.zeros_like(acc_ref)
    base = cnt_ref[0]
    first = ts_ref[:, 0:1]
    last = ts_ref[:, 1:2]

    def body(c, carry):
        slot = lax.rem(base + c, 2)
        copy(b, start0, c, slot).wait()

        @pl.when(c + 1 < n_chunks)
        def _():
            copy(b, start0, c + 1, 1 - slot).start()

        @pl.when(jnp.logical_and(c + 1 == n_chunks, jnp.logical_not(is_last)))
        def _():
            copy(nxt_b, nxt_start, 0, 1 - slot).start()

        rows = zbuf[slot]
        j = (start0 + c * COMBINE_ROWS + lax.broadcasted_iota(jnp.int32, (1, COMBINE_ROWS), 1)).astype(F32)
        seg = jnp.where(j >= first, jnp.where(j < last, 1.0, 0.0), 0.0).astype(BF16)
        hi = rows.astype(BF16)
        lo = (rows - hi.astype(F32)).astype(BF16)
        acc_ref[...] += (jnp.dot(seg, hi, preferred_element_type=F32)
                         + jnp.dot(seg, lo, preferred_element_type=F32))
        return carry

    lax.fori_loop(0, n_chunks, body, 0)
    cnt_ref[0] = base + n_chunks
    o_ref[...] = x1_ref[...] + g_ref[...] * (_rms(acc_ref[...]) * w_ref[...])


def _combine(starts, ts2, z, x1, mod3, npost_ffn, n_slot, tm):
    bsz, n_tok, d = x1.shape
    grid_spec = pltpu.PrefetchScalarGridSpec(
        num_scalar_prefetch=1,
        grid=(bsz, n_tok // tm),
        in_specs=[pl.BlockSpec((None, tm, 2), lambda b, t, *_: (b, t, 0)),
                  pl.BlockSpec((None, tm, d), lambda b, t, *_: (b, t, 0)),
                  pl.BlockSpec((None, 1, d), lambda b, t, *_: (b * 6 + 5, 0, 0)),
                  pl.BlockSpec((1, d), lambda b, t, *_: (0, 0)),
                  pl.BlockSpec(memory_space=pl.ANY)],
        out_specs=pl.BlockSpec((None, tm, d), lambda b, t, *_: (b, t, 0)),
        scratch_shapes=[pltpu.VMEM((2, COMBINE_ROWS, d), F32), pltpu.VMEM((tm, d), F32),
                        pltpu.SemaphoreType.DMA((2,)), pltpu.SMEM((1,), jnp.int32)],
    )
    return pl.pallas_call(
        functools.partial(_combine_kernel, n_slot),
        grid_spec=grid_spec,
        out_shape=jax.ShapeDtypeStruct((bsz, n_tok, d), F32),
        compiler_params=_params(2),
        name="moe_combine",
    )(starts, ts2, x1, mod3, npost_ffn, z)


def _ssd_inputs(h, w_xs, w_bc, w_dt, conv_w, conv_b, dt_bias_pad, seg, tm):
    cw_x, cw_bc = conv_w[:, :SSD_INNER], conv_w[:, SSD_INNER:]
    cb_x, cb_bc = conv_b[:, :SSD_INNER], conv_b[:, SSD_INNER:]
    epi = functools.partial(_epi_conv_silu, seg)
    (xs,) = _fused_matmul([h], [(w_xs, 0)], [0], [cw_x, cb_x], [], epi, [F32], SSD_INNER, tm, 1024, "proj_xs")
    (bc,) = _fused_matmul([h], [(w_bc, 0)], [0], [cw_bc, cb_bc], [], epi, [BF16],
                          2 * SSD_GROUPS * SSD_STATE, tm, 1024, "proj_bc")
    (dt,) = _fused_matmul([h], [(w_dt, 0)], [0], [dt_bias_pad], [], _epi_softplus, [F32], LANES, tm, LANES,
                          "proj_dt")
    return xs, bc, dt


def kernel(x, c, ctx, c_ctx, w_mod, b_mod, norm_pre_mix, norm_post_mix, w_in, ssd_conv_w, ssd_conv_b,
           dt_bias, a_log, ssd_d, ssd_norm_w, w_ssd_out, sc_conv_w, w_sc_out, b_gate, w_o, norm_pre_ffn,
           norm_post_ffn, w_router, w_e1, w_e3, w_e2):
    assert w_mod.shape[0] == 1, "single-layer block"
    bsz, length, d = x.shape
    ctx_len = ctx.shape[1]
    m = bsz * length
    cap = CAPACITY_FACTOR * length // N_EXPERTS
    row = lambda v: v.reshape(1, -1)

    o_z, o_xbc, o_dt, o_scb, o_scc, o_scv = (SSD_INNER, SSD_INNER + SSD_XBC,
                                             SSD_INNER + SSD_XBC + 2 * SSD_HEADS,
                                             SSD_INNER + SSD_XBC + 2 * SSD_HEADS + SC_DIM,
                                             SSD_INNER + SSD_XBC + 2 * SSD_HEADS + 2 * SC_DIM,
                                             SSD_INNER + SSD_XBC + 2 * SSD_HEADS + 3 * SC_DIM)
    w_in0 = w_in[0]
    w_z = w_in0[:, :o_z].astype(BF16)
    w_xs = w_in0[:, o_z:o_z + SSD_INNER].astype(BF16)
    w_bc = w_in0[:, o_z + SSD_INNER:o_xbc].astype(BF16)
    w_dt = jnp.pad(w_in0[:, o_xbc:o_dt], ((0, 0), (0, LANES - 2 * SSD_HEADS))).astype(BF16)
    w_scb = w_in0[:, o_dt:o_scb].astype(BF16)
    w_scc = w_in0[:, o_scb:o_scc].astype(BF16)
    w_scv = w_in0[:, o_scc:o_scv].astype(BF16)
    w_gate = w_in0[:, o_scv:].astype(BF16)
    dt_bias_pad = jnp.pad(dt_bias[0].reshape(1, -1), ((0, 0), (0, LANES - 2 * SSD_HEADS)))
    a_neg = -jnp.exp(a_log[0])
    ii = lax.broadcasted_iota(jnp.int32, (CHUNK, CHUNK), 0)
    jj = lax.broadcasted_iota(jnp.int32, (CHUNK, CHUNK), 1)
    tri = jnp.stack([ii >= jj, ii <= jj]).astype(BF16)
    conv_w, conv_b = ssd_conv_w[0], row(ssd_conv_b[0])

    cs = jnp.concatenate([c, c_ctx[None, :], jnp.zeros((8 - bsz - 1, d), F32)], axis=0)
    mod3 = _modulation(cs, w_mod[0], b_mod[0]).reshape(8 * 6, 1, d)

    hc = _norm_mod(ctx, norm_pre_mix[0], mod3, lambda b: bsz * 6, lambda b: bsz * 6 + 1, ctx_len)
    xs_c, bc_c, dt_c = _ssd_inputs(hc.reshape(bsz * ctx_len, d), w_xs, w_bc, w_dt, conv_w, conv_b,
                                   dt_bias_pad, ctx_len, ctx_len)
    h_zero = jnp.zeros((2, bsz, SSD_GROUPS, SSD_STATE, GROUP_W), F32)
    _, states = _ssd_scan(xs_c.reshape(bsz, ctx_len, -1), bc_c.reshape(bsz, ctx_len, -1),
                          dt_c.reshape(bsz, ctx_len, -1), a_neg, tri, h_zero)

    tm = 1024
    hx = _norm_mod(x, norm_pre_mix[0], mod3, lambda b: b * 6, lambda b: b * 6 + 1, tm).reshape(m, d)
    (z,) = _fused_matmul([hx], [(w_z, 0)], [0], [], [], _epi_identity, [F32], SSD_INNER, tm, 1024, "proj_z")
    xs, bc, dt = _ssd_inputs(hx, w_xs, w_bc, w_dt, conv_w, conv_b, dt_bias_pad, GRID_W, tm)
    (sc_y,) = _fused_matmul([hx], [(w_scb, 0), (w_scc, 0), (w_scv, 0)], [0, 0, 0], [sc_conv_w[0]], [],
                            functools.partial(_epi_shortconv, GRID_W), [BF16], SC_DIM, tm, 512, "proj_sc")
    (gates,) = _fused_matmul([hx], [(w_gate, 0)], [0], [row(b_gate[0])], [], _epi_sigmoid, [F32],
                             2 * d, tm, 1024, "proj_gate")
    y2, _ = _ssd_scan(xs.reshape(bsz, length, -1), bc.reshape(bsz, length, -1), dt.reshape(bsz, length, -1),
                      a_neg, tri, states)
    d_cols = jnp.repeat(ssd_d[0], SSD_HEAD_DIM).reshape(1, -1)
    y_norm = _ssd_post(y2, xs, z, d_cols, row(ssd_norm_w[0]), 256)
    tn = 1024
    (merged,) = _fused_matmul([y_norm, sc_y], [(w_ssd_out[0].astype(BF16), 0), (w_sc_out[0].astype(BF16), 0)],
                              [0, 1], [], [(gates, 0), (gates, d // tn)], _epi_gate_merge, [BF16], d, 512, tn,
                              "mix_merge")
    w_router_pad = jnp.pad(w_router[0], ((0, 0), (0, LANES - N_EXPERTS))).astype(BF16)
    x1, h2, aff = _mix_out(merged, w_o[0].astype(BF16), x.reshape(m, d), mod3, row(norm_post_mix[0]),
                           row(norm_pre_ffn[0]), w_router_pad, length, 256)

    aff_t = jnp.transpose(aff[:, :N_EXPERTS].reshape(bsz, length, N_EXPERTS), (0, 2, 1))
    e_i = lax.broadcasted_iota(jnp.int32, (N_EXPERTS, N_EXPERTS), 0)
    e_j = lax.broadcasted_iota(jnp.int32, (N_EXPERTS, N_EXPERTS), 1)
    lists, ts = _route(aff_t, tri[1], (e_j < e_i).astype(BF16), cap)
    n_slot = N_EXPERTS * cap
    li = lists[..., :4].astype(jnp.int32)
    b_i = jnp.arange(bsz, dtype=jnp.int32)[:, None, None]
    idx = (li[..., 0] * 64 + li[..., 1] + b_i * length).reshape(bsz * N_EXPERTS, cap)
    dst = (li[..., 2] * LANES + li[..., 3] + b_i * n_slot).reshape(bsz * N_EXPERTS, cap)
    dst_ext = jnp.concatenate([bsz * n_slot + jnp.arange(cap, dtype=jnp.int32)[None, :], dst], axis=0)
    vals = (lists[..., 4] + lists[..., 5] + lists[..., 6])[..., None]
    z = _experts(idx, dst_ext, h2, w_e1[0].astype(BF16), w_e3[0].astype(BF16), w_e2[0].astype(BF16), vals,
                 bsz * n_slot + cap, 512)
    tm_c = 256
    ts = ts.reshape(bsz, length)
    end = jnp.full((bsz, 1), n_slot, F32)
    ts2 = jnp.stack([ts, jnp.concatenate([ts[:, 1:], end], axis=1)], axis=-1)
    starts = jnp.concatenate([ts[:, ::tm_c], end], axis=1).astype(jnp.int32).reshape(-1)
    return _combine(starts, ts2, z, x1.reshape(bsz, length, d), mod3, row(norm_post_ffn[0]), n_slot, tm_c)
```

```python
import functools
import math

import jax
import jax.numpy as jnp
from jax import lax
from jax.experimental import pallas as pl
from jax.experimental.pallas import tpu as pltpu

F32 = jnp.float32
BF16 = jnp.bfloat16

D_MODEL = 2048
GRID_W = 64
SSD_HEADS = 32
SSD_HEAD_DIM = 64
SSD_INNER = SSD_HEADS * SSD_HEAD_DIM
SSD_GROUPS = 4
SSD_HPG = SSD_HEADS // SSD_GROUPS
SSD_STATE = 128
CHUNK = 128
SSD_XBC = SSD_INNER + 2 * SSD_GROUPS * SSD_STATE
SC_DIM = 2048
N_EXPERTS = 16
EXPERT_FF = 2048
CAPACITY_FACTOR = 2
EPS = 1e-6
GROUP_W = SSD_HPG * SSD_HEAD_DIM

LANES = 128
V7X_VMEM_BYTES = 64 * 1024 * 1024
VMEM_LIMIT = V7X_VMEM_BYTES - 8 * 1024 * 1024


def _params(n_grid):
    return pltpu.CompilerParams(dimension_semantics=("arbitrary",) * n_grid,
                                vmem_limit_bytes=VMEM_LIMIT)


def _silu(x):
    return x * (1.0 / (1.0 + jnp.exp(-x)))


def _sigmoid(x):
    return 1.0 / (1.0 + jnp.exp(-x))


def _softplus(x):
    return jnp.maximum(x, 0.0) + jnp.log(1.0 + jnp.exp(-jnp.abs(x)))


def _rms(x):
    return x * lax.rsqrt(jnp.mean(x * x, axis=-1, keepdims=True) + EPS)


def _mod_kernel(c_ref, w_ref, b_ref, o_ref):
    a = _silu(c_ref[...]).astype(BF16)
    o_ref[...] = jnp.dot(a, w_ref[...].astype(BF16), preferred_element_type=F32) + b_ref[...]


def _modulation(cs, w_mod, b_mod):
    rows, d = cs.shape
    n = w_mod.shape[1]
    tn = 1024
    return pl.pallas_call(
        _mod_kernel,
        grid=(n // tn,),
        in_specs=[pl.BlockSpec((rows, d), lambda j: (0, 0)),
                  pl.BlockSpec((d, tn), lambda j: (0, j)),
                  pl.BlockSpec((1, tn), lambda j: (0, j))],
        out_specs=pl.BlockSpec((rows, tn), lambda j: (0, j)),
        out_shape=jax.ShapeDtypeStruct((rows, n), F32),
        compiler_params=_params(1),
        name="modulation",
    )(cs, w_mod, b_mod.reshape(1, n))


def _norm_mod_kernel(x_ref, w_ref, sc_ref, sh_ref, o_ref):
    y = _rms(x_ref[...]) * w_ref[...]
    o_ref[...] = (y * (1.0 + sc_ref[...]) + sh_ref[...]).astype(o_ref.dtype)


def _norm_mod(x, w, mod3, sh_row, sc_row, tm):
    bsz, length, d = x.shape
    return pl.pallas_call(
        _norm_mod_kernel,
        grid=(bsz, length // tm),
        in_specs=[pl.BlockSpec((None, tm, d), lambda b, i: (b, i, 0)),
                  pl.BlockSpec((1, d), lambda b, i: (0, 0)),
                  pl.BlockSpec((None, 1, d), lambda b, i: (sc_row(b), 0, 0)),
                  pl.BlockSpec((None, 1, d), lambda b, i: (sh_row(b), 0, 0))],
        out_specs=pl.BlockSpec((None, tm, d), lambda b, i: (b, i, 0)),
        out_shape=jax.ShapeDtypeStruct((bsz, length, d), BF16),
        compiler_params=_params(2),
        name="norm_mod",
    )(x, w.reshape(1, d), mod3, mod3)


def _fused_matmul(a_list, w_list, w_to_a, col_list, tile_list, epilogue, out_dtypes, n, tm, tn, name):
    m, k = a_list[0].shape
    na, nw, nc, nt = len(a_list), len(w_list), len(col_list), len(tile_list)

    def body(*refs):
        a_refs = refs[:na]
        w_refs = refs[na:na + nw]
        c_refs = refs[na + nw:na + nw + nc]
        t_refs = refs[na + nw + nc:na + nw + nc + nt]
        o_refs = refs[na + nw + nc + nt:]
        prods = [jnp.dot(a_refs[w_to_a[i]][...], w_refs[i][...], preferred_element_type=F32)
                 for i in range(nw)]
        outs = epilogue(prods, [c[...] for c in c_refs], [t[...] for t in t_refs])
        for o_ref, o in zip(o_refs, outs):
            o_ref[...] = o.astype(o_ref.dtype)

    in_specs = [pl.BlockSpec((tm, k), lambda i, j: (i, 0)) for _ in a_list]
    args = list(a_list)
    for w, off in w_list:
        in_specs.append(pl.BlockSpec((k, tn), functools.partial(lambda i, j, off: (0, j + off), off=off)))
        args.append(w)
    for c in col_list:
        in_specs.append(pl.BlockSpec((c.shape[0], tn), lambda i, j: (0, j)))
        args.append(c)
    for t, off in tile_list:
        in_specs.append(pl.BlockSpec((tm, tn), functools.partial(lambda i, j, off: (i, j + off), off=off)))
        args.append(t)
    return pl.pallas_call(
        body,
        grid=(m // tm, n // tn),
        in_specs=in_specs,
        out_specs=[pl.BlockSpec((tm, tn), lambda i, j: (i, j)) for _ in out_dtypes],
        out_shape=[jax.ShapeDtypeStruct((m, n), dt) for dt in out_dtypes],
        compiler_params=_params(2),
        name=name,
    )(*args)


def _conv3_rows(x, cw, seg):
    tm = x.shape[0]
    r = lax.broadcasted_iota(jnp.int32, x.shape, 0) & (seg - 1)
    xp = jnp.where(r == 0, 0.0, pltpu.roll(x, 1, 0))
    xn = jnp.where(r == seg - 1, 0.0, pltpu.roll(x, tm - 1, 0))
    return xp * cw[0:1, :] + x * cw[1:2, :] + xn * cw[2:3, :]


def _epi_identity(prods, cols, tiles):
    return [prods[0]]


def _epi_conv_silu(seg, prods, cols, tiles):
    return [_silu(_conv3_rows(prods[0], cols[0], seg) + cols[1])]


def _epi_softplus(prods, cols, tiles):
    return [_softplus(prods[0] + cols[0])]


def _epi_shortconv(seg, prods, cols, tiles):
    sc_b, sc_c, sc_v = prods
    return [sc_b * _conv3_rows(sc_c * sc_v, cols[0], seg)]


def _epi_sigmoid(prods, cols, tiles):
    return [_sigmoid(prods[0] + cols[0])]


def _epi_gate_merge(prods, cols, tiles):
    return [tiles[0] * prods[0] + tiles[1] * prods[1]]


def _split3(x):
    hi = x.astype(BF16)
    r1 = x - hi.astype(F32)
    mid = r1.astype(BF16)
    lo = (r1 - mid.astype(F32)).astype(BF16)
    return hi, mid, lo


def _cumsum_rows(tri, x):
    return sum(jnp.dot(tri, p, preferred_element_type=F32) for p in _split3(x))


def _cumsum_cols(x, tri):
    return sum(jnp.dot(p, tri, preferred_element_type=F32) for p in _split3(x))


def _pair(lo_mask, v, h):
    return jnp.where(lo_mask, v[:, h:h + 1], v[:, h + 1:h + 2])


def _spread(x, sel):
    return jnp.dot(jnp.concatenate(_split3(x), axis=1), sel, preferred_element_type=F32)


def _ssd_kernel(xs_ref, bc_ref, dtc_ref, dtr_ref, arow_ref, acol_ref, tri_ref, trit_ref, sel128_ref, sel64_ref,
                h0_ref, y_ref, hfin_ref, st_ref):
    q, n = CHUNK, SSD_STATE
    step = pl.program_id(2)

    @pl.when(step == 0)
    def _():
        st_ref[...] = h0_ref[...]

    tri = tri_ref[...]
    dtc = dtc_ref[...]
    dtr = dtr_ref[...]
    al_c = dtc * arow_ref[...]
    al_r = dtr * acol_ref[...]
    cs_c = _cumsum_rows(tri, al_c)
    cs_r = _cumsum_cols(al_r, trit_ref[...])
    tot_c = jnp.sum(al_c, axis=0, keepdims=True)
    cs_b = _spread(cs_c, sel128_ref[...])
    dt_b = _spread(dtc, sel64_ref[...])
    mask = tri.astype(F32) > 0.5
    lo_q = lax.broadcasted_iota(jnp.int32, (q, LANES), 1) < SSD_HEAD_DIM
    lo_1 = lax.broadcasted_iota(jnp.int32, (1, LANES), 1) < SSD_HEAD_DIM
    for gi in range(SSD_GROUPS):
        bm = bc_ref[:, gi * n:(gi + 1) * n]
        cm = bc_ref[:, (SSD_GROUPS + gi) * n:(SSD_GROUPS + gi + 1) * n]
        cb = lax.dot_general(cm, bm, (((1,), (1,)), ((), ())), preferred_element_type=F32)
        ht = st_ref[gi]
        y_off = jnp.dot(cm, ht.astype(BF16), preferred_element_type=F32)
        xw = []
        dec = []
        for k in range(SSD_HPG // 2):
            h0 = gi * SSD_HPG + 2 * k
            col = gi * GROUP_W + k * LANES
            ms = []
            for h in (h0, h0 + 1):
                seg = cs_b[:, h * LANES:(h + 1) * LANES] - cs_r[h:h + 1, :]
                lmat = jnp.exp(jnp.where(mask, seg, -1e30))
                ms.append((cb * lmat * dtr[h:h + 1, :]).astype(BF16))
            lhs = jnp.concatenate(ms, axis=1)
            xp = xs_ref[:, col:col + LANES]
            rhs = jnp.concatenate([jnp.where(lo_q, xp, 0.0).astype(BF16),
                                   jnp.where(lo_q, 0.0, xp).astype(BF16)], axis=0)
            y_diag = jnp.dot(lhs, rhs, preferred_element_type=F32)
            cs_p = jnp.where(lo_q, cs_b[:, h0 * LANES:(h0 + 1) * LANES], cs_b[:, (h0 + 1) * LANES:(h0 + 2) * LANES])
            tot_p = _pair(lo_1, tot_c, h0)
            dt_p = dt_b[:, h0 * SSD_HEAD_DIM:(h0 + 2) * SSD_HEAD_DIM]
            y_ref[:, col:col + LANES] = y_diag + y_off[:, k * LANES:(k + 1) * LANES] * jnp.exp(cs_p)
            xw.append((xp * (jnp.exp(tot_p - cs_p) * dt_p)).astype(BF16))
            dec.append(jnp.exp(tot_p))
        xw = jnp.concatenate(xw, axis=1)
        s_t = lax.dot_general(bm, xw, (((0,), (0,)), ((), ())), preferred_element_type=F32)
        st_ref[gi] = ht * jnp.concatenate(dec, axis=1) + s_t

    @pl.when(step == pl.num_programs(2) - 1)
    def _():
        hfin_ref[...] = st_ref[...]


def _ssd_scan(xs, bc, dt, a_neg, tri, h0):
    bsz, length, _ = xs.shape
    nc = length // CHUNK
    g, q, n, nh = SSD_GROUPS, CHUNK, SSD_STATE, SSD_HEADS
    dt4 = dt[:, :, :2 * nh].reshape(bsz, length, 2, nh)
    dt_col = jnp.transpose(dt4, (2, 0, 1, 3))
    dt_row = jnp.transpose(dt4, (2, 0, 3, 1))
    a_row = a_neg.reshape(2, 1, nh)
    a_col = a_neg.reshape(2, nh, 1)

    def head_select(width):
        rows = lax.broadcasted_iota(jnp.int32, (3 * nh, nh * width), 0) % nh
        cols = lax.broadcasted_iota(jnp.int32, (3 * nh, nh * width), 1) // width
        return (rows == cols).astype(BF16)

    sel128, sel64 = head_select(LANES), head_select(SSD_HEAD_DIM)

    def chunk(d, c):
        return c + d * (nc - 1 - 2 * c)

    return pl.pallas_call(
        _ssd_kernel,
        grid=(2, bsz, nc),
        in_specs=[
            pl.BlockSpec((None, q, SSD_INNER), lambda d, b, c: (b, chunk(d, c), 0)),
            pl.BlockSpec((None, q, 2 * g * n), lambda d, b, c: (b, chunk(d, c), 0)),
            pl.BlockSpec((None, None, q, nh), lambda d, b, c: (d, b, chunk(d, c), 0)),
            pl.BlockSpec((None, None, nh, q), lambda d, b, c: (d, b, 0, chunk(d, c))),
            pl.BlockSpec((None, 1, nh), lambda d, b, c: (d, 0, 0)),
            pl.BlockSpec((None, nh, 1), lambda d, b, c: (d, 0, 0)),
            pl.BlockSpec((None, q, q), lambda d, b, c: (d, 0, 0)),
            pl.BlockSpec((None, q, q), lambda d, b, c: (1 - d, 0, 0)),
            pl.BlockSpec(sel128.shape, lambda d, b, c: (0, 0)),
            pl.BlockSpec(sel64.shape, lambda d, b, c: (0, 0)),
            pl.BlockSpec((None, None, g, n, GROUP_W), lambda d, b, c: (d, b, 0, 0, 0)),
        ],
        out_specs=[
            pl.BlockSpec((None, None, q, SSD_INNER), lambda d, b, c: (d, b, chunk(d, c), 0)),
            pl.BlockSpec((None, None, g, n, GROUP_W), lambda d, b, c: (d, b, 0, 0, 0)),
        ],
        out_shape=[jax.ShapeDtypeStruct((2, bsz, length, SSD_INNER), F32),
                   jax.ShapeDtypeStruct((2, bsz, g, n, GROUP_W), F32)],
        scratch_shapes=[pltpu.VMEM((g, n, GROUP_W), F32)],
        compiler_params=_params(3),
        name="ssd_scan",
    )(xs, bc, dt_col, dt_row, a_row, a_col, tri, tri, sel128, sel64, h0)


def _ssd_post_kernel(yf_ref, yb_ref, xs_ref, z_ref, d_ref, w_ref, o_ref):
    y = (yf_ref[...] + yb_ref[...] + d_ref[...] * xs_ref[...]) * _silu(z_ref[...])
    for g in range(SSD_GROUPS):
        sl = slice(g * GROUP_W, (g + 1) * GROUP_W)
        o_ref[:, sl] = (_rms(y[:, sl]) * w_ref[:, sl]).astype(o_ref.dtype)


def _ssd_post(y2, xs, z, d_cols, norm_w, tm):
    _, bsz, length, d = y2.shape
    nt = length // tm
    return pl.pallas_call(
        _ssd_post_kernel,
        grid=(bsz, nt),
        in_specs=[pl.BlockSpec((None, None, tm, d), lambda b, i: (0, b, i, 0)),
                  pl.BlockSpec((None, None, tm, d), lambda b, i: (1, b, i, 0)),
                  pl.BlockSpec((tm, d), lambda b, i: (b * nt + i, 0)),
                  pl.BlockSpec((tm, d), lambda b, i: (b * nt + i, 0)),
                  pl.BlockSpec((1, d), lambda b, i: (0, 0)),
                  pl.BlockSpec((1, d), lambda b, i: (0, 0))],
        out_specs=pl.BlockSpec((tm, d), lambda b, i: (b * nt + i, 0)),
        out_shape=jax.ShapeDtypeStruct((bsz * length, d), BF16),
        compiler_params=_params(2),
        name="ssd_post",
    )(y2, y2, xs, z, d_cols, norm_w)


def _mix_out_kernel(m_ref, wo_ref, x_ref, gx_ref, scf_ref, shf_ref, npost_ref, npre_ref, wr_ref,
                    x1_ref, h2_ref, aff_ref):
    out = jnp.dot(m_ref[...], wo_ref[...], preferred_element_type=F32)
    x1 = x_ref[...] + gx_ref[...] * (_rms(out) * npost_ref[...])
    x1_ref[...] = x1
    h2 = (_rms(x1) * npre_ref[...]) * (1.0 + scf_ref[...]) + shf_ref[...]
    h2_ref[...] = h2
    logits = jnp.dot(h2.astype(BF16), wr_ref[...], preferred_element_type=F32)
    lane = lax.broadcasted_iota(jnp.int32, logits.shape, 1)
    logits = jnp.where(lane < N_EXPERTS, logits, -1e30)
    e = jnp.exp(logits - jnp.max(logits, axis=-1, keepdims=True))
    aff_ref[...] = e / jnp.sum(e, axis=-1, keepdims=True)


def _mix_out(merged, w_o, x2d, mod3, npost, npre, w_router_pad, length, tm):
    m, d = merged.shape
    nt = length // tm
    row = lambda kind: (lambda i: ((i // nt) * 6 + kind, 0, 0))
    return pl.pallas_call(
        _mix_out_kernel,
        grid=(m // tm,),
        in_specs=[pl.BlockSpec((tm, d), lambda i: (i, 0)),
                  pl.BlockSpec((d, d), lambda i: (0, 0)),
                  pl.BlockSpec((tm, d), lambda i: (i, 0)),
                  pl.BlockSpec((None, 1, d), row(2)),
                  pl.BlockSpec((None, 1, d), row(4)),
                  pl.BlockSpec((None, 1, d), row(3)),
                  pl.BlockSpec((1, d), lambda i: (0, 0)),
                  pl.BlockSpec((1, d), lambda i: (0, 0)),
                  pl.BlockSpec((d, LANES), lambda i: (0, 0))],
        out_specs=[pl.BlockSpec((tm, d), lambda i: (i, 0)),
                   pl.BlockSpec((tm, d), lambda i: (i, 0)),
                   pl.BlockSpec((tm, LANES), lambda i: (i, 0))],
        out_shape=[jax.ShapeDtypeStruct((m, d), F32),
                   jax.ShapeDtypeStruct((m, d), F32),
                   jax.ShapeDtypeStruct((m, LANES), F32)],
        compiler_params=_params(1),
        name="mix_out_router",
    )(merged, w_o, x2d, mod3, mod3, mod3, npost, npre, w_router_pad)


def _route_select(cap, aff_ref, triu_ref, low_ref, pos_ref, slot_ref, ts_ref):
    aff = aff_ref[...]
    e_cnt, n_tok = aff.shape

    def count(m):
        return jnp.sum(jnp.where(m, 1.0, 0.0), axis=1, keepdims=True)

    def search_bits(_, carry):
        lo, hi = carry
        mid = lo + ((hi - lo + 1) >> 1)
        ok = count(aff >= pltpu.bitcast(mid, F32)) >= cap
        return jnp.where(ok, mid, lo), jnp.where(ok, hi, mid - 1)

    lo0 = jnp.zeros((e_cnt, 1), jnp.int32)
    hi0 = jnp.full((e_cnt, 1), 0x7F800000, jnp.int32)
    lo_b, _ = lax.fori_loop(0, 32, search_bits, (lo0, hi0))

    def search_mid(_, carry):
        lo, hi = carry
        mid = 0.5 * (lo + hi)
        ok = count(aff >= mid) >= cap
        return jnp.where(ok, mid, lo), jnp.where(ok, hi, mid)

    lo_f, _ = lax.fori_loop(0, 24, search_mid, (pltpu.bitcast(lo_b, F32), pltpu.bitcast(lo_b + 1, F32)))
    thr = jnp.min(jnp.where(aff >= lo_f, aff, jnp.inf), axis=1, keepdims=True)
    need_f = cap - count(aff > thr)
    triu = triu_ref[...]
    carry_eq = jnp.zeros((e_cnt, 1), F32)
    carry_sel = jnp.zeros((e_cnt, 1), F32)
    for k in range(n_tok // LANES):
        sl = slice(k * LANES, (k + 1) * LANES)
        aff_k = aff[:, sl]
        eq_f = jnp.where(aff_k == thr, 1.0, 0.0)
        eq_incl = jnp.dot(eq_f.astype(BF16), triu, preferred_element_type=F32) + carry_eq
        sel_k = (aff_k > thr) | ((aff_k == thr) & (eq_incl <= need_f))
        sel_f = jnp.where(sel_k, 1.0, 0.0)
        sel_b = sel_f.astype(BF16)
        sel_incl = jnp.dot(sel_b, triu, preferred_element_type=F32) + carry_sel
        excl = sel_incl - sel_f
        pos_ref[:, sl] = jnp.where(sel_k, excl.astype(jnp.int32), -1)
        tok_start = jnp.sum(excl, axis=0, keepdims=True)
        ts_ref[:, sl] = tok_start
        slot_ref[:, sl] = tok_start + jnp.dot(low_ref[...], sel_b, preferred_element_type=F32)
        carry_eq = eq_incl[:, LANES - 1:LANES]
        carry_sel = sel_incl[:, LANES - 1:LANES]


def _route_kernel(cap, aff_ref, triu_ref, low_ref, list_ref, ts_ref, pos_ref, slot_ref):
    e = pl.program_id(1)

    @pl.when(e == 0)
    def _():
        _route_select(cap, aff_ref, triu_ref, low_ref, pos_ref, slot_ref, ts_ref)

    n_tok = aff_ref.shape[1]
    pos = pos_ref[pl.ds(e, 1), :]
    slot = slot_ref[pl.ds(e, 1), :]
    aff = aff_ref[pl.ds(e, 1), :]
    hit = lax.broadcasted_iota(jnp.int32, (cap, n_tok), 0) == pos
    tok = lax.broadcasted_iota(jnp.int32, (1, n_tok), 1)
    slot_hi = jnp.floor(slot * (1.0 / LANES))
    a_hi = aff.astype(BF16).astype(F32)
    a_mid = (aff - a_hi).astype(BF16).astype(F32)
    rows = [(tok >> 6).astype(F32), (tok & 63).astype(F32), slot_hi, slot - slot_hi * LANES,
            a_hi, a_mid, aff - a_hi - a_mid]
    sub = lax.broadcasted_iota(jnp.int32, (8, n_tok), 0)
    vals = jnp.zeros((8, n_tok), F32)
    for i, r in enumerate(rows):
        vals = jnp.where(sub == i, r, vals)
    list_ref[...] = lax.dot_general(jnp.where(hit, 1.0, 0.0).astype(BF16), vals.astype(BF16),
                                    (((1,), (1,)), ((), ())), preferred_element_type=F32)


def _route(aff_t, triu, low, cap):
    bsz, e_cnt, n_tok = aff_t.shape
    return pl.pallas_call(
        functools.partial(_route_kernel, cap),
        grid=(bsz, e_cnt),
        in_specs=[pl.BlockSpec((None, e_cnt, n_tok), lambda b, e: (b, 0, 0)),
                  pl.BlockSpec((LANES, LANES), lambda b, e: (0, 0)),
                  pl.BlockSpec((e_cnt, e_cnt), lambda b, e: (0, 0))],
        out_specs=[pl.BlockSpec((None, None, cap, 8), lambda b, e: (b, e, 0, 0)),
                   pl.BlockSpec((None, 1, n_tok), lambda b, e: (b, 0, 0))],
        out_shape=[jax.ShapeDtypeStruct((bsz, e_cnt, cap, 8), F32),
                   jax.ShapeDtypeStruct((bsz, 1, n_tok), F32)],
        scratch_shapes=[pltpu.VMEM((e_cnt, n_tok), jnp.int32), pltpu.VMEM((e_cnt, n_tok), F32)],
        compiler_params=_params(2),
        name="route_topk",
    )(aff_t, triu, low)


MXU_COLS = 256


def _expert_kernel(cap, n_e, n_f, n_pairs, idx_ref, dst_ref, h_hbm, w1_ref, w3_ref, w2_ref, val_ref, z_hbm,
                   gbuf, obuf, xg_ref, acc_ref, gsem, ssem):
    f, j = pl.program_id(2), pl.program_id(3)
    pair = pl.program_id(0) * n_e + pl.program_id(1)
    nxt = jnp.minimum(pair + 1, n_pairs - 1)
    tf, d = w2_ref.shape
    rows = cap // n_f
    s = 2 * f + j
    part = lax.rem(s, n_f)

    def gather(p, t, q, r):
        src = idx_ref[(2 * p + t) * cap + q * rows + r]
        return pltpu.make_async_copy(h_hbm.at[pl.ds(src, 1), :], gbuf.at[t, q, pl.ds(r, 1), :], gsem.at[0])

    def scatter(pos, t, q, r):
        dst = dst_ref[(2 * pos + t) * cap + q * rows + r]
        return pltpu.make_async_copy(obuf.at[t, q, pl.ds(r, 1), :], z_hbm.at[pl.ds(dst, 1), :], ssem.at[t])

    def for_all_rows(fn):
        def body(r, carry):
            for t in range(2):
                for q in range(n_f):
                    fn(t, q, r)
            return carry

        lax.fori_loop(0, rows, body, 0)

    @pl.when(jnp.logical_and(pair == 0, s == 0))
    def _():
        obuf[...] = jnp.zeros_like(obuf)
        for_all_rows(lambda t, q, r: gather(0, t, q, r).start())

    @pl.when(s == 0)
    def _():
        pltpu.make_async_copy(gbuf, gbuf, gsem.at[0]).wait()
        for t in range(2):
            for q in range(n_f):
                xg_ref[t, q * rows:(q + 1) * rows, :] = gbuf[t, q].astype(BF16)

    @pl.when(f == 0)
    def _():
        acc_ref[j] = jnp.zeros((cap, d), F32)

    @pl.when(s < n_f)
    def _():
        for t in range(2):
            for r in range(rows):
                scatter(pair, t, part, r).start()

    @pl.when(s >= n_f)
    def _():
        for t in range(2):
            for r in range(rows):
                gather(nxt, t, part, r).start()

    xg = xg_ref[j]
    hs = []
    for c in range(tf // MXU_COLS):
        sl = slice(c * MXU_COLS, (c + 1) * MXU_COLS)
        g = jnp.dot(xg, w1_ref[:, sl].astype(BF16), preferred_element_type=F32)
        u = jnp.dot(xg, w3_ref[:, sl].astype(BF16), preferred_element_type=F32)
        hs.append((_silu(g) * u).astype(BF16))
    h = jnp.concatenate(hs, axis=1)
    for c in range(d // MXU_COLS):
        sl = slice(c * MXU_COLS, (c + 1) * MXU_COLS)
        acc_ref[j, :, sl] += jnp.dot(h, w2_ref[:, sl].astype(BF16), preferred_element_type=F32)

    @pl.when(f == n_f - 1)
    def _():
        pltpu.make_async_copy(obuf.at[j], obuf.at[j], ssem.at[j]).wait()
        for q in range(n_f):
            sl = slice(q * rows, (q + 1) * rows)
            obuf[j, q] = acc_ref[j, sl, :] * val_ref[sl, :]

        @pl.when(jnp.logical_and(pair == n_pairs - 1, j == 1))
        def _():
            for_all_rows(lambda t, q, r: scatter(n_pairs, t, q, r).start())
            for t in range(2):
                pltpu.make_async_copy(obuf.at[t], obuf.at[t], ssem.at[t]).wait()
            pltpu.make_async_copy(gbuf, gbuf, gsem.at[0]).wait()


def _experts(idx, dst_ext, h2, w1, w3, w2, vals, n_rows_out, tf):
    bsz, e_cnt, cap, _ = vals.shape
    d = h2.shape[-1]
    n_f = w1.shape[-1] // tf
    n_pairs = bsz // 2 * e_cnt
    grid_spec = pltpu.PrefetchScalarGridSpec(
        num_scalar_prefetch=2,
        grid=(bsz // 2, e_cnt, n_f, 2),
        in_specs=[pl.BlockSpec(memory_space=pl.ANY),
                  pl.BlockSpec((None, d, tf), lambda bh, e, f, j, *_: (e, 0, f)),
                  pl.BlockSpec((None, d, tf), lambda bh, e, f, j, *_: (e, 0, f)),
                  pl.BlockSpec((None, tf, d), lambda bh, e, f, j, *_: (e, f, 0)),
                  pl.BlockSpec((None, None, cap, 1), lambda bh, e, f, j, *_: (2 * bh + j, e, 0, 0))],
        out_specs=pl.BlockSpec(memory_space=pl.ANY),
        scratch_shapes=[pltpu.VMEM((2, n_f, cap // n_f, d), F32), pltpu.VMEM((2, n_f, cap // n_f, d), F32),
                        pltpu.VMEM((2, cap, d), BF16), pltpu.VMEM((2, cap, d), F32),
                        pltpu.SemaphoreType.DMA((1,)), pltpu.SemaphoreType.DMA((2,))],
    )
    return pl.pallas_call(
        functools.partial(_expert_kernel, cap, e_cnt, n_f, n_pairs),
        grid_spec=grid_spec,
        out_shape=jax.ShapeDtypeStruct((n_rows_out, d), F32),
        compiler_params=_params(4),
        name="moe_experts",
    )(idx.reshape(-1), dst_ext.reshape(-1), h2, w1, w3, w2, vals)


COMBINE_ROWS = 256


def _combine_kernel(n_slot, starts_ref, ts_ref, x1_ref, g_ref, w_ref, z_hbm, o_ref, zbuf, acc_ref, sem, cnt_ref):
    b, t = pl.program_id(0), pl.program_id(1)
    nb, nt = pl.num_programs(0), pl.num_programs(1)
    shift = COMBINE_ROWS.bit_length() - 1

    def tile_chunks(bb, tt):
        s0 = starts_ref[bb * (nt + 1) + tt]
        s1 = starts_ref[bb * (nt + 1) + tt + 1]
        start = lax.shift_left(lax.shift_right_logical(s0, 3), 3)
        return start, jnp.maximum(lax.shift_right_logical(s1 - start + COMBINE_ROWS - 1, shift), 1)

    def copy(bb, start, c, slot):
        row = pl.multiple_of(bb * n_slot + start + c * COMBINE_ROWS, 8)
        return pltpu.make_async_copy(z_hbm.at[pl.ds(row, COMBINE_ROWS), :], zbuf.at[slot], sem.at[slot])

    start0, n_chunks = tile_chunks(b, t)
    is_first = jnp.logical_and(b == 0, t == 0)
    is_last = jnp.logical_and(b == nb - 1, t == nt - 1)
    wrap = t == nt - 1
    nxt_b = jnp.where(is_last, b, jnp.where(wrap, b + 1, b))
    nxt_t = jnp.where(is_last, t, jnp.where(wrap, 0, t + 1))
    nxt_start, _ = tile_chunks(nxt_b, nxt_t)

    @pl.when(is_first)
    def _():
        cnt_ref[0] = 0
        copy(b, start0, 0, 0).start()

    acc_ref[...] = jnp.zeros_like(acc_ref)
    base = cnt_ref[0]
    first = ts_ref[:, 0:1]
    last = ts_ref[:, 1:2]

    def body(c, carry):
        slot = lax.rem(base + c, 2)
        copy(b, start0, c, slot).wait()

        @pl.when(c + 1 < n_chunks)
        def _():
            copy(b, start0, c + 1, 1 - slot).start()

        @pl.when(jnp.logical_and(c + 1 == n_chunks, jnp.logical_not(is_last)))
        def _():
            copy(nxt_b, nxt_start, 0, 1 - slot).start()

        rows = zbuf[slot]
        j = (start0 + c * COMBINE_ROWS + lax.broadcasted_iota(jnp.int32, (1, COMBINE_ROWS), 1)).astype(F32)
        seg = jnp.where(j >= first, jnp.where(j < last, 1.0, 0.0), 0.0).astype(BF16)
        hi = rows.astype(BF16)
        lo = (rows - hi.astype(F32)).astype(BF16)
        acc_ref[...] += (jnp.dot(seg, hi, preferred_element_type=F32)
                         + jnp.dot(seg, lo, preferred_element_type=F32))
        return carry

    lax.fori_loop(0, n_chunks, body, 0)
    cnt_ref[0] = base + n_chunks
    o_ref[...] = x1_ref[...] + g_ref[...] * (_rms(acc_ref[...]) * w_ref[...])


def _combine(starts, ts2, z, x1, mod3, npost_ffn, n_slot, tm):
    bsz, n_tok, d = x1.shape
    grid_spec = pltpu.PrefetchScalarGridSpec(
        num_scalar_prefetch=1,
        grid=(bsz, n_tok // tm),
        in_specs=[pl.BlockSpec((None, tm, 2), lambda b, t, *_: (b, t, 0)),
                  pl.BlockSpec((None, tm, d), lambda b, t, *_: (b, t, 0)),
                  pl.BlockSpec((None, 1, d), lambda b, t, *_: (b * 6 + 5, 0, 0)),
                  pl.BlockSpec((1, d), lambda b, t, *_: (0, 0)),
                  pl.BlockSpec(memory_space=pl.ANY)],
        out_specs=pl.BlockSpec((None, tm, d), lambda b, t, *_: (b, t, 0)),
        scratch_shapes=[pltpu.VMEM((2, COMBINE_ROWS, d), F32), pltpu.VMEM((tm, d), F32),
                        pltpu.SemaphoreType.DMA((2,)), pltpu.SMEM((1,), jnp.int32)],
    )
    return pl.pallas_call(
        functools.partial(_combine_kernel, n_slot),
        grid_spec=grid_spec,
        out_shape=jax.ShapeDtypeStruct((bsz, n_tok, d), F32),
        compiler_params=_params(2),
        name="moe_combine",
    )(starts, ts2, x1, mod3, npost_ffn, z)


def _ssd_inputs(h, w_xs, w_bc, w_dt, conv_w, conv_b, dt_bias_pad, seg, tm):
    cw_x, cw_bc = conv_w[:, :SSD_INNER], conv_w[:, SSD_INNER:]
    cb_x, cb_bc = conv_b[:, :SSD_INNER], conv_b[:, SSD_INNER:]
    epi = functools.partial(_epi_conv_silu, seg)
    (xs,) = _fused_matmul([h], [(w_xs, 0)], [0], [cw_x, cb_x], [], epi, [F32], SSD_INNER, tm, 1024, "proj_xs")
    (bc,) = _fused_matmul([h], [(w_bc, 0)], [0], [cw_bc, cb_bc], [], epi, [BF16],
                          2 * SSD_GROUPS * SSD_STATE, tm, 1024, "proj_bc")
    (dt,) = _fused_matmul([h], [(w_dt, 0)], [0], [dt_bias_pad], [], _epi_softplus, [F32], LANES, tm, LANES,
                          "proj_dt")
    return xs, bc, dt


def kernel(x, c, ctx, c_ctx, w_mod, b_mod, norm_pre_mix, norm_post_mix, w_in, ssd_conv_w, ssd_conv_b,
           dt_bias, a_log, ssd_d, ssd_norm_w, w_ssd_out, sc_conv_w, w_sc_out, b_gate, w_o, norm_pre_ffn,
           norm_post_ffn, w_router, w_e1, w_e3, w_e2):
    assert w_mod.shape[0] == 1, "single-layer block"
    bsz, length, d = x.shape
    ctx_len = ctx.shape[1]
    m = bsz * length
    cap = CAPACITY_FACTOR * length // N_EXPERTS
    row = lambda v: v.reshape(1, -1)

    o_z, o_xbc, o_dt, o_scb, o_scc, o_scv = (SSD_INNER, SSD_INNER + SSD_XBC,
                                             SSD_INNER + SSD_XBC + 2 * SSD_HEADS,
                                             SSD_INNER + SSD_XBC + 2 * SSD_HEADS + SC_DIM,
                                             SSD_INNER + SSD_XBC + 2 * SSD_HEADS + 2 * SC_DIM,
                                             SSD_INNER + SSD_XBC + 2 * SSD_HEADS + 3 * SC_DIM)
    w_in0 = w_in[0]
    w_z = w_in0[:, :o_z].astype(BF16)
    w_xs = w_in0[:, o_z:o_z + SSD_INNER].astype(BF16)
    w_bc = w_in0[:, o_z + SSD_INNER:o_xbc].astype(BF16)
    w_dt = jnp.pad(w_in0[:, o_xbc:o_dt], ((0, 0), (0, LANES - 2 * SSD_HEADS))).astype(BF16)
    w_scb = w_in0[:, o_dt:o_scb].astype(BF16)
    w_scc = w_in0[:, o_scb:o_scc].astype(BF16)
    w_scv = w_in0[:, o_scc:o_scv].astype(BF16)
    w_gate = w_in0[:, o_scv:].astype(BF16)
    dt_bias_pad = jnp.pad(dt_bias[0].reshape(1, -1), ((0, 0), (0, LANES - 2 * SSD_HEADS)))
    a_neg = -jnp.exp(a_log[0])
    ii = lax.broadcasted_iota(jnp.int32, (CHUNK, CHUNK), 0)
    jj = lax.broadcasted_iota(jnp.int32, (CHUNK, CHUNK), 1)
    tri = jnp.stack([ii >= jj, ii <= jj]).astype(BF16)
    conv_w, conv_b = ssd_conv_w[0], row(ssd_conv_b[0])

    cs = jnp.concatenate([c, c_ctx[None, :], jnp.zeros((8 - bsz - 1, d), F32)], axis=0)
    mod3 = _modulation(cs, w_mod[0], b_mod[0]).reshape(8 * 6, 1, d)

    hc = _norm_mod(ctx, norm_pre_mix[0], mod3, lambda b: bsz * 6, lambda b: bsz * 6 + 1, ctx_len)
    xs_c, bc_c, dt_c = _ssd_inputs(hc.reshape(bsz * ctx_len, d), w_xs, w_bc, w_dt, conv_w, conv_b,
                                   dt_bias_pad, ctx_len, ctx_len)
    h_zero = jnp.zeros((2, bsz, SSD_GROUPS, SSD_STATE, GROUP_W), F32)
    _, states = _ssd_scan(xs_c.reshape(bsz, ctx_len, -1), bc_c.reshape(bsz, ctx_len, -1),
                          dt_c.reshape(bsz, ctx_len, -1), a_neg, tri, h_zero)

    tm = 1024
    hx = _norm_mod(x, norm_pre_mix[0], mod3, lambda b: b * 6, lambda b: b * 6 + 1, tm).reshape(m, d)
    (z,) = _fused_matmul([hx], [(w_z, 0)], [0], [], [], _epi_identity, [F32], SSD_INNER, tm, 1024, "proj_z")
    xs, bc, dt = _ssd_inputs(hx, w_xs, w_bc, w_dt, conv_w, conv_b, dt_bias_pad, GRID_W, tm)
    (sc_y,) = _fused_matmul([hx], [(w_scb, 0), (w_scc, 0), (w_scv, 0)], [0, 0, 0], [sc_conv_w[0]], [],
                            functools.partial(_epi_shortconv, GRID_W), [BF16], SC_DIM, tm, 512, "proj_sc")
    (gates,) = _fused_matmul([hx], [(w_gate, 0)], [0], [row(b_gate[0])], [], _epi_sigmoid, [F32],
                             2 * d, tm, 1024, "proj_gate")
    y2, _ = _ssd_scan(xs.reshape(bsz, length, -1), bc.reshape(bsz, length, -1), dt.reshape(bsz, length, -1),
                      a_neg, tri, states)
    d_cols = jnp.repeat(ssd_d[0], SSD_HEAD_DIM).reshape(1, -1)
    y_norm = _ssd_post(y2, xs, z, d_cols, row(ssd_norm_w[0]), 256)
    tn = 1024
    (merged,) = _fused_matmul([y_norm, sc_y], [(w_ssd_out[0].astype(BF16), 0), (w_sc_out[0].astype(BF16), 0)],
                              [0, 1], [], [(gates, 0), (gates, d // tn)], _epi_gate_merge, [BF16], d, 512, tn,
                              "mix_merge")
    w_router_pad = jnp.pad(w_router[0], ((0, 0), (0, LANES - N_EXPERTS))).astype(BF16)
    x1, h2, aff = _mix_out(merged, w_o[0].astype(BF16), x.reshape(m, d), mod3, row(norm_post_mix[0]),
                           row(norm_pre_ffn[0]), w_router_pad, length, 256)

    aff_t = jnp.transpose(aff[:, :N_EXPERTS].reshape(bsz, length, N_EXPERTS), (0, 2, 1))
    e_i = lax.broadcasted_iota(jnp.int32, (N_EXPERTS, N_EXPERTS), 0)
    e_j = lax.broadcasted_iota(jnp.int32, (N_EXPERTS, N_EXPERTS), 1)
    lists, ts = _route(aff_t, tri[1], (e_j < e_i).astype(BF16), cap)
    n_slot = N_EXPERTS * cap
    li = lists[..., :4].astype(jnp.int32)
    b_i = jnp.arange(bsz, dtype=jnp.int32)[:, None, None]
    def pair_order(a):
        return jnp.transpose(a.reshape(bsz // 2, 2, N_EXPERTS, cap), (0, 2, 1, 3)).reshape(-1, cap)

    idx = pair_order(li[..., 0] * 64 + li[..., 1] + b_i * length)
    dst = pair_order(li[..., 2] * LANES + li[..., 3] + b_i * n_slot)
    n_pad = 2 * cap
    dst_ext = jnp.concatenate([bsz * n_slot + jnp.arange(n_pad, dtype=jnp.int32).reshape(2, cap), dst], axis=0)
    vals = (lists[..., 4] + lists[..., 5] + lists[..., 6])[..., None]
    z = _experts(idx, dst_ext, h2, w_e1[0], w_e3[0], w_e2[0], vals, bsz * n_slot + n_pad, MXU_COLS)
    tm_c = 256
    ts = ts.reshape(bsz, length)
    end = jnp.full((bsz, 1), n_slot, F32)
    ts2 = jnp.stack([ts, jnp.concatenate([ts[:, 1:], end], axis=1)], axis=-1)
    starts = jnp.concatenate([ts[:, ::tm_c], end], axis=1).astype(jnp.int32).reshape(-1)
    return _combine(starts, ts2, z, x1.reshape(bsz, length, d), mod3, row(norm_post_ffn[0]), n_slot, tm_c)
```

```python
import functools
import math

import jax
import jax.numpy as jnp
from jax import lax
from jax.experimental import pallas as pl
from jax.experimental.pallas import tpu as pltpu

F32 = jnp.float32
BF16 = jnp.bfloat16

D_MODEL = 2048
GRID_W = 64
SSD_HEADS = 32
SSD_HEAD_DIM = 64
SSD_INNER = SSD_HEADS * SSD_HEAD_DIM
SSD_GROUPS = 4
SSD_HPG = SSD_HEADS // SSD_GROUPS
SSD_STATE = 128
CHUNK = 128
SSD_XBC = SSD_INNER + 2 * SSD_GROUPS * SSD_STATE
SC_DIM = 2048
N_EXPERTS = 16
EXPERT_FF = 2048
CAPACITY_FACTOR = 2
EPS = 1e-6
GROUP_W = SSD_HPG * SSD_HEAD_DIM

LANES = 128
V7X_VMEM_BYTES = 64 * 1024 * 1024
VMEM_LIMIT = V7X_VMEM_BYTES - 8 * 1024 * 1024


def _params(n_grid):
    return pltpu.CompilerParams(dimension_semantics=("arbitrary",) * n_grid,
                                vmem_limit_bytes=VMEM_LIMIT)


def _silu(x):
    return x * (1.0 / (1.0 + jnp.exp(-x)))


def _sigmoid(x):
    return 1.0 / (1.0 + jnp.exp(-x))


def _softplus(x):
    return jnp.maximum(x, 0.0) + jnp.log(1.0 + jnp.exp(-jnp.abs(x)))


def _rms(x):
    return x * lax.rsqrt(jnp.mean(x * x, axis=-1, keepdims=True) + EPS)


def _mod_kernel(c_ref, w_ref, b_ref, o_ref):
    a = _silu(c_ref[...]).astype(BF16)
    o_ref[...] = jnp.dot(a, w_ref[...].astype(BF16), preferred_element_type=F32) + b_ref[...]


def _modulation(cs, w_mod, b_mod):
    rows, d = cs.shape
    n = w_mod.shape[1]
    tn = 1024
    return pl.pallas_call(
        _mod_kernel,
        grid=(n // tn,),
        in_specs=[pl.BlockSpec((rows, d), lambda j: (0, 0)),
                  pl.BlockSpec((d, tn), lambda j: (0, j)),
                  pl.BlockSpec((1, tn), lambda j: (0, j))],
        out_specs=pl.BlockSpec((rows, tn), lambda j: (0, j)),
        out_shape=jax.ShapeDtypeStruct((rows, n), F32),
        compiler_params=_params(1),
        name="modulation",
    )(cs, w_mod, b_mod.reshape(1, n))


def _norm_mod_kernel(x_ref, w_ref, sc_ref, sh_ref, o_ref):
    y = _rms(x_ref[...]) * w_ref[...]
    o_ref[...] = (y * (1.0 + sc_ref[...]) + sh_ref[...]).astype(o_ref.dtype)


def _norm_mod(x, w, mod3, sh_row, sc_row, tm):
    bsz, length, d = x.shape
    return pl.pallas_call(
        _norm_mod_kernel,
        grid=(bsz, length // tm),
        in_specs=[pl.BlockSpec((None, tm, d), lambda b, i: (b, i, 0)),
                  pl.BlockSpec((1, d), lambda b, i: (0, 0)),
                  pl.BlockSpec((None, 1, d), lambda b, i: (sc_row(b), 0, 0)),
                  pl.BlockSpec((None, 1, d), lambda b, i: (sh_row(b), 0, 0))],
        out_specs=pl.BlockSpec((None, tm, d), lambda b, i: (b, i, 0)),
        out_shape=jax.ShapeDtypeStruct((bsz, length, d), BF16),
        compiler_params=_params(2),
        name="norm_mod",
    )(x, w.reshape(1, d), mod3, mod3)


def _fused_matmul(a_list, w_list, w_to_a, col_list, tile_list, epilogue, out_dtypes, n, tm, tn, name):
    m, k = a_list[0].shape
    na, nw, nc, nt = len(a_list), len(w_list), len(col_list), len(tile_list)

    def body(*refs):
        a_refs = refs[:na]
        w_refs = refs[na:na + nw]
        c_refs = refs[na + nw:na + nw + nc]
        t_refs = refs[na + nw + nc:na + nw + nc + nt]
        o_refs = refs[na + nw + nc + nt:]
        prods = [jnp.dot(a_refs[w_to_a[i]][...], w_refs[i][...], preferred_element_type=F32)
                 for i in range(nw)]
        outs = epilogue(prods, [c[...] for c in c_refs], [t[...] for t in t_refs])
        for o_ref, o in zip(o_refs, outs):
            o_ref[...] = o.astype(o_ref.dtype)

    in_specs = [pl.BlockSpec((tm, k), lambda i, j: (i, 0)) for _ in a_list]
    args = list(a_list)
    for w, off in w_list:
        in_specs.append(pl.BlockSpec((k, tn), functools.partial(lambda i, j, off: (0, j + off), off=off)))
        args.append(w)
    for c in col_list:
        in_specs.append(pl.BlockSpec((c.shape[0], tn), lambda i, j: (0, j)))
        args.append(c)
    for t, off in tile_list:
        in_specs.append(pl.BlockSpec((tm, tn), functools.partial(lambda i, j, off: (i, j + off), off=off)))
        args.append(t)
    return pl.pallas_call(
        body,
        grid=(m // tm, n // tn),
        in_specs=in_specs,
        out_specs=[pl.BlockSpec((tm, tn), lambda i, j: (i, j)) for _ in out_dtypes],
        out_shape=[jax.ShapeDtypeStruct((m, n), dt) for dt in out_dtypes],
        compiler_params=_params(2),
        name=name,
    )(*args)


def _conv3_rows(x, cw, seg):
    tm = x.shape[0]
    r = lax.broadcasted_iota(jnp.int32, x.shape, 0) & (seg - 1)
    xp = jnp.where(r == 0, 0.0, pltpu.roll(x, 1, 0))
    xn = jnp.where(r == seg - 1, 0.0, pltpu.roll(x, tm - 1, 0))
    return xp * cw[0:1, :] + x * cw[1:2, :] + xn * cw[2:3, :]


def _epi_identity(prods, cols, tiles):
    return [prods[0]]


def _epi_conv_silu(seg, prods, cols, tiles):
    return [_silu(_conv3_rows(prods[0], cols[0], seg) + cols[1])]


def _epi_softplus(prods, cols, tiles):
    return [_softplus(prods[0] + cols[0])]


def _epi_shortconv(seg, prods, cols, tiles):
    sc_b, sc_c, sc_v = prods
    return [sc_b * _conv3_rows(sc_c * sc_v, cols[0], seg)]


def _epi_sigmoid(prods, cols, tiles):
    return [_sigmoid(prods[0] + cols[0])]


def _epi_gate_merge(prods, cols, tiles):
    return [tiles[0] * prods[0] + tiles[1] * prods[1]]


def _split3(x):
    hi = x.astype(BF16)
    r1 = x - hi.astype(F32)
    mid = r1.astype(BF16)
    lo = (r1 - mid.astype(F32)).astype(BF16)
    return hi, mid, lo


def _cumsum_rows(tri, x):
    return sum(jnp.dot(tri, p, preferred_element_type=F32) for p in _split3(x))


def _cumsum_cols(x, tri):
    return sum(jnp.dot(p, tri, preferred_element_type=F32) for p in _split3(x))


def _pair(lo_mask, v, h):
    return jnp.where(lo_mask, v[:, h:h + 1], v[:, h + 1:h + 2])


def _spread(x, sel):
    return jnp.dot(jnp.concatenate(_split3(x), axis=1), sel, preferred_element_type=F32)


def _ssd_kernel(post, xs_ref, bc_ref, dtc_ref, dtr_ref, arow_ref, acol_ref, tri_ref, trit_ref, sel128_ref,
                sel64_ref, h0_ref, *rest):
    if post:
        yo_ref, z_ref, dcol_ref, nw_ref, y_ref, hfin_ref, st_ref = rest
    else:
        y_ref, hfin_ref, st_ref = rest
    q, n = CHUNK, SSD_STATE
    step = pl.program_id(2)

    @pl.when(step == 0)
    def _():
        st_ref[...] = h0_ref[...]

    tri = tri_ref[...]
    dtc = dtc_ref[...]
    dtr = dtr_ref[...]
    al_c = dtc * arow_ref[...]
    al_r = dtr * acol_ref[...]
    cs_c = _cumsum_rows(tri, al_c)
    cs_r = _cumsum_cols(al_r, trit_ref[...])
    tot_c = jnp.sum(al_c, axis=0, keepdims=True)
    cs_b = _spread(cs_c, sel128_ref[...])
    dt_b = _spread(dtc, sel64_ref[...])
    mask = tri.astype(F32) > 0.5
    lo_q = lax.broadcasted_iota(jnp.int32, (q, LANES), 1) < SSD_HEAD_DIM
    lo_1 = lax.broadcasted_iota(jnp.int32, (1, LANES), 1) < SSD_HEAD_DIM
    for gi in range(SSD_GROUPS):
        bm = bc_ref[:, gi * n:(gi + 1) * n]
        cm = bc_ref[:, (SSD_GROUPS + gi) * n:(SSD_GROUPS + gi + 1) * n]
        cb = lax.dot_general(cm, bm, (((1,), (1,)), ((), ())), preferred_element_type=F32)
        ht = st_ref[gi]
        y_off = jnp.dot(cm, ht.astype(BF16), preferred_element_type=F32)
        xw = []
        dec = []
        ys = []
        for k in range(SSD_HPG // 2):
            h0 = gi * SSD_HPG + 2 * k
            col = gi * GROUP_W + k * LANES
            ms = []
            for h in (h0, h0 + 1):
                seg = cs_b[:, h * LANES:(h + 1) * LANES] - cs_r[h:h + 1, :]
                lmat = jnp.exp(jnp.where(mask, seg, -1e30))
                ms.append((cb * lmat * dtr[h:h + 1, :]).astype(BF16))
            lhs = jnp.concatenate(ms, axis=1)
            xp = xs_ref[:, col:col + LANES]
            rhs = jnp.concatenate([jnp.where(lo_q, xp, 0.0).astype(BF16),
                                   jnp.where(lo_q, 0.0, xp).astype(BF16)], axis=0)
            y_diag = jnp.dot(lhs, rhs, preferred_element_type=F32)
            cs_p = jnp.where(lo_q, cs_b[:, h0 * LANES:(h0 + 1) * LANES], cs_b[:, (h0 + 1) * LANES:(h0 + 2) * LANES])
            tot_p = _pair(lo_1, tot_c, h0)
            dt_p = dt_b[:, h0 * SSD_HEAD_DIM:(h0 + 2) * SSD_HEAD_DIM]
            y_k = y_diag + y_off[:, k * LANES:(k + 1) * LANES] * jnp.exp(cs_p)
            if post:
                cs = slice(col, col + LANES)
                ys.append((y_k + yo_ref[:, cs] + dcol_ref[:, cs] * xp) * _silu(z_ref[:, cs]))
            else:
                y_ref[:, col:col + LANES] = y_k
            xw.append((xp * (jnp.exp(tot_p - cs_p) * dt_p)).astype(BF16))
            dec.append(jnp.exp(tot_p))
        xw = jnp.concatenate(xw, axis=1)
        s_t = lax.dot_general(bm, xw, (((0,), (0,)), ((), ())), preferred_element_type=F32)
        st_ref[gi] = ht * jnp.concatenate(dec, axis=1) + s_t
        if post:
            ssq = sum(jnp.sum(v * v, axis=-1, keepdims=True) for v in ys)
            inv = lax.rsqrt(ssq * (1.0 / GROUP_W) + EPS)
            for k, v in enumerate(ys):
                cs = slice(gi * GROUP_W + k * LANES, gi * GROUP_W + (k + 1) * LANES)
                y_ref[:, cs] = (v * inv * nw_ref[:, cs]).astype(y_ref.dtype)

    @pl.when(step == pl.num_programs(2) - 1)
    def _():
        hfin_ref[...] = st_ref[...]


def _ssd_scan(xs, bc, dt, a_neg, tri, h0, post=None):
    bsz, length, _ = xs.shape
    nc = length // CHUNK
    g, q, n, nh = SSD_GROUPS, CHUNK, SSD_STATE, SSD_HEADS
    dt4 = dt[:, :, :2 * nh].reshape(bsz, length, 2, nh)
    dt_col = jnp.transpose(dt4, (2, 0, 1, 3))
    dt_row = jnp.transpose(dt4, (2, 0, 3, 1))
    a_row = a_neg.reshape(2, 1, nh)
    a_col = a_neg.reshape(2, nh, 1)

    def head_select(width):
        rows = lax.broadcasted_iota(jnp.int32, (3 * nh, nh * width), 0) % nh
        cols = lax.broadcasted_iota(jnp.int32, (3 * nh, nh * width), 1) // width
        return (rows == cols).astype(BF16)

    sel128, sel64 = head_select(LANES), head_select(SSD_HEAD_DIM)

    def call(d0, nd, extra, y_dtype):
        def dd(d):
            return d + d0

        def chunk(d, c):
            return c + dd(d) * (nc - 1 - 2 * c)

        extra_specs = []
        if extra:
            extra_specs = [pl.BlockSpec((None, q, SSD_INNER), lambda d, b, c: (b, chunk(d, c), 0)),
                           pl.BlockSpec((None, q, SSD_INNER), lambda d, b, c: (b, chunk(d, c), 0)),
                           pl.BlockSpec((1, SSD_INNER), lambda d, b, c: (0, 0)),
                           pl.BlockSpec((1, SSD_INNER), lambda d, b, c: (0, 0))]
        return pl.pallas_call(
            functools.partial(_ssd_kernel, bool(extra)),
            grid=(nd, bsz, nc),
            in_specs=[
                pl.BlockSpec((None, q, SSD_INNER), lambda d, b, c: (b, chunk(d, c), 0)),
                pl.BlockSpec((None, q, 2 * g * n), lambda d, b, c: (b, chunk(d, c), 0)),
                pl.BlockSpec((None, None, q, nh), lambda d, b, c: (dd(d), b, chunk(d, c), 0)),
                pl.BlockSpec((None, None, nh, q), lambda d, b, c: (dd(d), b, 0, chunk(d, c))),
                pl.BlockSpec((None, 1, nh), lambda d, b, c: (dd(d), 0, 0)),
                pl.BlockSpec((None, nh, 1), lambda d, b, c: (dd(d), 0, 0)),
                pl.BlockSpec((None, q, q), lambda d, b, c: (dd(d), 0, 0)),
                pl.BlockSpec((None, q, q), lambda d, b, c: (1 - dd(d), 0, 0)),
                pl.BlockSpec(sel128.shape, lambda d, b, c: (0, 0)),
                pl.BlockSpec(sel64.shape, lambda d, b, c: (0, 0)),
                pl.BlockSpec((None, None, g, n, GROUP_W), lambda d, b, c: (dd(d), b, 0, 0, 0)),
            ] + extra_specs,
            out_specs=[
                pl.BlockSpec((None, None, q, SSD_INNER), lambda d, b, c: (d, b, chunk(d, c), 0)),
                pl.BlockSpec((None, None, g, n, GROUP_W), lambda d, b, c: (d, b, 0, 0, 0)),
            ],
            out_shape=[jax.ShapeDtypeStruct((nd, bsz, length, SSD_INNER), y_dtype),
                       jax.ShapeDtypeStruct((nd, bsz, g, n, GROUP_W), F32)],
            scratch_shapes=[pltpu.VMEM((g, n, GROUP_W), F32)],
            compiler_params=_params(3),
            name="ssd_scan",
        )(xs, bc, dt_col, dt_row, a_row, a_col, tri, tri, sel128, sel64, h0, *extra)

    if post is None:
        return call(0, 2, (), F32)
    y_fwd, _ = call(0, 1, (), F32)
    y_out, _ = call(1, 1, (y_fwd[0],) + tuple(post), BF16)
    return y_out[0]


def _mix_out_kernel(m_ref, wo_ref, x_ref, gx_ref, scf_ref, shf_ref, npost_ref, npre_ref, wr_ref,
                    x1_ref, h2_ref, aff_ref):
    out = jnp.dot(m_ref[...], wo_ref[...], preferred_element_type=F32)
    x1 = x_ref[...] + gx_ref[...] * (_rms(out) * npost_ref[...])
    x1_ref[...] = x1
    h2 = (_rms(x1) * npre_ref[...]) * (1.0 + scf_ref[...]) + shf_ref[...]
    h2_ref[...] = h2
    logits = jnp.dot(h2.astype(BF16), wr_ref[...], preferred_element_type=F32)
    lane = lax.broadcasted_iota(jnp.int32, logits.shape, 1)
    logits = jnp.where(lane < N_EXPERTS, logits, -1e30)
    e = jnp.exp(logits - jnp.max(logits, axis=-1, keepdims=True))
    aff_ref[...] = e / jnp.sum(e, axis=-1, keepdims=True)


def _mix_out(merged, w_o, x2d, mod3, npost, npre, w_router_pad, length, tm):
    m, d = merged.shape
    nt = length // tm
    row = lambda kind: (lambda i: ((i // nt) * 6 + kind, 0, 0))
    return pl.pallas_call(
        _mix_out_kernel,
        grid=(m // tm,),
        in_specs=[pl.BlockSpec((tm, d), lambda i: (i, 0)),
                  pl.BlockSpec((d, d), lambda i: (0, 0)),
                  pl.BlockSpec((tm, d), lambda i: (i, 0)),
                  pl.BlockSpec((None, 1, d), row(2)),
                  pl.BlockSpec((None, 1, d), row(4)),
                  pl.BlockSpec((None, 1, d), row(3)),
                  pl.BlockSpec((1, d), lambda i: (0, 0)),
                  pl.BlockSpec((1, d), lambda i: (0, 0)),
                  pl.BlockSpec((d, LANES), lambda i: (0, 0))],
        out_specs=[pl.BlockSpec((tm, d), lambda i: (i, 0)),
                   pl.BlockSpec((tm, d), lambda i: (i, 0)),
                   pl.BlockSpec((tm, LANES), lambda i: (i, 0))],
        out_shape=[jax.ShapeDtypeStruct((m, d), F32),
                   jax.ShapeDtypeStruct((m, d), F32),
                   jax.ShapeDtypeStruct((m, LANES), F32)],
        compiler_params=_params(1),
        name="mix_out_router",
    )(merged, w_o, x2d, mod3, mod3, mod3, npost, npre, w_router_pad)


def _route_select(cap, aff_ref, triu_ref, low_ref, pos_ref, slot_ref, ts_ref):
    aff = aff_ref[...]
    e_cnt, n_tok = aff.shape

    def count(m):
        return jnp.sum(jnp.where(m, 1.0, 0.0), axis=1, keepdims=True)

    def search_bits(_, carry):
        lo, hi = carry
        mid = lo + ((hi - lo + 1) >> 1)
        ok = count(aff >= pltpu.bitcast(mid, F32)) >= cap
        return jnp.where(ok, mid, lo), jnp.where(ok, hi, mid - 1)

    lo0 = jnp.zeros((e_cnt, 1), jnp.int32)
    hi0 = jnp.full((e_cnt, 1), 0x7F800000, jnp.int32)
    lo_b, _ = lax.fori_loop(0, 32, search_bits, (lo0, hi0))

    def search_mid(_, carry):
        lo, hi = carry
        mid = 0.5 * (lo + hi)
        ok = count(aff >= mid) >= cap
        return jnp.where(ok, mid, lo), jnp.where(ok, hi, mid)

    lo_f, _ = lax.fori_loop(0, 24, search_mid, (pltpu.bitcast(lo_b, F32), pltpu.bitcast(lo_b + 1, F32)))
    thr = jnp.min(jnp.where(aff >= lo_f, aff, jnp.inf), axis=1, keepdims=True)
    need_f = cap - count(aff > thr)
    triu = triu_ref[...]
    carry_eq = jnp.zeros((e_cnt, 1), F32)
    carry_sel = jnp.zeros((e_cnt, 1), F32)
    for k in range(n_tok // LANES):
        sl = slice(k * LANES, (k + 1) * LANES)
        aff_k = aff[:, sl]
        eq_f = jnp.where(aff_k == thr, 1.0, 0.0)
        eq_incl = jnp.dot(eq_f.astype(BF16), triu, preferred_element_type=F32) + carry_eq
        sel_k = (aff_k > thr) | ((aff_k == thr) & (eq_incl <= need_f))
        sel_f = jnp.where(sel_k, 1.0, 0.0)
        sel_b = sel_f.astype(BF16)
        sel_incl = jnp.dot(sel_b, triu, preferred_element_type=F32) + carry_sel
        excl = sel_incl - sel_f
        pos_ref[:, sl] = jnp.where(sel_k, excl.astype(jnp.int32), -1)
        tok_start = jnp.sum(excl, axis=0, keepdims=True)
        ts_ref[:, sl] = tok_start
        slot_ref[:, sl] = tok_start + jnp.dot(low_ref[...], sel_b, preferred_element_type=F32)
        carry_eq = eq_incl[:, LANES - 1:LANES]
        carry_sel = sel_incl[:, LANES - 1:LANES]


def _route_kernel(cap, aff_ref, triu_ref, low_ref, list_ref, ts_ref, pos_ref, slot_ref):
    e = pl.program_id(1)

    @pl.when(e == 0)
    def _():
        _route_select(cap, aff_ref, triu_ref, low_ref, pos_ref, slot_ref, ts_ref)

    n_tok = aff_ref.shape[1]
    pos = pos_ref[pl.ds(e, 1), :]
    slot = slot_ref[pl.ds(e, 1), :]
    aff = aff_ref[pl.ds(e, 1), :]
    hit = lax.broadcasted_iota(jnp.int32, (cap, n_tok), 0) == pos
    tok = lax.broadcasted_iota(jnp.int32, (1, n_tok), 1)
    slot_hi = jnp.floor(slot * (1.0 / LANES))
    a_hi = aff.astype(BF16).astype(F32)
    a_mid = (aff - a_hi).astype(BF16).astype(F32)
    rows = [(tok >> 6).astype(F32), (tok & 63).astype(F32), slot_hi, slot - slot_hi * LANES,
            a_hi, a_mid, aff - a_hi - a_mid]
    sub = lax.broadcasted_iota(jnp.int32, (8, n_tok), 0)
    vals = jnp.zeros((8, n_tok), F32)
    for i, r in enumerate(rows):
        vals = jnp.where(sub == i, r, vals)
    list_ref[...] = lax.dot_general(jnp.where(hit, 1.0, 0.0).astype(BF16), vals.astype(BF16),
                                    (((1,), (1,)), ((), ())), preferred_element_type=F32)


def _route(aff_t, triu, low, cap):
    bsz, e_cnt, n_tok = aff_t.shape
    return pl.pallas_call(
        functools.partial(_route_kernel, cap),
        grid=(bsz, e_cnt),
        in_specs=[pl.BlockSpec((None, e_cnt, n_tok), lambda b, e: (b, 0, 0)),
                  pl.BlockSpec((LANES, LANES), lambda b, e: (0, 0)),
                  pl.BlockSpec((e_cnt, e_cnt), lambda b, e: (0, 0))],
        out_specs=[pl.BlockSpec((None, None, cap, 8), lambda b, e: (b, e, 0, 0)),
                   pl.BlockSpec((None, 1, n_tok), lambda b, e: (b, 0, 0))],
        out_shape=[jax.ShapeDtypeStruct((bsz, e_cnt, cap, 8), F32),
                   jax.ShapeDtypeStruct((bsz, 1, n_tok), F32)],
        scratch_shapes=[pltpu.VMEM((e_cnt, n_tok), jnp.int32), pltpu.VMEM((e_cnt, n_tok), F32)],
        compiler_params=_params(2),
        name="route_topk",
    )(aff_t, triu, low)


MXU_COLS = 256


def _expert_kernel(cap, n_e, n_f, n_pairs, tf, idx_ref, dst_ref, h_hbm, w1_hbm, w3_hbm, w2_hbm, val_ref, z_hbm,
                   gbuf, obuf, xg_ref, acc_ref, w1_buf, w3_buf, w2_buf, gsem, ssem, wsem):
    e, f, j = pl.program_id(1), pl.program_id(2), pl.program_id(3)
    pair = pl.program_id(0) * n_e + e
    nxt = jnp.minimum(pair + 1, n_pairs - 1)
    d = h_hbm.shape[1]
    rows = cap // n_f
    s = 2 * f + j
    part = lax.rem(s, n_f)
    wslot = lax.rem(pair * n_f + f, 2)

    def weight_copies(ee, ff, slot):
        col = pl.multiple_of(ff * tf, tf)
        return (pltpu.make_async_copy(w1_hbm.at[ee, :, pl.ds(col, tf)], w1_buf.at[slot], wsem.at[slot, 0]),
                pltpu.make_async_copy(w3_hbm.at[ee, :, pl.ds(col, tf)], w3_buf.at[slot], wsem.at[slot, 1]),
                pltpu.make_async_copy(w2_hbm.at[ee, pl.ds(col, tf), :], w2_buf.at[slot], wsem.at[slot, 2]))

    def gather(p, t, q, r):
        src = idx_ref[(2 * p + t) * cap + q * rows + r]
        return pltpu.make_async_copy(h_hbm.at[pl.ds(src, 1), :], gbuf.at[t, q, pl.ds(r, 1), :], gsem.at[0])

    def scatter(pos, t, q, r):
        dst = dst_ref[(2 * pos + t) * cap + q * rows + r]
        return pltpu.make_async_copy(obuf.at[t, q, pl.ds(r, 1), :], z_hbm.at[pl.ds(dst, 1), :], ssem.at[t])

    def for_all_rows(fn):
        def body(r, carry):
            for t in range(2):
                for q in range(n_f):
                    fn(t, q, r)
            return carry

        lax.fori_loop(0, rows, body, 0)

    @pl.when(jnp.logical_and(pair == 0, s == 0))
    def _():
        for c in weight_copies(e, f, wslot):
            c.start()
        obuf[...] = jnp.zeros_like(obuf)
        for_all_rows(lambda t, q, r: gather(0, t, q, r).start())

    @pl.when(j == 0)
    def _():
        last_f = f == n_f - 1
        e_nxt = jnp.where(last_f, jnp.where(e == n_e - 1, 0, e + 1), e)
        f_nxt = jnp.where(last_f, 0, f + 1)

        @pl.when(jnp.logical_not(jnp.logical_and(pair == n_pairs - 1, last_f)))
        def _():
            for c in weight_copies(e_nxt, f_nxt, 1 - wslot):
                c.start()

        for c in weight_copies(e, f, wslot):
            c.wait()

    @pl.when(s == 0)
    def _():
        pltpu.make_async_copy(gbuf, gbuf, gsem.at[0]).wait()
        for t in range(2):
            for q in range(n_f):
                xg_ref[t, q * rows:(q + 1) * rows, :] = gbuf[t, q].astype(BF16)

    @pl.when(f == 0)
    def _():
        acc_ref[j] = jnp.zeros((cap, d), F32)

    @pl.when(s < n_f)
    def _():
        for t in range(2):
            for r in range(rows):
                scatter(pair, t, part, r).start()

    @pl.when(s >= n_f)
    def _():
        for t in range(2):
            for r in range(rows):
                gather(nxt, t, part, r).start()

    xg = xg_ref[j]
    hs = []
    for c in range(tf // MXU_COLS):
        sl = slice(c * MXU_COLS, (c + 1) * MXU_COLS)
        g = jnp.dot(xg, w1_buf[wslot, :, sl].astype(BF16), preferred_element_type=F32)
        u = jnp.dot(xg, w3_buf[wslot, :, sl].astype(BF16), preferred_element_type=F32)
        hs.append((_silu(g) * u).astype(BF16))
    h = jnp.concatenate(hs, axis=1)
    for c in range(d // MXU_COLS):
        sl = slice(c * MXU_COLS, (c + 1) * MXU_COLS)
        acc_ref[j, :, sl] += jnp.dot(h, w2_buf[wslot, :, sl].astype(BF16), preferred_element_type=F32)

    @pl.when(f == n_f - 1)
    def _():
        pltpu.make_async_copy(obuf.at[j], obuf.at[j], ssem.at[j]).wait()
        for q in range(n_f):
            sl = slice(q * rows, (q + 1) * rows)
            obuf[j, q] = acc_ref[j, sl, :] * val_ref[sl, :]

        @pl.when(jnp.logical_and(pair == n_pairs - 1, j == 1))
        def _():
            for_all_rows(lambda t, q, r: scatter(n_pairs, t, q, r).start())
            for t in range(2):
                pltpu.make_async_copy(obuf.at[t], obuf.at[t], ssem.at[t]).wait()
            pltpu.make_async_copy(gbuf, gbuf, gsem.at[0]).wait()


def _experts(idx, dst_ext, h2, w1, w3, w2, vals, n_rows_out, tf):
    bsz, e_cnt, cap, _ = vals.shape
    d = h2.shape[-1]
    n_f = w1.shape[-1] // tf
    n_pairs = bsz // 2 * e_cnt
    grid_spec = pltpu.PrefetchScalarGridSpec(
        num_scalar_prefetch=2,
        grid=(bsz // 2, e_cnt, n_f, 2),
        in_specs=[pl.BlockSpec(memory_space=pl.ANY),
                  pl.BlockSpec(memory_space=pl.ANY),
                  pl.BlockSpec(memory_space=pl.ANY),
                  pl.BlockSpec(memory_space=pl.ANY),
                  pl.BlockSpec((None, None, cap, 1), lambda bh, e, f, j, *_: (2 * bh + j, e, 0, 0))],
        out_specs=pl.BlockSpec(memory_space=pl.ANY),
        scratch_shapes=[pltpu.VMEM((2, n_f, cap // n_f, d), F32), pltpu.VMEM((2, n_f, cap // n_f, d), F32),
                        pltpu.VMEM((2, cap, d), BF16), pltpu.VMEM((2, cap, d), F32),
                        pltpu.VMEM((2, d, tf), F32), pltpu.VMEM((2, d, tf), F32), pltpu.VMEM((2, tf, d), F32),
                        pltpu.SemaphoreType.DMA((1,)), pltpu.SemaphoreType.DMA((2,)),
                        pltpu.SemaphoreType.DMA((2, 3))],
    )
    return pl.pallas_call(
        functools.partial(_expert_kernel, cap, e_cnt, n_f, n_pairs, tf),
        grid_spec=grid_spec,
        out_shape=jax.ShapeDtypeStruct((n_rows_out, d), F32),
        compiler_params=_params(4),
        name="moe_experts",
    )(idx.reshape(-1), dst_ext.reshape(-1), h2, w1, w3, w2, vals)


COMBINE_ROWS = 256


def _combine_kernel(n_slot, starts_ref, ts_ref, x1_ref, g_ref, w_ref, z_hbm, o_ref, zbuf, acc_ref, sem, cnt_ref):
    b, t = pl.program_id(0), pl.program_id(1)
    nb, nt = pl.num_programs(0), pl.num_programs(1)
    shift = COMBINE_ROWS.bit_length() - 1

    def tile_chunks(bb, tt):
        s0 = starts_ref[bb * (nt + 1) + tt]
        s1 = starts_ref[bb * (nt + 1) + tt + 1]
        start = lax.shift_left(lax.shift_right_logical(s0, 3), 3)
        return start, jnp.maximum(lax.shift_right_logical(s1 - start + COMBINE_ROWS - 1, shift), 1)

    def copy(bb, start, c, slot):
        row = pl.multiple_of(bb * n_slot + start + c * COMBINE_ROWS, 8)
        return pltpu.make_async_copy(z_hbm.at[pl.ds(row, COMBINE_ROWS), :], zbuf.at[slot], sem.at[slot])

    start0, n_chunks = tile_chunks(b, t)
    is_first = jnp.logical_and(b == 0, t == 0)
    is_last = jnp.logical_and(b == nb - 1, t == nt - 1)
    wrap = t == nt - 1
    nxt_b = jnp.where(is_last, b, jnp.where(wrap, b + 1, b))
    nxt_t = jnp.where(is_last, t, jnp.where(wrap, 0, t + 1))
    nxt_start, _ = tile_chunks(nxt_b, nxt_t)

    @pl.when(is_first)
    def _():
        cnt_ref[0] = 0
        copy(b, start0, 0, 0).start()

    acc_ref[...] = jnp.zeros_like(acc_ref)
    base = cnt_ref[0]
    first = ts_ref[:, 0:1]
    last = ts_ref[:, 1:2]

    def body(c, carry):
        slot = lax.rem(base + c, 2)
        copy(b, start0, c, slot).wait()

        @pl.when(c + 1 < n_chunks)
        def _():
            copy(b, start0, c + 1, 1 - slot).start()

        @pl.when(jnp.logical_and(c + 1 == n_chunks, jnp.logical_not(is_last)))
        def _():
            copy(nxt_b, nxt_start, 0, 1 - slot).start()

        rows = zbuf[slot]
        j = (start0 + c * COMBINE_ROWS + lax.broadcasted_iota(jnp.int32, (1, COMBINE_ROWS), 1)).astype(F32)
        seg = jnp.where(j >= first, jnp.where(j < last, 1.0, 0.0), 0.0).astype(BF16)
        hi = rows.astype(BF16)
        lo = (rows - hi.astype(F32)).astype(BF16)
        acc_ref[...] += (jnp.dot(seg, hi, preferred_element_type=F32)
                         + jnp.dot(seg, lo, preferred_element_type=F32))
        return carry

    lax.fori_loop(0, n_chunks, body, 0)
    cnt_ref[0] = base + n_chunks
    o_ref[...] = x1_ref[...] + g_ref[...] * (_rms(acc_ref[...]) * w_ref[...])


def _combine(starts, ts2, z, x1, mod3, npost_ffn, n_slot, tm):
    bsz, n_tok, d = x1.shape
    grid_spec = pltpu.PrefetchScalarGridSpec(
        num_scalar_prefetch=1,
        grid=(bsz, n_tok // tm),
        in_specs=[pl.BlockSpec((None, tm, 2), lambda b, t, *_: (b, t, 0)),
                  pl.BlockSpec((None, tm, d), lambda b, t, *_: (b, t, 0)),
                  pl.BlockSpec((None, 1, d), lambda b, t, *_: (b * 6 + 5, 0, 0)),
                  pl.BlockSpec((1, d), lambda b, t, *_: (0, 0)),
                  pl.BlockSpec(memory_space=pl.ANY)],
        out_specs=pl.BlockSpec((None, tm, d), lambda b, t, *_: (b, t, 0)),
        scratch_shapes=[pltpu.VMEM((2, COMBINE_ROWS, d), F32), pltpu.VMEM((tm, d), F32),
                        pltpu.SemaphoreType.DMA((2,)), pltpu.SMEM((1,), jnp.int32)],
    )
    return pl.pallas_call(
        functools.partial(_combine_kernel, n_slot),
        grid_spec=grid_spec,
        out_shape=jax.ShapeDtypeStruct((bsz, n_tok, d), F32),
        compiler_params=_params(2),
        name="moe_combine",
    )(starts, ts2, x1, mod3, npost_ffn, z)


def _ssd_inputs(h, w_xs, w_bc, w_dt, conv_w, conv_b, dt_bias_pad, seg, tm):
    cw_x, cw_bc = conv_w[:, :SSD_INNER], conv_w[:, SSD_INNER:]
    cb_x, cb_bc = conv_b[:, :SSD_INNER], conv_b[:, SSD_INNER:]
    epi = functools.partial(_epi_conv_silu, seg)
    (xs,) = _fused_matmul([h], [(w_xs, 0)], [0], [cw_x, cb_x], [], epi, [F32], SSD_INNER, tm, 1024, "proj_xs")
    (bc,) = _fused_matmul([h], [(w_bc, 0)], [0], [cw_bc, cb_bc], [], epi, [BF16],
                          2 * SSD_GROUPS * SSD_STATE, tm, 1024, "proj_bc")
    (dt,) = _fused_matmul([h], [(w_dt, 0)], [0], [dt_bias_pad], [], _epi_softplus, [F32], LANES, tm, LANES,
                          "proj_dt")
    return xs, bc, dt


def kernel(x, c, ctx, c_ctx, w_mod, b_mod, norm_pre_mix, norm_post_mix, w_in, ssd_conv_w, ssd_conv_b,
           dt_bias, a_log, ssd_d, ssd_norm_w, w_ssd_out, sc_conv_w, w_sc_out, b_gate, w_o, norm_pre_ffn,
           norm_post_ffn, w_router, w_e1, w_e3, w_e2):
    assert w_mod.shape[0] == 1, "single-layer block"
    bsz, length, d = x.shape
    ctx_len = ctx.shape[1]
    m = bsz * length
    cap = CAPACITY_FACTOR * length // N_EXPERTS
    row = lambda v: v.reshape(1, -1)

    o_z, o_xbc, o_dt, o_scb, o_scc, o_scv = (SSD_INNER, SSD_INNER + SSD_XBC,
                                             SSD_INNER + SSD_XBC + 2 * SSD_HEADS,
                                             SSD_INNER + SSD_XBC + 2 * SSD_HEADS + SC_DIM,
                                             SSD_INNER + SSD_XBC + 2 * SSD_HEADS + 2 * SC_DIM,
                                             SSD_INNER + SSD_XBC + 2 * SSD_HEADS + 3 * SC_DIM)
    w_in0 = w_in[0]
    w_z = w_in0[:, :o_z].astype(BF16)
    w_xs = w_in0[:, o_z:o_z + SSD_INNER].astype(BF16)
    w_bc = w_in0[:, o_z + SSD_INNER:o_xbc].astype(BF16)
    w_dt = jnp.pad(w_in0[:, o_xbc:o_dt], ((0, 0), (0, LANES - 2 * SSD_HEADS))).astype(BF16)
    w_scb = w_in0[:, o_dt:o_scb].astype(BF16)
    w_scc = w_in0[:, o_scb:o_scc].astype(BF16)
    w_scv = w_in0[:, o_scc:o_scv].astype(BF16)
    w_gate = w_in0[:, o_scv:].astype(BF16)
    dt_bias_pad = jnp.pad(dt_bias[0].reshape(1, -1), ((0, 0), (0, LANES - 2 * SSD_HEADS)))
    a_neg = -jnp.exp(a_log[0])
    ii = lax.broadcasted_iota(jnp.int32, (CHUNK, CHUNK), 0)
    jj = lax.broadcasted_iota(jnp.int32, (CHUNK, CHUNK), 1)
    tri = jnp.stack([ii >= jj, ii <= jj]).astype(BF16)
    conv_w, conv_b = ssd_conv_w[0], row(ssd_conv_b[0])

    cs = jnp.concatenate([c, c_ctx[None, :], jnp.zeros((8 - bsz - 1, d), F32)], axis=0)
    mod3 = _modulation(cs, w_mod[0], b_mod[0]).reshape(8 * 6, 1, d)

    hc = _norm_mod(ctx, norm_pre_mix[0], mod3, lambda b: bsz * 6, lambda b: bsz * 6 + 1, ctx_len)
    xs_c, bc_c, dt_c = _ssd_inputs(hc.reshape(bsz * ctx_len, d), w_xs, w_bc, w_dt, conv_w, conv_b,
                                   dt_bias_pad, ctx_len, ctx_len)
    h_zero = jnp.zeros((2, bsz, SSD_GROUPS, SSD_STATE, GROUP_W), F32)
    _, states = _ssd_scan(xs_c.reshape(bsz, ctx_len, -1), bc_c.reshape(bsz, ctx_len, -1),
                          dt_c.reshape(bsz, ctx_len, -1), a_neg, tri, h_zero)

    tm = 1024
    hx = _norm_mod(x, norm_pre_mix[0], mod3, lambda b: b * 6, lambda b: b * 6 + 1, tm).reshape(m, d)
    (z,) = _fused_matmul([hx], [(w_z, 0)], [0], [], [], _epi_identity, [F32], SSD_INNER, tm, 1024, "proj_z")
    xs, bc, dt = _ssd_inputs(hx, w_xs, w_bc, w_dt, conv_w, conv_b, dt_bias_pad, GRID_W, tm)
    (sc_y,) = _fused_matmul([hx], [(w_scb, 0), (w_scc, 0), (w_scv, 0)], [0, 0, 0], [sc_conv_w[0]], [],
                            functools.partial(_epi_shortconv, GRID_W), [BF16], SC_DIM, tm, 512, "proj_sc")
    (gates,) = _fused_matmul([hx], [(w_gate, 0)], [0], [row(b_gate[0])], [], _epi_sigmoid, [F32],
                             2 * d, tm, 1024, "proj_gate")
    d_cols = jnp.repeat(ssd_d[0], SSD_HEAD_DIM).reshape(1, -1)
    y_norm = _ssd_scan(xs.reshape(bsz, length, -1), bc.reshape(bsz, length, -1), dt.reshape(bsz, length, -1),
                       a_neg, tri, states, post=(z.reshape(bsz, length, -1), d_cols, row(ssd_norm_w[0])))
    y_norm = y_norm.reshape(m, -1)
    tn = 1024
    (merged,) = _fused_matmul([y_norm, sc_y], [(w_ssd_out[0].astype(BF16), 0), (w_sc_out[0].astype(BF16), 0)],
                              [0, 1], [], [(gates, 0), (gates, d // tn)], _epi_gate_merge, [BF16], d, 512, tn,
                              "mix_merge")
    w_router_pad = jnp.pad(w_router[0], ((0, 0), (0, LANES - N_EXPERTS))).astype(BF16)
    x1, h2, aff = _mix_out(merged, w_o[0].astype(BF16), x.reshape(m, d), mod3, row(norm_post_mix[0]),
                           row(norm_pre_ffn[0]), w_router_pad, length, 256)

    aff_t = jnp.transpose(aff[:, :N_EXPERTS].reshape(bsz, length, N_EXPERTS), (0, 2, 1))
    e_i = lax.broadcasted_iota(jnp.int32, (N_EXPERTS, N_EXPERTS), 0)
    e_j = lax.broadcasted_iota(jnp.int32, (N_EXPERTS, N_EXPERTS), 1)
    lists, ts = _route(aff_t, tri[1], (e_j < e_i).astype(BF16), cap)
    n_slot = N_EXPERTS * cap
    li = lists[..., :4].astype(jnp.int32)
    b_i = jnp.arange(bsz, dtype=jnp.int32)[:, None, None]
    def pair_order(a):
        return jnp.transpose(a.reshape(bsz // 2, 2, N_EXPERTS, cap), (0, 2, 1, 3)).reshape(-1, cap)

    idx = pair_order(li[..., 0] * 64 + li[..., 1] + b_i * length)
    dst = pair_order(li[..., 2] * LANES + li[..., 3] + b_i * n_slot)
    n_pad = 2 * cap
    dst_ext = jnp.concatenate([bsz * n_slot + jnp.arange(n_pad, dtype=jnp.int32).reshape(2, cap), dst], axis=0)
    vals = (lists[..., 4] + lists[..., 5] + lists[..., 6])[..., None]
    z = _experts(idx, dst_ext, h2, w_e1[0], w_e3[0], w_e2[0], vals, bsz * n_slot + n_pad, MXU_COLS)
    tm_c = 256
    ts = ts.reshape(bsz, length)
    end = jnp.full((bsz, 1), n_slot, F32)
    ts2 = jnp.stack([ts, jnp.concatenate([ts[:, 1:], end], axis=1)], axis=-1)
    starts = jnp.concatenate([ts[:, ::tm_c], end], axis=1).astype(jnp.int32).reshape(-1)
    return _combine(starts, ts2, z, x1.reshape(bsz, length, d), mod3, row(norm_post_ffn[0]), n_slot, tm_c)
```

```python
import functools
import math

import jax
import jax.numpy as jnp
from jax import lax
from jax.experimental import pallas as pl
from jax.experimental.pallas import tpu as pltpu

F32 = jnp.float32
BF16 = jnp.bfloat16

D_MODEL = 2048
GRID_W = 64
SSD_HEADS = 32
SSD_HEAD_DIM = 64
SSD_INNER = SSD_HEADS * SSD_HEAD_DIM
SSD_GROUPS = 4
SSD_HPG = SSD_HEADS // SSD_GROUPS
SSD_STATE = 128
CHUNK = 128
SSD_XBC = SSD_INNER + 2 * SSD_GROUPS * SSD_STATE
SC_DIM = 2048
N_EXPERTS = 16
EXPERT_FF = 2048
CAPACITY_FACTOR = 2
EPS = 1e-6
GROUP_W = SSD_HPG * SSD_HEAD_DIM

LANES = 128
MXU_COLS = 256
V7X_VMEM_BYTES = 64 * 1024 * 1024
VMEM_LIMIT = V7X_VMEM_BYTES - 8 * 1024 * 1024


def _params(n_grid):
    return pltpu.CompilerParams(dimension_semantics=("arbitrary",) * n_grid,
                                vmem_limit_bytes=VMEM_LIMIT)


def _silu(x):
    return x * (1.0 / (1.0 + jnp.exp(-x)))


def _sigmoid(x):
    return 1.0 / (1.0 + jnp.exp(-x))


def _softplus(x):
    return jnp.maximum(x, 0.0) + jnp.log(1.0 + jnp.exp(-jnp.abs(x)))


def _rms(x):
    return x * lax.rsqrt(jnp.mean(x * x, axis=-1, keepdims=True) + EPS)


def _mod_kernel(c_ref, w_ref, b_ref, o_ref):
    a = _silu(c_ref[...]).astype(BF16)
    o_ref[...] = jnp.dot(a, w_ref[...].astype(BF16), preferred_element_type=F32) + b_ref[...]


def _modulation(cs, w_mod, b_mod):
    rows, d = cs.shape
    n = w_mod.shape[1]
    tn = 1024
    return pl.pallas_call(
        _mod_kernel,
        grid=(n // tn,),
        in_specs=[pl.BlockSpec((rows, d), lambda j: (0, 0)),
                  pl.BlockSpec((d, tn), lambda j: (0, j)),
                  pl.BlockSpec((1, tn), lambda j: (0, j))],
        out_specs=pl.BlockSpec((rows, tn), lambda j: (0, j)),
        out_shape=jax.ShapeDtypeStruct((rows, n), F32),
        compiler_params=_params(1),
        name="modulation",
    )(cs, w_mod, b_mod.reshape(1, n))


def _norm_mod_kernel(x_ref, w_ref, sc_ref, sh_ref, o_ref):
    y = _rms(x_ref[...]) * w_ref[...]
    o_ref[...] = (y * (1.0 + sc_ref[...]) + sh_ref[...]).astype(o_ref.dtype)


def _norm_mod(x, w, mod3, sh_row, sc_row, tm):
    bsz, length, d = x.shape
    return pl.pallas_call(
        _norm_mod_kernel,
        grid=(bsz, length // tm),
        in_specs=[pl.BlockSpec((None, tm, d), lambda b, i: (b, i, 0)),
                  pl.BlockSpec((1, d), lambda b, i: (0, 0)),
                  pl.BlockSpec((None, 1, d), lambda b, i: (sc_row(b), 0, 0)),
                  pl.BlockSpec((None, 1, d), lambda b, i: (sh_row(b), 0, 0))],
        out_specs=pl.BlockSpec((None, tm, d), lambda b, i: (b, i, 0)),
        out_shape=jax.ShapeDtypeStruct((bsz, length, d), BF16),
        compiler_params=_params(2),
        name="norm_mod",
    )(x, w.reshape(1, d), mod3, mod3)


def _fused_matmul(a_list, w_list, w_to_a, col_list, tile_list, epilogue, out_dtypes, n, tm, tn, name):
    m, k = a_list[0].shape
    na, nw, nc, nt = len(a_list), len(w_list), len(col_list), len(tile_list)
    sub = min(tn, 2 * MXU_COLS)

    def body(*refs):
        a_refs = refs[:na]
        w_refs = refs[na:na + nw]
        c_refs = refs[na + nw:na + nw + nc]
        t_refs = refs[na + nw + nc:na + nw + nc + nt]
        o_refs = refs[na + nw + nc + nt:]
        for s in range(tn // sub):
            sl = slice(s * sub, (s + 1) * sub)
            prods = [jnp.dot(a_refs[w_to_a[i]][...], w_refs[i][:, sl], preferred_element_type=F32)
                     for i in range(nw)]
            outs = epilogue(prods, [c[:, sl] for c in c_refs], [t[:, sl] for t in t_refs])
            for o_ref, o in zip(o_refs, outs):
                o_ref[:, sl] = o.astype(o_ref.dtype)

    in_specs = [pl.BlockSpec((tm, k), lambda i, j: (i, 0)) for _ in a_list]
    args = list(a_list)
    for w, off in w_list:
        in_specs.append(pl.BlockSpec((k, tn), functools.partial(lambda i, j, off: (0, j + off), off=off)))
        args.append(w)
    for c in col_list:
        in_specs.append(pl.BlockSpec((c.shape[0], tn), lambda i, j: (0, j)))
        args.append(c)
    for t, off in tile_list:
        in_specs.append(pl.BlockSpec((tm, tn), functools.partial(lambda i, j, off: (i, j + off), off=off)))
        args.append(t)
    return pl.pallas_call(
        body,
        grid=(m // tm, n // tn),
        in_specs=in_specs,
        out_specs=[pl.BlockSpec((tm, tn), lambda i, j: (i, j)) for _ in out_dtypes],
        out_shape=[jax.ShapeDtypeStruct((m, n), dt) for dt in out_dtypes],
        compiler_params=_params(2),
        name=name,
    )(*args)


def _conv3_rows(x, cw, seg):
    tm = x.shape[0]
    r = lax.broadcasted_iota(jnp.int32, x.shape, 0) & (seg - 1)
    xp = jnp.where(r == 0, 0.0, pltpu.roll(x, 1, 0))
    xn = jnp.where(r == seg - 1, 0.0, pltpu.roll(x, tm - 1, 0))
    return xp * cw[0:1, :] + x * cw[1:2, :] + xn * cw[2:3, :]


def _epi_identity(prods, cols, tiles):
    return [prods[0]]


def _epi_conv_silu(seg, prods, cols, tiles):
    return [_silu(_conv3_rows(prods[0], cols[0], seg) + cols[1])]


def _epi_softplus(prods, cols, tiles):
    return [_softplus(prods[0] + cols[0])]


def _epi_shortconv(seg, prods, cols, tiles):
    sc_b, sc_c, sc_v = prods
    return [sc_b * _conv3_rows(sc_c * sc_v, cols[0], seg)]


def _epi_sigmoid(prods, cols, tiles):
    return [_sigmoid(prods[0] + cols[0])]


def _epi_gate_merge(prods, cols, tiles):
    return [tiles[0] * prods[0] + tiles[1] * prods[1]]


def _split3(x):
    hi = x.astype(BF16)
    r1 = x - hi.astype(F32)
    mid = r1.astype(BF16)
    lo = (r1 - mid.astype(F32)).astype(BF16)
    return hi, mid, lo


def _cumsum_rows(tri, x):
    return sum(jnp.dot(tri, p, preferred_element_type=F32) for p in _split3(x))


def _cumsum_cols(x, tri):
    return sum(jnp.dot(p, tri, preferred_element_type=F32) for p in _split3(x))


def _pair(lo_mask, v, h):
    return jnp.where(lo_mask, v[:, h:h + 1], v[:, h + 1:h + 2])


def _spread(x, sel):
    return jnp.dot(jnp.concatenate(_split3(x), axis=1), sel, preferred_element_type=F32)


def _ssd_kernel(post, xs_ref, bc_ref, dtc_ref, dtr_ref, arow_ref, acol_ref, tri_ref, trit_ref, sel128_ref,
                sel64_ref, h0_ref, *rest):
    if post:
        yo_ref, z_ref, dcol_ref, nw_ref, y_ref, hfin_ref, st_ref = rest
    else:
        y_ref, hfin_ref, st_ref = rest
    q, n = CHUNK, SSD_STATE
    step = pl.program_id(2)

    @pl.when(step == 0)
    def _():
        st_ref[...] = h0_ref[...]

    tri = tri_ref[...]
    dtc = dtc_ref[...]
    dtr = dtr_ref[...]
    al_c = dtc * arow_ref[...]
    al_r = dtr * acol_ref[...]
    cs_c = _cumsum_rows(tri, al_c)
    cs_r = _cumsum_cols(al_r, trit_ref[...])
    tot_c = jnp.sum(al_c, axis=0, keepdims=True)
    cs_b = _spread(cs_c, sel128_ref[...])
    dt_b = _spread(dtc, sel64_ref[...])
    mask = tri.astype(F32) > 0.5
    lo_q = lax.broadcasted_iota(jnp.int32, (q, LANES), 1) < SSD_HEAD_DIM
    lo_1 = lax.broadcasted_iota(jnp.int32, (1, LANES), 1) < SSD_HEAD_DIM
    for gi in range(SSD_GROUPS):
        bm = bc_ref[:, gi * n:(gi + 1) * n]
        cm = bc_ref[:, (SSD_GROUPS + gi) * n:(SSD_GROUPS + gi + 1) * n]
        cb = lax.dot_general(cm, bm, (((1,), (1,)), ((), ())), preferred_element_type=F32)
        ht = st_ref[gi]
        y_off = jnp.dot(cm, ht.astype(BF16), preferred_element_type=F32)
        xw = []
        dec = []
        ys = []
        for k in range(SSD_HPG // 2):
            h0 = gi * SSD_HPG + 2 * k
            col = gi * GROUP_W + k * LANES
            ms = []
            for h in (h0, h0 + 1):
                seg = cs_b[:, h * LANES:(h + 1) * LANES] - cs_r[h:h + 1, :]
                lmat = jnp.exp(jnp.where(mask, seg, -1e30))
                ms.append((cb * lmat * dtr[h:h + 1, :]).astype(BF16))
            lhs = jnp.concatenate(ms, axis=1)
            xp = xs_ref[:, col:col + LANES]
            rhs = jnp.concatenate([jnp.where(lo_q, xp, 0.0).astype(BF16),
                                   jnp.where(lo_q, 0.0, xp).astype(BF16)], axis=0)
            y_diag = jnp.dot(lhs, rhs, preferred_element_type=F32)
            cs_p = jnp.where(lo_q, cs_b[:, h0 * LANES:(h0 + 1) * LANES], cs_b[:, (h0 + 1) * LANES:(h0 + 2) * LANES])
            tot_p = _pair(lo_1, tot_c, h0)
            dt_p = dt_b[:, h0 * SSD_HEAD_DIM:(h0 + 2) * SSD_HEAD_DIM]
            y_k = y_diag + y_off[:, k * LANES:(k + 1) * LANES] * jnp.exp(cs_p)
            if post:
                cs = slice(col, col + LANES)
                ys.append((y_k + yo_ref[:, cs] + dcol_ref[:, cs] * xp) * _silu(z_ref[:, cs]))
            else:
                y_ref[:, col:col + LANES] = y_k
            xw.append((xp * (jnp.exp(tot_p - cs_p) * dt_p)).astype(BF16))
            dec.append(jnp.exp(tot_p))
        xw = jnp.concatenate(xw, axis=1)
        s_t = lax.dot_general(bm, xw, (((0,), (0,)), ((), ())), preferred_element_type=F32)
        st_ref[gi] = ht * jnp.concatenate(dec, axis=1) + s_t
        if post:
            ssq = sum(jnp.sum(v * v, axis=-1, keepdims=True) for v in ys)
            inv = lax.rsqrt(ssq * (1.0 / GROUP_W) + EPS)
            for k, v in enumerate(ys):
                cs = slice(gi * GROUP_W + k * LANES, gi * GROUP_W + (k + 1) * LANES)
                y_ref[:, cs] = (v * inv * nw_ref[:, cs]).astype(y_ref.dtype)

    @pl.when(step == pl.num_programs(2) - 1)
    def _():
        hfin_ref[...] = st_ref[...]


def _ssd_scan(xs, bc, dt, a_neg, tri, h0, post=None):
    bsz, length, _ = xs.shape
    nc = length // CHUNK
    g, q, n, nh = SSD_GROUPS, CHUNK, SSD_STATE, SSD_HEADS
    dt4 = dt[:, :, :2 * nh].reshape(bsz, length, 2, nh)
    dt_col = jnp.transpose(dt4, (2, 0, 1, 3))
    dt_row = jnp.transpose(dt4, (2, 0, 3, 1))
    a_row = a_neg.reshape(2, 1, nh)
    a_col = a_neg.reshape(2, nh, 1)

    def head_select(width):
        rows = lax.broadcasted_iota(jnp.int32, (3 * nh, nh * width), 0) % nh
        cols = lax.broadcasted_iota(jnp.int32, (3 * nh, nh * width), 1) // width
        return (rows == cols).astype(BF16)

    sel128, sel64 = head_select(LANES), head_select(SSD_HEAD_DIM)

    def call(d0, nd, extra, y_dtype):
        def dd(d):
            return d + d0

        def chunk(d, c):
            return c + dd(d) * (nc - 1 - 2 * c)

        extra_specs = []
        if extra:
            extra_specs = [pl.BlockSpec((None, q, SSD_INNER), lambda d, b, c: (b, chunk(d, c), 0)),
                           pl.BlockSpec((None, q, SSD_INNER), lambda d, b, c: (b, chunk(d, c), 0)),
                           pl.BlockSpec((1, SSD_INNER), lambda d, b, c: (0, 0)),
                           pl.BlockSpec((1, SSD_INNER), lambda d, b, c: (0, 0))]
        return pl.pallas_call(
            functools.partial(_ssd_kernel, bool(extra)),
            grid=(nd, bsz, nc),
            in_specs=[
                pl.BlockSpec((None, q, SSD_INNER), lambda d, b, c: (b, chunk(d, c), 0)),
                pl.BlockSpec((None, q, 2 * g * n), lambda d, b, c: (b, chunk(d, c), 0)),
                pl.BlockSpec((None, None, q, nh), lambda d, b, c: (dd(d), b, chunk(d, c), 0)),
                pl.BlockSpec((None, None, nh, q), lambda d, b, c: (dd(d), b, 0, chunk(d, c))),
                pl.BlockSpec((None, 1, nh), lambda d, b, c: (dd(d), 0, 0)),
                pl.BlockSpec((None, nh, 1), lambda d, b, c: (dd(d), 0, 0)),
                pl.BlockSpec((None, q, q), lambda d, b, c: (dd(d), 0, 0)),
                pl.BlockSpec((None, q, q), lambda d, b, c: (1 - dd(d), 0, 0)),
                pl.BlockSpec(sel128.shape, lambda d, b, c: (0, 0)),
                pl.BlockSpec(sel64.shape, lambda d, b, c: (0, 0)),
                pl.BlockSpec((None, None, g, n, GROUP_W), lambda d, b, c: (dd(d), b, 0, 0, 0)),
            ] + extra_specs,
            out_specs=[
                pl.BlockSpec((None, None, q, SSD_INNER), lambda d, b, c: (d, b, chunk(d, c), 0)),
                pl.BlockSpec((None, None, g, n, GROUP_W), lambda d, b, c: (d, b, 0, 0, 0)),
            ],
            out_shape=[jax.ShapeDtypeStruct((nd, bsz, length, SSD_INNER), y_dtype),
                       jax.ShapeDtypeStruct((nd, bsz, g, n, GROUP_W), F32)],
            scratch_shapes=[pltpu.VMEM((g, n, GROUP_W), F32)],
            compiler_params=_params(3),
            name="ssd_scan",
        )(xs, bc, dt_col, dt_row, a_row, a_col, tri, tri, sel128, sel64, h0, *extra)

    if post is None:
        return call(0, 2, (), F32)
    y_fwd, _ = call(0, 1, (), F32)
    y_out, _ = call(1, 1, (y_fwd[0],) + tuple(post), BF16)
    return y_out[0]


def _mix_out_kernel(m_ref, wo_ref, x_ref, gx_ref, scf_ref, shf_ref, npost_ref, npre_ref, wr_ref,
                    x1_ref, h2_ref, aff_ref):
    half = m_ref.shape[0] // 2
    for p in range(2):
        rs = slice(p * half, (p + 1) * half)
        out = jnp.dot(m_ref[rs, :], wo_ref[...], preferred_element_type=F32)
        x1 = x_ref[rs, :] + gx_ref[...] * (_rms(out) * npost_ref[...])
        x1_ref[rs, :] = x1
        h2 = (_rms(x1) * npre_ref[...]) * (1.0 + scf_ref[...]) + shf_ref[...]
        h2_ref[rs, :] = h2
        logits = jnp.dot(h2.astype(BF16), wr_ref[...], preferred_element_type=F32)
        lane = lax.broadcasted_iota(jnp.int32, logits.shape, 1)
        logits = jnp.where(lane < N_EXPERTS, logits, -1e30)
        e = jnp.exp(logits - jnp.max(logits, axis=-1, keepdims=True))
        aff_ref[rs, :] = e / jnp.sum(e, axis=-1, keepdims=True)


def _mix_out(merged, w_o, x2d, mod3, npost, npre, w_router_pad, length, tm):
    m, d = merged.shape
    nt = length // tm
    row = lambda kind: (lambda i: ((i // nt) * 6 + kind, 0, 0))
    return pl.pallas_call(
        _mix_out_kernel,
        grid=(m // tm,),
        in_specs=[pl.BlockSpec((tm, d), lambda i: (i, 0)),
                  pl.BlockSpec((d, d), lambda i: (0, 0), pipeline_mode=pl.Buffered(1)),
                  pl.BlockSpec((tm, d), lambda i: (i, 0)),
                  pl.BlockSpec((None, 1, d), row(2)),
                  pl.BlockSpec((None, 1, d), row(4)),
                  pl.BlockSpec((None, 1, d), row(3)),
                  pl.BlockSpec((1, d), lambda i: (0, 0)),
                  pl.BlockSpec((1, d), lambda i: (0, 0)),
                  pl.BlockSpec((d, LANES), lambda i: (0, 0))],
        out_specs=[pl.BlockSpec((tm, d), lambda i: (i, 0)),
                   pl.BlockSpec((tm, d), lambda i: (i, 0)),
                   pl.BlockSpec((tm, LANES), lambda i: (i, 0))],
        out_shape=[jax.ShapeDtypeStruct((m, d), F32),
                   jax.ShapeDtypeStruct((m, d), F32),
                   jax.ShapeDtypeStruct((m, LANES), F32)],
        compiler_params=_params(1),
        name="mix_out_router",
    )(merged, w_o, x2d, mod3, mod3, mod3, npost, npre, w_router_pad)


def _route_select(cap, aff_ref, triu_ref, low_ref, pos_ref, slot_ref, ts_ref):
    aff = aff_ref[...]
    e_cnt, n_tok = aff.shape

    def count(m):
        return jnp.sum(jnp.where(m, 1.0, 0.0), axis=1, keepdims=True)

    def search_bits(_, carry):
        lo, hi = carry
        mid = lo + ((hi - lo + 1) >> 1)
        ok = count(aff >= pltpu.bitcast(mid, F32)) >= cap
        return jnp.where(ok, mid, lo), jnp.where(ok, hi, mid - 1)

    lo0 = jnp.zeros((e_cnt, 1), jnp.int32)
    hi0 = jnp.full((e_cnt, 1), 0x7F800000, jnp.int32)
    lo_b, _ = lax.fori_loop(0, 32, search_bits, (lo0, hi0))

    def search_mid(_, carry):
        lo, hi = carry
        mid = 0.5 * (lo + hi)
        ok = count(aff >= mid) >= cap
        return jnp.where(ok, mid, lo), jnp.where(ok, hi, mid)

    lo_f, _ = lax.fori_loop(0, 24, search_mid, (pltpu.bitcast(lo_b, F32), pltpu.bitcast(lo_b + 1, F32)))
    thr = jnp.min(jnp.where(aff >= lo_f, aff, jnp.inf), axis=1, keepdims=True)
    need_f = cap - count(aff > thr)
    triu = triu_ref[...]
    carry_eq = jnp.zeros((e_cnt, 1), F32)
    carry_sel = jnp.zeros((e_cnt, 1), F32)
    for k in range(n_tok // LANES):
        sl = slice(k * LANES, (k + 1) * LANES)
        aff_k = aff[:, sl]
        eq_f = jnp.where(aff_k == thr, 1.0, 0.0)
        eq_incl = jnp.dot(eq_f.astype(BF16), triu, preferred_element_type=F32) + carry_eq
        sel_k = (aff_k > thr) | ((aff_k == thr) & (eq_incl <= need_f))
        sel_f = jnp.where(sel_k, 1.0, 0.0)
        sel_b = sel_f.astype(BF16)
        sel_incl = jnp.dot(sel_b, triu, preferred_element_type=F32) + carry_sel
        excl = sel_incl - sel_f
        pos_ref[:, sl] = jnp.where(sel_k, excl.astype(jnp.int32), -1)
        tok_start = jnp.sum(excl, axis=0, keepdims=True)
        ts_ref[:, sl] = tok_start
        slot_ref[:, sl] = tok_start + jnp.dot(low_ref[...], sel_b, preferred_element_type=F32)
        carry_eq = eq_incl[:, LANES - 1:LANES]
        carry_sel = sel_incl[:, LANES - 1:LANES]


def _route_kernel(cap, aff_ref, triu_ref, low_ref, list_ref, ts_ref, pos_ref, slot_ref):
    e = pl.program_id(1)

    @pl.when(e == 0)
    def _():
        _route_select(cap, aff_ref, triu_ref, low_ref, pos_ref, slot_ref, ts_ref)

    n_tok = aff_ref.shape[1]
    pos = pos_ref[pl.ds(e, 1), :]
    slot = slot_ref[pl.ds(e, 1), :]
    aff = aff_ref[pl.ds(e, 1), :]
    hit = lax.broadcasted_iota(jnp.int32, (cap, n_tok), 0) == pos
    tok = lax.broadcasted_iota(jnp.int32, (1, n_tok), 1)
    slot_hi = jnp.floor(slot * (1.0 / LANES))
    a_hi = aff.astype(BF16).astype(F32)
    a_mid = (aff - a_hi).astype(BF16).astype(F32)
    rows = [(tok >> 6).astype(F32), (tok & 63).astype(F32), slot_hi, slot - slot_hi * LANES,
            a_hi, a_mid, aff - a_hi - a_mid]
    sub = lax.broadcasted_iota(jnp.int32, (8, n_tok), 0)
    vals = jnp.zeros((8, n_tok), F32)
    for i, r in enumerate(rows):
        vals = jnp.where(sub == i, r, vals)
    list_ref[...] = lax.dot_general(jnp.where(hit, 1.0, 0.0).astype(BF16), vals.astype(BF16),
                                    (((1,), (1,)), ((), ())), preferred_element_type=F32)


def _route(aff_t, triu, low, cap):
    bsz, e_cnt, n_tok = aff_t.shape
    return pl.pallas_call(
        functools.partial(_route_kernel, cap),
        grid=(bsz, e_cnt),
        in_specs=[pl.BlockSpec((None, e_cnt, n_tok), lambda b, e: (b, 0, 0)),
                  pl.BlockSpec((LANES, LANES), lambda b, e: (0, 0)),
                  pl.BlockSpec((e_cnt, e_cnt), lambda b, e: (0, 0))],
        out_specs=[pl.BlockSpec((None, None, cap, 8), lambda b, e: (b, e, 0, 0)),
                   pl.BlockSpec((None, 1, n_tok), lambda b, e: (b, 0, 0))],
        out_shape=[jax.ShapeDtypeStruct((bsz, e_cnt, cap, 8), F32),
                   jax.ShapeDtypeStruct((bsz, 1, n_tok), F32)],
        scratch_shapes=[pltpu.VMEM((e_cnt, n_tok), jnp.int32), pltpu.VMEM((e_cnt, n_tok), F32)],
        compiler_params=_params(2),
        name="route_topk",
    )(aff_t, triu, low)


def _expert_kernel(cap, n_e, n_f, n_pairs, tf, idx_ref, dst_ref, h_hbm, w1_hbm, w3_hbm, w2_hbm, val_ref, z_hbm,
                   gbuf, obuf, xg_ref, acc_ref, w1_buf, w3_buf, w2_buf, gsem, ssem, wsem):
    e, f, j = pl.program_id(1), pl.program_id(2), pl.program_id(3)
    pair = pl.program_id(0) * n_e + e
    nxt = jnp.minimum(pair + 1, n_pairs - 1)
    d = h_hbm.shape[1]
    rows = cap // n_f
    s = 2 * f + j
    part = lax.rem(s, n_f)
    wslot = lax.rem(pair * n_f + f, 2)

    def weight_copies(ee, ff, slot):
        col = pl.multiple_of(ff * tf, tf)
        return (pltpu.make_async_copy(w1_hbm.at[ee, :, pl.ds(col, tf)], w1_buf.at[slot], wsem.at[slot, 0]),
                pltpu.make_async_copy(w3_hbm.at[ee, :, pl.ds(col, tf)], w3_buf.at[slot], wsem.at[slot, 1]),
                pltpu.make_async_copy(w2_hbm.at[ee, pl.ds(col, tf), :], w2_buf.at[slot], wsem.at[slot, 2]))

    def gather(p, t, q, r):
        src = idx_ref[(2 * p + t) * cap + q * rows + r]
        return pltpu.make_async_copy(h_hbm.at[pl.ds(src, 1), :], gbuf.at[t, q, pl.ds(r, 1), :], gsem.at[0])

    def scatter(pos, t, q, r):
        dst = dst_ref[(2 * pos + t) * cap + q * rows + r]
        return pltpu.make_async_copy(obuf.at[t, q, pl.ds(r, 1), :], z_hbm.at[pl.ds(dst, 1), :], ssem.at[t])

    def for_all_rows(fn):
        def body(r, carry):
            for t in range(2):
                for q in range(n_f):
                    fn(t, q, r)
            return carry

        lax.fori_loop(0, rows, body, 0)

    @pl.when(jnp.logical_and(pair == 0, s == 0))
    def _():
        for c in weight_copies(e, f, wslot):
            c.start()
        obuf[...] = jnp.zeros_like(obuf)
        for_all_rows(lambda t, q, r: gather(0, t, q, r).start())

    @pl.when(j == 0)
    def _():
        last_f = f == n_f - 1
        e_nxt = jnp.where(last_f, jnp.where(e == n_e - 1, 0, e + 1), e)
        f_nxt = jnp.where(last_f, 0, f + 1)

        @pl.when(jnp.logical_not(jnp.logical_and(pair == n_pairs - 1, last_f)))
        def _():
            for c in weight_copies(e_nxt, f_nxt, 1 - wslot):
                c.start()

        for c in weight_copies(e, f, wslot):
            c.wait()

    @pl.when(s == 0)
    def _():
        pltpu.make_async_copy(gbuf, gbuf, gsem.at[0]).wait()
        for t in range(2):
            for q in range(n_f):
                xg_ref[t, q * rows:(q + 1) * rows, :] = gbuf[t, q].astype(BF16)

    @pl.when(f == 0)
    def _():
        acc_ref[j] = jnp.zeros((cap, d), F32)

    @pl.when(s < n_f)
    def _():
        for t in range(2):
            for r in range(rows):
                scatter(pair, t, part, r).start()

    @pl.when(s >= n_f)
    def _():
        for t in range(2):
            for r in range(rows):
                gather(nxt, t, part, r).start()

    xg = xg_ref[j]
    hs = []
    for c in range(tf // MXU_COLS):
        sl = slice(c * MXU_COLS, (c + 1) * MXU_COLS)
        g = jnp.dot(xg, w1_buf[wslot, :, sl].astype(BF16), preferred_element_type=F32)
        u = jnp.dot(xg, w3_buf[wslot, :, sl].astype(BF16), preferred_element_type=F32)
        hs.append((_silu(g) * u).astype(BF16))
    h = jnp.concatenate(hs, axis=1)
    for c in range(d // MXU_COLS):
        sl = slice(c * MXU_COLS, (c + 1) * MXU_COLS)
        acc_ref[j, :, sl] += jnp.dot(h, w2_buf[wslot, :, sl].astype(BF16), preferred_element_type=F32)

    @pl.when(f == n_f - 1)
    def _():
        pltpu.make_async_copy(obuf.at[j], obuf.at[j], ssem.at[j]).wait()
        for q in range(n_f):
            sl = slice(q * rows, (q + 1) * rows)
            obuf[j, q] = acc_ref[j, sl, :] * val_ref[sl, :]

        @pl.when(jnp.logical_and(pair == n_pairs - 1, j == 1))
        def _():
            for_all_rows(lambda t, q, r: scatter(n_pairs, t, q, r).start())
            for t in range(2):
                pltpu.make_async_copy(obuf.at[t], obuf.at[t], ssem.at[t]).wait()
            pltpu.make_async_copy(gbuf, gbuf, gsem.at[0]).wait()


def _experts(idx, dst_ext, h2, w1, w3, w2, vals, n_rows_out, tf):
    bsz, e_cnt, cap, _ = vals.shape
    d = h2.shape[-1]
    n_f = w1.shape[-1] // tf
    n_pairs = bsz // 2 * e_cnt
    grid_spec = pltpu.PrefetchScalarGridSpec(
        num_scalar_prefetch=2,
        grid=(bsz // 2, e_cnt, n_f, 2),
        in_specs=[pl.BlockSpec(memory_space=pl.ANY),
                  pl.BlockSpec(memory_space=pl.ANY),
                  pl.BlockSpec(memory_space=pl.ANY),
                  pl.BlockSpec(memory_space=pl.ANY),
                  pl.BlockSpec((None, None, cap, 1), lambda bh, e, f, j, *_: (2 * bh + j, e, 0, 0))],
        out_specs=pl.BlockSpec(memory_space=pl.ANY),
        scratch_shapes=[pltpu.VMEM((2, n_f, cap // n_f, d), F32), pltpu.VMEM((2, n_f, cap // n_f, d), F32),
                        pltpu.VMEM((2, cap, d), BF16), pltpu.VMEM((2, cap, d), F32),
                        pltpu.VMEM((2, d, tf), F32), pltpu.VMEM((2, d, tf), F32), pltpu.VMEM((2, tf, d), F32),
                        pltpu.SemaphoreType.DMA((1,)), pltpu.SemaphoreType.DMA((2,)),
                        pltpu.SemaphoreType.DMA((2, 3))],
    )
    return pl.pallas_call(
        functools.partial(_expert_kernel, cap, e_cnt, n_f, n_pairs, tf),
        grid_spec=grid_spec,
        out_shape=jax.ShapeDtypeStruct((n_rows_out, d), F32),
        compiler_params=_params(4),
        name="moe_experts",
    )(idx.reshape(-1), dst_ext.reshape(-1), h2, w1, w3, w2, vals)


COMBINE_ROWS = 256


COMBINE_SLOTS = 4


def _combine_kernel(n_slot, nb, nt, starts_ref, ts_ref, x1_ref, g_ref, w_ref, z_hbm, o_ref, zbuf, acc_ref, sem,
                    st_ref):
    b, t = pl.program_id(0), pl.program_id(1)
    shift = COMBINE_ROWS.bit_length() - 1

    def tile_chunks(bb, tt):
        s0 = starts_ref[bb * (nt + 1) + tt]
        s1 = starts_ref[bb * (nt + 1) + tt + 1]
        start = lax.shift_left(lax.shift_right_logical(s0, 3), 3)
        return start, jnp.maximum(lax.shift_right_logical(s1 - start + COMBINE_ROWS - 1, shift), 1)

    def copy(row, slot):
        return pltpu.make_async_copy(z_hbm.at[pl.ds(row, COMBINE_ROWS), :], zbuf.at[slot], sem.at[slot])

    def fetch_one():
        tile = st_ref[2]

        @pl.when(tile < nb * nt)
        def _():
            bb = lax.div(tile, nt)
            start, n = tile_chunks(bb, tile - bb * nt)
            c, k = st_ref[3], st_ref[1]
            copy(pl.multiple_of(bb * n_slot + start + c * COMBINE_ROWS, 8), lax.rem(k, COMBINE_SLOTS)).start()
            st_ref[1] = k + 1
            tile_done = c + 1 == n
            st_ref[2] = jnp.where(tile_done, tile + 1, tile)
            st_ref[3] = jnp.where(tile_done, 0, c + 1)

    @pl.when(jnp.logical_and(b == 0, t == 0))
    def _():
        for i in range(4):
            st_ref[i] = 0
        for _ in range(COMBINE_SLOTS - 1):
            fetch_one()

    start0, n_chunks = tile_chunks(b, t)
    acc_ref[...] = jnp.zeros_like(acc_ref)
    first = ts_ref[:, 0:1]
    last = ts_ref[:, 1:2]

    def body(c, carry):
        k = st_ref[0]
        slot = lax.rem(k, COMBINE_SLOTS)
        copy(0, slot).wait()
        st_ref[0] = k + 1
        fetch_one()
        rows = zbuf[slot]
        j = (start0 + c * COMBINE_ROWS + lax.broadcasted_iota(jnp.int32, (1, COMBINE_ROWS), 1)).astype(F32)
        seg = jnp.where(j >= first, jnp.where(j < last, 1.0, 0.0), 0.0).astype(BF16)
        hi = rows.astype(BF16)
        lo = (rows - hi.astype(F32)).astype(BF16)
        acc_ref[...] += (jnp.dot(seg, hi, preferred_element_type=F32)
                         + jnp.dot(seg, lo, preferred_element_type=F32))
        return carry

    lax.fori_loop(0, n_chunks, body, 0)
    o_ref[...] = x1_ref[...] + g_ref[...] * (_rms(acc_ref[...]) * w_ref[...])


def _combine(starts, ts2, z, x1, mod3, npost_ffn, n_slot, tm):
    bsz, n_tok, d = x1.shape
    grid_spec = pltpu.PrefetchScalarGridSpec(
        num_scalar_prefetch=1,
        grid=(bsz, n_tok // tm),
        in_specs=[pl.BlockSpec((None, tm, 2), lambda b, t, *_: (b, t, 0)),
                  pl.BlockSpec((None, tm, d), lambda b, t, *_: (b, t, 0)),
                  pl.BlockSpec((None, 1, d), lambda b, t, *_: (b * 6 + 5, 0, 0)),
                  pl.BlockSpec((1, d), lambda b, t, *_: (0, 0)),
                  pl.BlockSpec(memory_space=pl.ANY)],
        out_specs=pl.BlockSpec((None, tm, d), lambda b, t, *_: (b, t, 0)),
        scratch_shapes=[pltpu.VMEM((COMBINE_SLOTS, COMBINE_ROWS, d), F32), pltpu.VMEM((tm, d), F32),
                        pltpu.SemaphoreType.DMA((COMBINE_SLOTS,)), pltpu.SMEM((4,), jnp.int32)],
    )
    return pl.pallas_call(
        functools.partial(_combine_kernel, n_slot, bsz, n_tok // tm),
        grid_spec=grid_spec,
        out_shape=jax.ShapeDtypeStruct((bsz, n_tok, d), F32),
        compiler_params=_params(2),
        name="moe_combine",
    )(starts, ts2, x1, mod3, npost_ffn, z)


def _ssd_inputs(h, w_xs, w_bc, w_dt, conv_w, conv_b, dt_bias_pad, seg, tm):
    cw_x, cw_bc = conv_w[:, :SSD_INNER], conv_w[:, SSD_INNER:]
    cb_x, cb_bc = conv_b[:, :SSD_INNER], conv_b[:, SSD_INNER:]
    epi = functools.partial(_epi_conv_silu, seg)
    (xs,) = _fused_matmul([h], [(w_xs, 0)], [0], [cw_x, cb_x], [], epi, [F32], SSD_INNER, tm, 1024, "proj_xs")
    (bc,) = _fused_matmul([h], [(w_bc, 0)], [0], [cw_bc, cb_bc], [], epi, [BF16],
                          2 * SSD_GROUPS * SSD_STATE, tm, 1024, "proj_bc")
    (dt,) = _fused_matmul([h], [(w_dt, 0)], [0], [dt_bias_pad], [], _epi_softplus, [F32], LANES, tm, LANES,
                          "proj_dt")
    return xs, bc, dt


def kernel(x, c, ctx, c_ctx, w_mod, b_mod, norm_pre_mix, norm_post_mix, w_in, ssd_conv_w, ssd_conv_b,
           dt_bias, a_log, ssd_d, ssd_norm_w, w_ssd_out, sc_conv_w, w_sc_out, b_gate, w_o, norm_pre_ffn,
           norm_post_ffn, w_router, w_e1, w_e3, w_e2):
    assert w_mod.shape[0] == 1, "single-layer block"
    bsz, length, d = x.shape
    ctx_len = ctx.shape[1]
    m = bsz * length
    cap = CAPACITY_FACTOR * length // N_EXPERTS
    row = lambda v: v.reshape(1, -1)

    o_z, o_xbc, o_dt, o_scb, o_scc, o_scv = (SSD_INNER, SSD_INNER + SSD_XBC,
                                             SSD_INNER + SSD_XBC + 2 * SSD_HEADS,
                                             SSD_INNER + SSD_XBC + 2 * SSD_HEADS + SC_DIM,
                                             SSD_INNER + SSD_XBC + 2 * SSD_HEADS + 2 * SC_DIM,
                                             SSD_INNER + SSD_XBC + 2 * SSD_HEADS + 3 * SC_DIM)
    w_in0 = w_in[0]
    w_z = w_in0[:, :o_z].astype(BF16)
    w_xs = w_in0[:, o_z:o_z + SSD_INNER].astype(BF16)
    w_bc = w_in0[:, o_z + SSD_INNER:o_xbc].astype(BF16)
    w_dt = jnp.pad(w_in0[:, o_xbc:o_dt], ((0, 0), (0, LANES - 2 * SSD_HEADS))).astype(BF16)
    w_scb = w_in0[:, o_dt:o_scb].astype(BF16)
    w_scc = w_in0[:, o_scb:o_scc].astype(BF16)
    w_scv = w_in0[:, o_scc:o_scv].astype(BF16)
    w_gate = w_in0[:, o_scv:].astype(BF16)
    dt_bias_pad = jnp.pad(dt_bias[0].reshape(1, -1), ((0, 0), (0, LANES - 2 * SSD_HEADS)))
    a_neg = -jnp.exp(a_log[0])
    ii = lax.broadcasted_iota(jnp.int32, (CHUNK, CHUNK), 0)
    jj = lax.broadcasted_iota(jnp.int32, (CHUNK, CHUNK), 1)
    tri = jnp.stack([ii >= jj, ii <= jj]).astype(BF16)
    conv_w, conv_b = ssd_conv_w[0], row(ssd_conv_b[0])

    cs = jnp.concatenate([c, c_ctx[None, :], jnp.zeros((8 - bsz - 1, d), F32)], axis=0)
    mod3 = _modulation(cs, w_mod[0], b_mod[0]).reshape(8 * 6, 1, d)

    hc = _norm_mod(ctx, norm_pre_mix[0], mod3, lambda b: bsz * 6, lambda b: bsz * 6 + 1, ctx_len)
    xs_c, bc_c, dt_c = _ssd_inputs(hc.reshape(bsz * ctx_len, d), w_xs, w_bc, w_dt, conv_w, conv_b,
                                   dt_bias_pad, ctx_len, ctx_len)
    h_zero = jnp.zeros((2, bsz, SSD_GROUPS, SSD_STATE, GROUP_W), F32)
    _, states = _ssd_scan(xs_c.reshape(bsz, ctx_len, -1), bc_c.reshape(bsz, ctx_len, -1),
                          dt_c.reshape(bsz, ctx_len, -1), a_neg, tri, h_zero)

    tm = 1024
    hx = _norm_mod(x, norm_pre_mix[0], mod3, lambda b: b * 6, lambda b: b * 6 + 1, tm).reshape(m, d)
    (z,) = _fused_matmul([hx], [(w_z, 0)], [0], [], [], _epi_identity, [F32], SSD_INNER, tm, 1024, "proj_z")
    xs, bc, dt = _ssd_inputs(hx, w_xs, w_bc, w_dt, conv_w, conv_b, dt_bias_pad, GRID_W, tm)
    (sc_y,) = _fused_matmul([hx], [(w_scb, 0), (w_scc, 0), (w_scv, 0)], [0, 0, 0], [sc_conv_w[0]], [],
                            functools.partial(_epi_shortconv, GRID_W), [BF16], SC_DIM, tm, 512, "proj_sc")
    (gates,) = _fused_matmul([hx], [(w_gate, 0)], [0], [row(b_gate[0])], [], _epi_sigmoid, [F32],
                             2 * d, tm, 1024, "proj_gate")
    d_cols = jnp.repeat(ssd_d[0], SSD_HEAD_DIM).reshape(1, -1)
    y_norm = _ssd_scan(xs.reshape(bsz, length, -1), bc.reshape(bsz, length, -1), dt.reshape(bsz, length, -1),
                       a_neg, tri, states, post=(z.reshape(bsz, length, -1), d_cols, row(ssd_norm_w[0])))
    y_norm = y_norm.reshape(m, -1)
    tn = 512
    (merged,) = _fused_matmul([y_norm, sc_y], [(w_ssd_out[0].astype(BF16), 0), (w_sc_out[0].astype(BF16), 0)],
                              [0, 1], [], [(gates, 0), (gates, d // tn)], _epi_gate_merge, [BF16], d, tm, tn,
                              "mix_merge")
    w_router_pad = jnp.pad(w_router[0], ((0, 0), (0, LANES - N_EXPERTS))).astype(BF16)
    x1, h2, aff = _mix_out(merged, w_o[0].astype(BF16), x.reshape(m, d), mod3, row(norm_post_mix[0]),
                           row(norm_pre_ffn[0]), w_router_pad, length, 512)

    aff_t = jnp.transpose(aff[:, :N_EXPERTS].reshape(bsz, length, N_EXPERTS), (0, 2, 1))
    e_i = lax.broadcasted_iota(jnp.int32, (N_EXPERTS, N_EXPERTS), 0)
    e_j = lax.broadcasted_iota(jnp.int32, (N_EXPERTS, N_EXPERTS), 1)
    lists, ts = _route(aff_t, tri[1], (e_j < e_i).astype(BF16), cap)
    n_slot = N_EXPERTS * cap
    li = lists[..., :4].astype(jnp.int32)
    b_i = jnp.arange(bsz, dtype=jnp.int32)[:, None, None]
    def pair_order(a):
        return jnp.transpose(a.reshape(bsz // 2, 2, N_EXPERTS, cap), (0, 2, 1, 3)).reshape(-1, cap)

    idx = pair_order(li[..., 0] * 64 + li[..., 1] + b_i * length)
    dst = pair_order(li[..., 2] * LANES + li[..., 3] + b_i * n_slot)
    n_pad = 2 * cap
    dst_ext = jnp.concatenate([bsz * n_slot + jnp.arange(n_pad, dtype=jnp.int32).reshape(2, cap), dst], axis=0)
    vals = (lists[..., 4] + lists[..., 5] + lists[..., 6])[..., None]
    z = _experts(idx, dst_ext, h2, w_e1[0], w_e3[0], w_e2[0], vals, bsz * n_slot + n_pad, MXU_COLS)
    tm_c = 256
    ts = ts.reshape(bsz, length)
    end = jnp.full((bsz, 1), n_slot, F32)
    ts2 = jnp.stack([ts, jnp.concatenate([ts[:, 1:], end], axis=1)], axis=-1)
    starts = jnp.concatenate([ts[:, ::tm_c], end], axis=1).astype(jnp.int32).reshape(-1)
    return _combine(starts, ts2, z, x1.reshape(bsz, length, d), mod3, row(norm_post_ffn[0]), n_slot, tm_c)
```

```python
import functools
import math

import jax
import jax.numpy as jnp
from jax import lax
from jax.experimental import pallas as pl
from jax.experimental.pallas import tpu as pltpu

F32 = jnp.float32
BF16 = jnp.bfloat16

D_MODEL = 2048
GRID_W = 64
SSD_HEADS = 32
SSD_HEAD_DIM = 64
SSD_INNER = SSD_HEADS * SSD_HEAD_DIM
SSD_GROUPS = 4
SSD_HPG = SSD_HEADS // SSD_GROUPS
SSD_STATE = 128
CHUNK = 128
SSD_XBC = SSD_INNER + 2 * SSD_GROUPS * SSD_STATE
SC_DIM = 2048
N_EXPERTS = 16
EXPERT_FF = 2048
CAPACITY_FACTOR = 2
EPS = 1e-6
GROUP_W = SSD_HPG * SSD_HEAD_DIM

LANES = 128
MXU_COLS = 256
V7X_VMEM_BYTES = 64 * 1024 * 1024
VMEM_LIMIT = V7X_VMEM_BYTES - 8 * 1024 * 1024


def _params(n_grid):
    return pltpu.CompilerParams(dimension_semantics=("arbitrary",) * n_grid,
                                vmem_limit_bytes=VMEM_LIMIT)


def _silu(x):
    return x * (1.0 / (1.0 + jnp.exp(-x)))


def _sigmoid(x):
    return 1.0 / (1.0 + jnp.exp(-x))


def _softplus(x):
    return jnp.maximum(x, 0.0) + jnp.log(1.0 + jnp.exp(-jnp.abs(x)))


def _rms(x):
    return x * lax.rsqrt(jnp.mean(x * x, axis=-1, keepdims=True) + EPS)


def _mod_kernel(c_ref, w_ref, b_ref, o_ref):
    a = _silu(c_ref[...]).astype(BF16)
    o_ref[...] = jnp.dot(a, w_ref[...].astype(BF16), preferred_element_type=F32) + b_ref[...]


def _modulation(cs, w_mod, b_mod):
    rows, d = cs.shape
    n = w_mod.shape[1]
    tn = 1024
    return pl.pallas_call(
        _mod_kernel,
        grid=(n // tn,),
        in_specs=[pl.BlockSpec((rows, d), lambda j: (0, 0)),
                  pl.BlockSpec((d, tn), lambda j: (0, j)),
                  pl.BlockSpec((1, tn), lambda j: (0, j))],
        out_specs=pl.BlockSpec((rows, tn), lambda j: (0, j)),
        out_shape=jax.ShapeDtypeStruct((rows, n), F32),
        compiler_params=_params(1),
        name="modulation",
    )(cs, w_mod, b_mod.reshape(1, n))


def _norm_mod_kernel(x_ref, w_ref, sc_ref, sh_ref, o_ref):
    y = _rms(x_ref[...]) * w_ref[...]
    o_ref[...] = (y * (1.0 + sc_ref[...]) + sh_ref[...]).astype(o_ref.dtype)


def _norm_mod(x, w, mod3, sh_row, sc_row, tm):
    bsz, length, d = x.shape
    return pl.pallas_call(
        _norm_mod_kernel,
        grid=(bsz, length // tm),
        in_specs=[pl.BlockSpec((None, tm, d), lambda b, i: (b, i, 0)),
                  pl.BlockSpec((1, d), lambda b, i: (0, 0)),
                  pl.BlockSpec((None, 1, d), lambda b, i: (sc_row(b), 0, 0)),
                  pl.BlockSpec((None, 1, d), lambda b, i: (sh_row(b), 0, 0))],
        out_specs=pl.BlockSpec((None, tm, d), lambda b, i: (b, i, 0)),
        out_shape=jax.ShapeDtypeStruct((bsz, length, d), BF16),
        compiler_params=_params(2),
        name="norm_mod",
    )(x, w.reshape(1, d), mod3, mod3)


def _fused_matmul(a_list, w_list, w_to_a, col_list, tile_list, epilogue, out_dtypes, n, tm, tn, name):
    m, k = a_list[0].shape
    na, nw, nc, nt = len(a_list), len(w_list), len(col_list), len(tile_list)
    sub = min(tn, 2 * MXU_COLS)

    def body(*refs):
        a_refs = refs[:na]
        w_refs = refs[na:na + nw]
        c_refs = refs[na + nw:na + nw + nc]
        t_refs = refs[na + nw + nc:na + nw + nc + nt]
        o_refs = refs[na + nw + nc + nt:]
        for s in range(tn // sub):
            sl = slice(s * sub, (s + 1) * sub)
            prods = [jnp.dot(a_refs[w_to_a[i]][...], w_refs[i][:, sl].astype(BF16), preferred_element_type=F32)
                     for i in range(nw)]
            outs = epilogue(prods, [c[:, sl] for c in c_refs], [t[:, sl] for t in t_refs])
            for o_ref, o in zip(o_refs, outs):
                o_ref[:, sl] = o.astype(o_ref.dtype)

    in_specs = [pl.BlockSpec((tm, k), lambda i, j: (i, 0)) for _ in a_list]
    args = list(a_list)
    for w, off in w_list:
        assert off % LANES == 0, "weight sections must start on a lane-tile boundary"
        in_specs.append(pl.BlockSpec((pl.Element(k), pl.Element(tn)),
                                     functools.partial(lambda i, j, off: (0, pl.multiple_of(off + j * tn, LANES)),
                                                       off=off)))
        args.append(w)
    for c in col_list:
        in_specs.append(pl.BlockSpec((c.shape[0], tn), lambda i, j: (0, j)))
        args.append(c)
    for t, off in tile_list:
        in_specs.append(pl.BlockSpec((tm, tn), functools.partial(lambda i, j, off: (i, j + off), off=off)))
        args.append(t)
    return pl.pallas_call(
        body,
        grid=(m // tm, n // tn),
        in_specs=in_specs,
        out_specs=[pl.BlockSpec((tm, tn), lambda i, j: (i, j)) for _ in out_dtypes],
        out_shape=[jax.ShapeDtypeStruct((m, n), dt) for dt in out_dtypes],
        compiler_params=_params(2),
        name=name,
    )(*args)


def _conv3_rows(x, cw, seg):
    tm = x.shape[0]
    r = lax.broadcasted_iota(jnp.int32, x.shape, 0) & (seg - 1)
    xp = jnp.where(r == 0, 0.0, pltpu.roll(x, 1, 0))
    xn = jnp.where(r == seg - 1, 0.0, pltpu.roll(x, tm - 1, 0))
    return xp * cw[0:1, :] + x * cw[1:2, :] + xn * cw[2:3, :]


def _epi_identity(prods, cols, tiles):
    return [prods[0]]


def _epi_conv_silu(seg, prods, cols, tiles):
    return [_silu(_conv3_rows(prods[0], cols[0], seg) + cols[1])]


def _epi_softplus(prods, cols, tiles):
    return [_softplus(prods[0] + cols[0])]


def _epi_shortconv(seg, prods, cols, tiles):
    sc_b, sc_c, sc_v = prods
    return [sc_b * _conv3_rows(sc_c * sc_v, cols[0], seg)]


def _epi_sigmoid(prods, cols, tiles):
    return [_sigmoid(prods[0] + cols[0])]


def _epi_gate_merge(prods, cols, tiles):
    return [tiles[0] * prods[0] + tiles[1] * prods[1]]


def _split3(x):
    hi = x.astype(BF16)
    r1 = x - hi.astype(F32)
    mid = r1.astype(BF16)
    lo = (r1 - mid.astype(F32)).astype(BF16)
    return hi, mid, lo


def _cumsum_rows(tri, x):
    return sum(jnp.dot(tri, p, preferred_element_type=F32) for p in _split3(x))


def _cumsum_cols(x, tri):
    return sum(jnp.dot(p, tri, preferred_element_type=F32) for p in _split3(x))


def _pair(lo_mask, v, h):
    return jnp.where(lo_mask, v[:, h:h + 1], v[:, h + 1:h + 2])


def _spread(x, sel):
    return jnp.dot(jnp.concatenate(_split3(x), axis=1), sel, preferred_element_type=F32)


def _ssd_kernel(post, xs_ref, bc_ref, dtc_ref, dtr_ref, arow_ref, acol_ref, tri_ref, trit_ref, sel128_ref,
                sel64_ref, h0_ref, *rest):
    if post:
        yo_ref, z_ref, dcol_ref, nw_ref, y_ref, hfin_ref, st_ref = rest
    else:
        y_ref, hfin_ref, st_ref = rest
    q, n = CHUNK, SSD_STATE
    step = pl.program_id(2)

    @pl.when(step == 0)
    def _():
        st_ref[...] = h0_ref[...]

    tri = tri_ref[...]
    dtc = dtc_ref[...]
    dtr = dtr_ref[...]
    al_c = dtc * arow_ref[...]
    al_r = dtr * acol_ref[...]
    cs_c = _cumsum_rows(tri, al_c)
    cs_r = _cumsum_cols(al_r, trit_ref[...])
    tot_c = jnp.sum(al_c, axis=0, keepdims=True)
    cs_b = _spread(cs_c, sel128_ref[...])
    dt_b = _spread(dtc, sel64_ref[...])
    mask = tri.astype(F32) > 0.5
    lo_q = lax.broadcasted_iota(jnp.int32, (q, LANES), 1) < SSD_HEAD_DIM
    lo_1 = lax.broadcasted_iota(jnp.int32, (1, LANES), 1) < SSD_HEAD_DIM
    for gi in range(SSD_GROUPS):
        bm = bc_ref[:, gi * n:(gi + 1) * n]
        cm = bc_ref[:, (SSD_GROUPS + gi) * n:(SSD_GROUPS + gi + 1) * n]
        cb = lax.dot_general(cm, bm, (((1,), (1,)), ((), ())), preferred_element_type=F32)
        ht = st_ref[gi]
        y_off = jnp.dot(cm, ht.astype(BF16), preferred_element_type=F32)
        xw = []
        dec = []
        ys = []
        for k in range(SSD_HPG // 2):
            h0 = gi * SSD_HPG + 2 * k
            col = gi * GROUP_W + k * LANES
            ms = []
            for h in (h0, h0 + 1):
                seg = cs_b[:, h * LANES:(h + 1) * LANES] - cs_r[h:h + 1, :]
                lmat = jnp.exp(jnp.where(mask, seg, -1e30))
                ms.append((cb * lmat * dtr[h:h + 1, :]).astype(BF16))
            lhs = jnp.concatenate(ms, axis=1)
            xp = xs_ref[:, col:col + LANES]
            rhs = jnp.concatenate([jnp.where(lo_q, xp, 0.0).astype(BF16),
                                   jnp.where(lo_q, 0.0, xp).astype(BF16)], axis=0)
            y_diag = jnp.dot(lhs, rhs, preferred_element_type=F32)
            cs_p = jnp.where(lo_q, cs_b[:, h0 * LANES:(h0 + 1) * LANES], cs_b[:, (h0 + 1) * LANES:(h0 + 2) * LANES])
            tot_p = _pair(lo_1, tot_c, h0)
            dt_p = dt_b[:, h0 * SSD_HEAD_DIM:(h0 + 2) * SSD_HEAD_DIM]
            y_k = y_diag + y_off[:, k * LANES:(k + 1) * LANES] * jnp.exp(cs_p)
            if post:
                cs = slice(col, col + LANES)
                ys.append((y_k + yo_ref[:, cs] + dcol_ref[:, cs] * xp) * _silu(z_ref[:, cs]))
            else:
                y_ref[:, col:col + LANES] = y_k
            xw.append((xp * (jnp.exp(tot_p - cs_p) * dt_p)).astype(BF16))
            dec.append(jnp.exp(tot_p))
        xw = jnp.concatenate(xw, axis=1)
        s_t = lax.dot_general(bm, xw, (((0,), (0,)), ((), ())), preferred_element_type=F32)
        st_ref[gi] = ht * jnp.concatenate(dec, axis=1) + s_t
        if post:
            ssq = sum(jnp.sum(v * v, axis=-1, keepdims=True) for v in ys)
            inv = lax.rsqrt(ssq * (1.0 / GROUP_W) + EPS)
            for k, v in enumerate(ys):
                cs = slice(gi * GROUP_W + k * LANES, gi * GROUP_W + (k + 1) * LANES)
                y_ref[:, cs] = (v * inv * nw_ref[:, cs]).astype(y_ref.dtype)

    @pl.when(step == pl.num_programs(2) - 1)
    def _():
        hfin_ref[...] = st_ref[...]


def _ssd_scan(xs, bc, dt, a_neg, tri, h0, post=None):
    bsz, length, _ = xs.shape
    nc = length // CHUNK
    g, q, n, nh = SSD_GROUPS, CHUNK, SSD_STATE, SSD_HEADS
    dt4 = dt[:, :, :2 * nh].reshape(bsz, length, 2, nh)
    dt_col = jnp.transpose(dt4, (2, 0, 1, 3))
    dt_row = jnp.transpose(dt4, (2, 0, 3, 1))
    a_row = a_neg.reshape(2, 1, nh)
    a_col = a_neg.reshape(2, nh, 1)

    def head_select(width):
        rows = lax.broadcasted_iota(jnp.int32, (3 * nh, nh * width), 0) % nh
        cols = lax.broadcasted_iota(jnp.int32, (3 * nh, nh * width), 1) // width
        return (rows == cols).astype(BF16)

    sel128, sel64 = head_select(LANES), head_select(SSD_HEAD_DIM)

    def call(d0, nd, extra, y_dtype):
        def dd(d):
            return d + d0

        def chunk(d, c):
            return c + dd(d) * (nc - 1 - 2 * c)

        extra_specs = []
        if extra:
            extra_specs = [pl.BlockSpec((None, q, SSD_INNER), lambda d, b, c: (b, chunk(d, c), 0)),
                           pl.BlockSpec((None, q, SSD_INNER), lambda d, b, c: (b, chunk(d, c), 0)),
                           pl.BlockSpec((1, SSD_INNER), lambda d, b, c: (0, 0)),
                           pl.BlockSpec((1, SSD_INNER), lambda d, b, c: (0, 0))]
        return pl.pallas_call(
            functools.partial(_ssd_kernel, bool(extra)),
            grid=(nd, bsz, nc),
            in_specs=[
                pl.BlockSpec((None, q, SSD_INNER), lambda d, b, c: (b, chunk(d, c), 0)),
                pl.BlockSpec((None, q, 2 * g * n), lambda d, b, c: (b, chunk(d, c), 0)),
                pl.BlockSpec((None, None, q, nh), lambda d, b, c: (dd(d), b, chunk(d, c), 0)),
                pl.BlockSpec((None, None, nh, q), lambda d, b, c: (dd(d), b, 0, chunk(d, c))),
                pl.BlockSpec((None, 1, nh), lambda d, b, c: (dd(d), 0, 0)),
                pl.BlockSpec((None, nh, 1), lambda d, b, c: (dd(d), 0, 0)),
                pl.BlockSpec((None, q, q), lambda d, b, c: (dd(d), 0, 0)),
                pl.BlockSpec((None, q, q), lambda d, b, c: (1 - dd(d), 0, 0)),
                pl.BlockSpec(sel128.shape, lambda d, b, c: (0, 0)),
                pl.BlockSpec(sel64.shape, lambda d, b, c: (0, 0)),
                pl.BlockSpec((None, None, g, n, GROUP_W), lambda d, b, c: (dd(d), b, 0, 0, 0)),
            ] + extra_specs,
            out_specs=[
                pl.BlockSpec((None, None, q, SSD_INNER), lambda d, b, c: (d, b, chunk(d, c), 0)),
                pl.BlockSpec((None, None, g, n, GROUP_W), lambda d, b, c: (d, b, 0, 0, 0)),
            ],
            out_shape=[jax.ShapeDtypeStruct((nd, bsz, length, SSD_INNER), y_dtype),
                       jax.ShapeDtypeStruct((nd, bsz, g, n, GROUP_W), F32)],
            scratch_shapes=[pltpu.VMEM((g, n, GROUP_W), F32)],
            compiler_params=_params(3),
            name="ssd_scan",
        )(xs, bc, dt_col, dt_row, a_row, a_col, tri, tri, sel128, sel64, h0, *extra)

    if post is None:
        return call(0, 2, (), F32)
    y_fwd, _ = call(0, 1, (), F32)
    y_out, _ = call(1, 1, (y_fwd[0],) + tuple(post), BF16)
    return y_out[0]


def _mix_out_kernel(m_ref, wo_ref, x_ref, gx_ref, scf_ref, shf_ref, npost_ref, npre_ref, wr_ref,
                    x1_ref, h2_ref, aff_ref):
    half = m_ref.shape[0] // 2
    for p in range(2):
        rs = slice(p * half, (p + 1) * half)
        out = jnp.dot(m_ref[rs, :], wo_ref[...], preferred_element_type=F32)
        x1 = x_ref[rs, :] + gx_ref[...] * (_rms(out) * npost_ref[...])
        x1_ref[rs, :] = x1
        h2 = (_rms(x1) * npre_ref[...]) * (1.0 + scf_ref[...]) + shf_ref[...]
        h2_ref[rs, :] = h2
        logits = jnp.dot(h2.astype(BF16), wr_ref[...], preferred_element_type=F32)
        lane = lax.broadcasted_iota(jnp.int32, logits.shape, 1)
        logits = jnp.where(lane < N_EXPERTS, logits, -1e30)
        e = jnp.exp(logits - jnp.max(logits, axis=-1, keepdims=True))
        aff_ref[rs, :] = e / jnp.sum(e, axis=-1, keepdims=True)


def _mix_out(merged, w_o, x2d, mod3, npost, npre, w_router_pad, length, tm):
    m, d = merged.shape
    nt = length // tm
    row = lambda kind: (lambda i: ((i // nt) * 6 + kind, 0, 0))
    return pl.pallas_call(
        _mix_out_kernel,
        grid=(m // tm,),
        in_specs=[pl.BlockSpec((tm, d), lambda i: (i, 0)),
                  pl.BlockSpec((d, d), lambda i: (0, 0), pipeline_mode=pl.Buffered(1)),
                  pl.BlockSpec((tm, d), lambda i: (i, 0)),
                  pl.BlockSpec((None, 1, d), row(2)),
                  pl.BlockSpec((None, 1, d), row(4)),
                  pl.BlockSpec((None, 1, d), row(3)),
                  pl.BlockSpec((1, d), lambda i: (0, 0)),
                  pl.BlockSpec((1, d), lambda i: (0, 0)),
                  pl.BlockSpec((d, LANES), lambda i: (0, 0))],
        out_specs=[pl.BlockSpec((tm, d), lambda i: (i, 0)),
                   pl.BlockSpec((tm, d), lambda i: (i, 0)),
                   pl.BlockSpec((tm, LANES), lambda i: (i, 0))],
        out_shape=[jax.ShapeDtypeStruct((m, d), F32),
                   jax.ShapeDtypeStruct((m, d), F32),
                   jax.ShapeDtypeStruct((m, LANES), F32)],
        compiler_params=_params(1),
        name="mix_out_router",
    )(merged, w_o, x2d, mod3, mod3, mod3, npost, npre, w_router_pad)


def _route_select(cap, aff_ref, triu_ref, low_ref, pos_ref, slot_ref, ts_ref):
    aff = aff_ref[...]
    e_cnt, n_tok = aff.shape

    def count(m):
        return jnp.sum(jnp.where(m, 1.0, 0.0), axis=1, keepdims=True)

    def search_bits(_, carry):
        lo, hi = carry
        mid = lo + ((hi - lo + 1) >> 1)
        ok = count(aff >= pltpu.bitcast(mid, F32)) >= cap
        return jnp.where(ok, mid, lo), jnp.where(ok, hi, mid - 1)

    lo0 = jnp.zeros((e_cnt, 1), jnp.int32)
    hi0 = jnp.full((e_cnt, 1), 0x7F800000, jnp.int32)
    lo_b, _ = lax.fori_loop(0, 32, search_bits, (lo0, hi0))

    def search_mid(_, carry):
        lo, hi = carry
        mid = 0.5 * (lo + hi)
        ok = count(aff >= mid) >= cap
        return jnp.where(ok, mid, lo), jnp.where(ok, hi, mid)

    lo_f, _ = lax.fori_loop(0, 24, search_mid, (pltpu.bitcast(lo_b, F32), pltpu.bitcast(lo_b + 1, F32)))
    thr = jnp.min(jnp.where(aff >= lo_f, aff, jnp.inf), axis=1, keepdims=True)
    need_f = cap - count(aff > thr)
    triu = triu_ref[...]
    carry_eq = jnp.zeros((e_cnt, 1), F32)
    carry_sel = jnp.zeros((e_cnt, 1), F32)
    for k in range(n_tok // LANES):
        sl = slice(k * LANES, (k + 1) * LANES)
        aff_k = aff[:, sl]
        eq_f = jnp.where(aff_k == thr, 1.0, 0.0)
        eq_incl = jnp.dot(eq_f.astype(BF16), triu, preferred_element_type=F32) + carry_eq
        sel_k = (aff_k > thr) | ((aff_k == thr) & (eq_incl <= need_f))
        sel_f = jnp.where(sel_k, 1.0, 0.0)
        sel_b = sel_f.astype(BF16)
        sel_incl = jnp.dot(sel_b, triu, preferred_element_type=F32) + carry_sel
        excl = sel_incl - sel_f
        pos_ref[:, sl] = jnp.where(sel_k, excl.astype(jnp.int32), -1)
        tok_start = jnp.sum(excl, axis=0, keepdims=True)
        ts_ref[:, sl] = tok_start
        slot_ref[:, sl] = tok_start + jnp.dot(low_ref[...], sel_b, preferred_element_type=F32)
        carry_eq = eq_incl[:, LANES - 1:LANES]
        carry_sel = sel_incl[:, LANES - 1:LANES]


def _route_kernel(cap, aff_ref, triu_ref, low_ref, list_ref, ts_ref, pos_ref, slot_ref):
    e = pl.program_id(1)

    @pl.when(e == 0)
    def _():
        _route_select(cap, aff_ref, triu_ref, low_ref, pos_ref, slot_ref, ts_ref)

    n_tok = aff_ref.shape[1]
    pos = pos_ref[pl.ds(e, 1), :]
    slot = slot_ref[pl.ds(e, 1), :]
    aff = aff_ref[pl.ds(e, 1), :]
    hit = lax.broadcasted_iota(jnp.int32, (cap, n_tok), 0) == pos
    tok = lax.broadcasted_iota(jnp.int32, (1, n_tok), 1)
    slot_hi = jnp.floor(slot * (1.0 / LANES))
    a_hi = aff.astype(BF16).astype(F32)
    a_mid = (aff - a_hi).astype(BF16).astype(F32)
    rows = [(tok >> 6).astype(F32), (tok & 63).astype(F32), slot_hi, slot - slot_hi * LANES,
            a_hi, a_mid, aff - a_hi - a_mid]
    sub = lax.broadcasted_iota(jnp.int32, (8, n_tok), 0)
    vals = jnp.zeros((8, n_tok), F32)
    for i, r in enumerate(rows):
        vals = jnp.where(sub == i, r, vals)
    list_ref[...] = lax.dot_general(jnp.where(hit, 1.0, 0.0).astype(BF16), vals.astype(BF16),
                                    (((1,), (1,)), ((), ())), preferred_element_type=F32)


def _route(aff_t, triu, low, cap):
    bsz, e_cnt, n_tok = aff_t.shape
    return pl.pallas_call(
        functools.partial(_route_kernel, cap),
        grid=(bsz, e_cnt),
        in_specs=[pl.BlockSpec((None, e_cnt, n_tok), lambda b, e: (b, 0, 0)),
                  pl.BlockSpec((LANES, LANES), lambda b, e: (0, 0)),
                  pl.BlockSpec((e_cnt, e_cnt), lambda b, e: (0, 0))],
        out_specs=[pl.BlockSpec((None, None, cap, 8), lambda b, e: (b, e, 0, 0)),
                   pl.BlockSpec((None, 1, n_tok), lambda b, e: (b, 0, 0))],
        out_shape=[jax.ShapeDtypeStruct((bsz, e_cnt, cap, 8), F32),
                   jax.ShapeDtypeStruct((bsz, 1, n_tok), F32)],
        scratch_shapes=[pltpu.VMEM((e_cnt, n_tok), jnp.int32), pltpu.VMEM((e_cnt, n_tok), F32)],
        compiler_params=_params(2),
        name="route_topk",
    )(aff_t, triu, low)


def _expert_kernel(cap, n_e, n_f, n_pairs, tf, idx_ref, dst_ref, h_hbm, w1_hbm, w3_hbm, w2_hbm, val_ref, z_hbm,
                   gbuf, obuf, xg_ref, acc_ref, w1_buf, w3_buf, w2_buf, gsem, ssem, wsem):
    e, f, j = pl.program_id(1), pl.program_id(2), pl.program_id(3)
    pair = pl.program_id(0) * n_e + e
    nxt = jnp.minimum(pair + 1, n_pairs - 1)
    d = h_hbm.shape[1]
    rows = cap // n_f
    s = 2 * f + j
    part = lax.rem(s, n_f)
    wslot = lax.rem(pair * n_f + f, 2)

    def weight_copies(ee, ff, slot):
        col = pl.multiple_of(ff * tf, tf)
        return (pltpu.make_async_copy(w1_hbm.at[ee, :, pl.ds(col, tf)], w1_buf.at[slot], wsem.at[slot, 0]),
                pltpu.make_async_copy(w3_hbm.at[ee, :, pl.ds(col, tf)], w3_buf.at[slot], wsem.at[slot, 1]),
                pltpu.make_async_copy(w2_hbm.at[ee, pl.ds(col, tf), :], w2_buf.at[slot], wsem.at[slot, 2]))

    def gather(p, t, q, r):
        src = idx_ref[(2 * p + t) * cap + q * rows + r]
        return pltpu.make_async_copy(h_hbm.at[pl.ds(src, 1), :], gbuf.at[t, q, pl.ds(r, 1), :], gsem.at[0])

    def scatter(pos, t, q, r):
        dst = dst_ref[(2 * pos + t) * cap + q * rows + r]
        return pltpu.make_async_copy(obuf.at[t, q, pl.ds(r, 1), :], z_hbm.at[pl.ds(dst, 1), :], ssem.at[t])

    def for_all_rows(fn):
        def body(r, carry):
            for t in range(2):
                for q in range(n_f):
                    fn(t, q, r)
            return carry

        lax.fori_loop(0, rows, body, 0)

    @pl.when(jnp.logical_and(pair == 0, s == 0))
    def _():
        for c in weight_copies(e, f, wslot):
            c.start()
        obuf[...] = jnp.zeros_like(obuf)
        for_all_rows(lambda t, q, r: gather(0, t, q, r).start())

    @pl.when(j == 0)
    def _():
        last_f = f == n_f - 1
        e_nxt = jnp.where(last_f, jnp.where(e == n_e - 1, 0, e + 1), e)
        f_nxt = jnp.where(last_f, 0, f + 1)

        @pl.when(jnp.logical_not(jnp.logical_and(pair == n_pairs - 1, last_f)))
        def _():
            for c in weight_copies(e_nxt, f_nxt, 1 - wslot):
                c.start()

        for c in weight_copies(e, f, wslot):
            c.wait()

    @pl.when(s == 0)
    def _():
        pltpu.make_async_copy(gbuf, gbuf, gsem.at[0]).wait()
        for t in range(2):
            for q in range(n_f):
                xg_ref[t, q * rows:(q + 1) * rows, :] = gbuf[t, q].astype(BF16)

    @pl.when(f == 0)
    def _():
        acc_ref[j] = jnp.zeros((cap, d), F32)

    @pl.when(s < n_f)
    def _():
        for t in range(2):
            for r in range(rows):
                scatter(pair, t, part, r).start()

    @pl.when(s >= n_f)
    def _():
        for t in range(2):
            for r in range(rows):
                gather(nxt, t, part, r).start()

    xg = xg_ref[j]
    hs = []
    for c in range(tf // MXU_COLS):
        sl = slice(c * MXU_COLS, (c + 1) * MXU_COLS)
        g = jnp.dot(xg, w1_buf[wslot, :, sl].astype(BF16), preferred_element_type=F32)
        u = jnp.dot(xg, w3_buf[wslot, :, sl].astype(BF16), preferred_element_type=F32)
        hs.append((_silu(g) * u).astype(BF16))
    h = jnp.concatenate(hs, axis=1)
    for c in range(d // MXU_COLS):
        sl = slice(c * MXU_COLS, (c + 1) * MXU_COLS)
        acc_ref[j, :, sl] += jnp.dot(h, w2_buf[wslot, :, sl].astype(BF16), preferred_element_type=F32)

    @pl.when(f == n_f - 1)
    def _():
        pltpu.make_async_copy(obuf.at[j], obuf.at[j], ssem.at[j]).wait()
        for q in range(n_f):
            sl = slice(q * rows, (q + 1) * rows)
            obuf[j, q] = acc_ref[j, sl, :] * val_ref[sl, :]

        @pl.when(jnp.logical_and(pair == n_pairs - 1, j == 1))
        def _():
            for_all_rows(lambda t, q, r: scatter(n_pairs, t, q, r).start())
            for t in range(2):
                pltpu.make_async_copy(obuf.at[t], obuf.at[t], ssem.at[t]).wait()
            pltpu.make_async_copy(gbuf, gbuf, gsem.at[0]).wait()


def _experts(idx, dst_ext, h2, w1, w3, w2, vals, n_rows_out, tf):
    bsz, e_cnt, cap, _ = vals.shape
    d = h2.shape[-1]
    n_f = w1.shape[-1] // tf
    n_pairs = bsz // 2 * e_cnt
    grid_spec = pltpu.PrefetchScalarGridSpec(
        num_scalar_prefetch=2,
        grid=(bsz // 2, e_cnt, n_f, 2),
        in_specs=[pl.BlockSpec(memory_space=pl.ANY),
                  pl.BlockSpec(memory_space=pl.ANY),
                  pl.BlockSpec(memory_space=pl.ANY),
                  pl.BlockSpec(memory_space=pl.ANY),
                  pl.BlockSpec((None, None, cap, 1), lambda bh, e, f, j, *_: (2 * bh + j, e, 0, 0))],
        out_specs=pl.BlockSpec(memory_space=pl.ANY),
        scratch_shapes=[pltpu.VMEM((2, n_f, cap // n_f, d), F32), pltpu.VMEM((2, n_f, cap // n_f, d), F32),
                        pltpu.VMEM((2, cap, d), BF16), pltpu.VMEM((2, cap, d), F32),
                        pltpu.VMEM((2, d, tf), F32), pltpu.VMEM((2, d, tf), F32), pltpu.VMEM((2, tf, d), F32),
                        pltpu.SemaphoreType.DMA((1,)), pltpu.SemaphoreType.DMA((2,)),
                        pltpu.SemaphoreType.DMA((2, 3))],
    )
    return pl.pallas_call(
        functools.partial(_expert_kernel, cap, e_cnt, n_f, n_pairs, tf),
        grid_spec=grid_spec,
        out_shape=jax.ShapeDtypeStruct((n_rows_out, d), F32),
        compiler_params=_params(4),
        name="moe_experts",
    )(idx.reshape(-1), dst_ext.reshape(-1), h2, w1, w3, w2, vals)


COMBINE_ROWS = 256


COMBINE_SLOTS = 4


def _combine_kernel(n_slot, nb, nt, starts_ref, ts_ref, x1_ref, g_ref, w_ref, z_hbm, o_ref, zbuf, acc_ref, sem,
                    st_ref):
    b, t = pl.program_id(0), pl.program_id(1)
    shift = COMBINE_ROWS.bit_length() - 1

    def tile_chunks(bb, tt):
        s0 = starts_ref[bb * (nt + 1) + tt]
        s1 = starts_ref[bb * (nt + 1) + tt + 1]
        start = lax.shift_left(lax.shift_right_logical(s0, 3), 3)
        return start, jnp.maximum(lax.shift_right_logical(s1 - start + COMBINE_ROWS - 1, shift), 1)

    def copy(row, slot):
        return pltpu.make_async_copy(z_hbm.at[pl.ds(row, COMBINE_ROWS), :], zbuf.at[slot], sem.at[slot])

    def fetch_one():
        tile = st_ref[2]

        @pl.when(tile < nb * nt)
        def _():
            bb = lax.div(tile, nt)
            start, n = tile_chunks(bb, tile - bb * nt)
            c, k = st_ref[3], st_ref[1]
            copy(pl.multiple_of(bb * n_slot + start + c * COMBINE_ROWS, 8), lax.rem(k, COMBINE_SLOTS)).start()
            st_ref[1] = k + 1
            tile_done = c + 1 == n
            st_ref[2] = jnp.where(tile_done, tile + 1, tile)
            st_ref[3] = jnp.where(tile_done, 0, c + 1)

    @pl.when(jnp.logical_and(b == 0, t == 0))
    def _():
        for i in range(4):
            st_ref[i] = 0
        for _ in range(COMBINE_SLOTS - 1):
            fetch_one()

    start0, n_chunks = tile_chunks(b, t)
    acc_ref[...] = jnp.zeros_like(acc_ref)
    first = ts_ref[:, 0:1]
    last = ts_ref[:, 1:2]

    def body(c, carry):
        k = st_ref[0]
        slot = lax.rem(k, COMBINE_SLOTS)
        copy(0, slot).wait()
        st_ref[0] = k + 1
        fetch_one()
        rows = zbuf[slot]
        j = (start0 + c * COMBINE_ROWS + lax.broadcasted_iota(jnp.int32, (1, COMBINE_ROWS), 1)).astype(F32)
        seg = jnp.where(j >= first, jnp.where(j < last, 1.0, 0.0), 0.0).astype(BF16)
        hi = rows.astype(BF16)
        lo = (rows - hi.astype(F32)).astype(BF16)
        acc_ref[...] += (jnp.dot(seg, hi, preferred_element_type=F32)
                         + jnp.dot(seg, lo, preferred_element_type=F32))
        return carry

    lax.fori_loop(0, n_chunks, body, 0)
    o_ref[...] = x1_ref[...] + g_ref[...] * (_rms(acc_ref[...]) * w_ref[...])


def _combine(starts, ts2, z, x1, mod3, npost_ffn, n_slot, tm):
    bsz, n_tok, d = x1.shape
    grid_spec = pltpu.PrefetchScalarGridSpec(
        num_scalar_prefetch=1,
        grid=(bsz, n_tok // tm),
        in_specs=[pl.BlockSpec((None, tm, 2), lambda b, t, *_: (b, t, 0)),
                  pl.BlockSpec((None, tm, d), lambda b, t, *_: (b, t, 0)),
                  pl.BlockSpec((None, 1, d), lambda b, t, *_: (b * 6 + 5, 0, 0)),
                  pl.BlockSpec((1, d), lambda b, t, *_: (0, 0)),
                  pl.BlockSpec(memory_space=pl.ANY)],
        out_specs=pl.BlockSpec((None, tm, d), lambda b, t, *_: (b, t, 0)),
        scratch_shapes=[pltpu.VMEM((COMBINE_SLOTS, COMBINE_ROWS, d), F32), pltpu.VMEM((tm, d), F32),
                        pltpu.SemaphoreType.DMA((COMBINE_SLOTS,)), pltpu.SMEM((4,), jnp.int32)],
    )
    return pl.pallas_call(
        functools.partial(_combine_kernel, n_slot, bsz, n_tok // tm),
        grid_spec=grid_spec,
        out_shape=jax.ShapeDtypeStruct((bsz, n_tok, d), F32),
        compiler_params=_params(2),
        name="moe_combine",
    )(starts, ts2, x1, mod3, npost_ffn, z)


def _ssd_inputs(h, w_xs, w_bc, w_dt, conv_w, conv_b, dt_bias_pad, seg, tm):
    cw_x, cw_bc = conv_w[:, :SSD_INNER], conv_w[:, SSD_INNER:]
    cb_x, cb_bc = conv_b[:, :SSD_INNER], conv_b[:, SSD_INNER:]
    epi = functools.partial(_epi_conv_silu, seg)
    (xs,) = _fused_matmul([h], [w_xs], [0], [cw_x, cb_x], [], epi, [F32], SSD_INNER, tm, 1024, "proj_xs")
    (bc,) = _fused_matmul([h], [w_bc], [0], [cw_bc, cb_bc], [], epi, [BF16],
                          2 * SSD_GROUPS * SSD_STATE, tm, 1024, "proj_bc")
    (dt,) = _fused_matmul([h], [w_dt], [0], [dt_bias_pad], [], _epi_softplus, [F32], LANES, tm, LANES,
                          "proj_dt")
    return xs, bc, dt


def kernel(x, c, ctx, c_ctx, w_mod, b_mod, norm_pre_mix, norm_post_mix, w_in, ssd_conv_w, ssd_conv_b,
           dt_bias, a_log, ssd_d, ssd_norm_w, w_ssd_out, sc_conv_w, w_sc_out, b_gate, w_o, norm_pre_ffn,
           norm_post_ffn, w_router, w_e1, w_e3, w_e2):
    assert w_mod.shape[0] == 1, "single-layer block"
    bsz, length, d = x.shape
    ctx_len = ctx.shape[1]
    m = bsz * length
    cap = CAPACITY_FACTOR * length // N_EXPERTS
    row = lambda v: v.reshape(1, -1)

    o_z, o_xbc, o_dt, o_scb, o_scc, o_scv = (SSD_INNER, SSD_INNER + SSD_XBC,
                                             SSD_INNER + SSD_XBC + 2 * SSD_HEADS,
                                             SSD_INNER + SSD_XBC + 2 * SSD_HEADS + SC_DIM,
                                             SSD_INNER + SSD_XBC + 2 * SSD_HEADS + 2 * SC_DIM,
                                             SSD_INNER + SSD_XBC + 2 * SSD_HEADS + 3 * SC_DIM)
    w_in0 = w_in[0]
    w_z, w_xs, w_bc, w_dt = (w_in0, 0), (w_in0, o_z), (w_in0, o_z + SSD_INNER), (w_in0, o_xbc)
    w_rest = w_in0[:, o_dt:].astype(BF16)
    w_scb, w_scc, w_scv, w_gate = ((w_rest, 0), (w_rest, o_scb - o_dt), (w_rest, o_scc - o_dt),
                                   (w_rest, o_scv - o_dt))
    dt_bias_pad = jnp.pad(dt_bias[0].reshape(1, -1), ((0, 0), (0, LANES - 2 * SSD_HEADS)))
    a_neg = -jnp.exp(a_log[0])
    ii = lax.broadcasted_iota(jnp.int32, (CHUNK, CHUNK), 0)
    jj = lax.broadcasted_iota(jnp.int32, (CHUNK, CHUNK), 1)
    tri = jnp.stack([ii >= jj, ii <= jj]).astype(BF16)
    conv_w, conv_b = ssd_conv_w[0], row(ssd_conv_b[0])

    cs = jnp.concatenate([c, c_ctx[None, :], jnp.zeros((8 - bsz - 1, d), F32)], axis=0)
    mod3 = _modulation(cs, w_mod[0], b_mod[0]).reshape(8 * 6, 1, d)

    hc = _norm_mod(ctx, norm_pre_mix[0], mod3, lambda b: bsz * 6, lambda b: bsz * 6 + 1, ctx_len)
    xs_c, bc_c, dt_c = _ssd_inputs(hc.reshape(bsz * ctx_len, d), w_xs, w_bc, w_dt, conv_w, conv_b,
                                   dt_bias_pad, ctx_len, ctx_len)
    h_zero = jnp.zeros((2, bsz, SSD_GROUPS, SSD_STATE, GROUP_W), F32)
    _, states = _ssd_scan(xs_c.reshape(bsz, ctx_len, -1), bc_c.reshape(bsz, ctx_len, -1),
                          dt_c.reshape(bsz, ctx_len, -1), a_neg, tri, h_zero)

    tm = 1024
    hx = _norm_mod(x, norm_pre_mix[0], mod3, lambda b: b * 6, lambda b: b * 6 + 1, tm).reshape(m, d)
    (z,) = _fused_matmul([hx], [w_z], [0], [], [], _epi_identity, [F32], SSD_INNER, tm, 1024, "proj_z")
    xs, bc, dt = _ssd_inputs(hx, w_xs, w_bc, w_dt, conv_w, conv_b, dt_bias_pad, GRID_W, tm)
    (sc_y,) = _fused_matmul([hx], [w_scb, w_scc, w_scv], [0, 0, 0], [sc_conv_w[0]], [],
                            functools.partial(_epi_shortconv, GRID_W), [BF16], SC_DIM, tm, 512, "proj_sc")
    (gates,) = _fused_matmul([hx], [w_gate], [0], [row(b_gate[0])], [], _epi_sigmoid, [F32],
                             2 * d, tm, 1024, "proj_gate")
    d_cols = jnp.repeat(ssd_d[0], SSD_HEAD_DIM).reshape(1, -1)
    y_norm = _ssd_scan(xs.reshape(bsz, length, -1), bc.reshape(bsz, length, -1), dt.reshape(bsz, length, -1),
                       a_neg, tri, states, post=(z.reshape(bsz, length, -1), d_cols, row(ssd_norm_w[0])))
    y_norm = y_norm.reshape(m, -1)
    tn = 512
    (merged,) = _fused_matmul([y_norm, sc_y], [(w_ssd_out[0], 0), (w_sc_out[0], 0)],
                              [0, 1], [], [(gates, 0), (gates, d // tn)], _epi_gate_merge, [BF16], d, tm, tn,
                              "mix_merge")
    w_router_pad = jnp.pad(w_router[0], ((0, 0), (0, LANES - N_EXPERTS))).astype(BF16)
    x1, h2, aff = _mix_out(merged, w_o[0].astype(BF16), x.reshape(m, d), mod3, row(norm_post_mix[0]),
                           row(norm_pre_ffn[0]), w_router_pad, length, 512)

    aff_t = jnp.transpose(aff[:, :N_EXPERTS].reshape(bsz, length, N_EXPERTS), (0, 2, 1))
    e_i = lax.broadcasted_iota(jnp.int32, (N_EXPERTS, N_EXPERTS), 0)
    e_j = lax.broadcasted_iota(jnp.int32, (N_EXPERTS, N_EXPERTS), 1)
    lists, ts = _route(aff_t, tri[1], (e_j < e_i).astype(BF16), cap)
    n_slot = N_EXPERTS * cap
    li = lists[..., :4].astype(jnp.int32)
    b_i = jnp.arange(bsz, dtype=jnp.int32)[:, None, None]
    def pair_order(a):
        return jnp.transpose(a.reshape(bsz // 2, 2, N_EXPERTS, cap), (0, 2, 1, 3)).reshape(-1, cap)

    idx = pair_order(li[..., 0] * 64 + li[..., 1] + b_i * length)
    dst = pair_order(li[..., 2] * LANES + li[..., 3] + b_i * n_slot)
    n_pad = 2 * cap
    dst_ext = jnp.concatenate([bsz * n_slot + jnp.arange(n_pad, dtype=jnp.int32).reshape(2, cap), dst], axis=0)
    vals = (lists[..., 4] + lists[..., 5] + lists[..., 6])[..., None]
    z = _experts(idx, dst_ext, h2, w_e1[0], w_e3[0], w_e2[0], vals, bsz * n_slot + n_pad, MXU_COLS)
    tm_c = 256
    ts = ts.reshape(bsz, length)
    end = jnp.full((bsz, 1), n_slot, F32)
    ts2 = jnp.stack([ts, jnp.concatenate([ts[:, 1:], end], axis=1)], axis=-1)
    starts = jnp.concatenate([ts[:, ::tm_c], end], axis=1).astype(jnp.int32).reshape(-1)
    return _combine(starts, ts2, z, x1.reshape(bsz, length, d), mod3, row(norm_post_ffn[0]), n_slot, tm_c)
```

```python
import functools
import math

import jax
import jax.numpy as jnp
from jax import lax
from jax.experimental import pallas as pl
from jax.experimental.pallas import tpu as pltpu

F32 = jnp.float32
BF16 = jnp.bfloat16

D_MODEL = 2048
GRID_W = 64
SSD_HEADS = 32
SSD_HEAD_DIM = 64
SSD_INNER = SSD_HEADS * SSD_HEAD_DIM
SSD_GROUPS = 4
SSD_HPG = SSD_HEADS // SSD_GROUPS
SSD_STATE = 128
CHUNK = 128
SSD_XBC = SSD_INNER + 2 * SSD_GROUPS * SSD_STATE
SC_DIM = 2048
N_EXPERTS = 16
EXPERT_FF = 2048
CAPACITY_FACTOR = 2
EPS = 1e-6
GROUP_W = SSD_HPG * SSD_HEAD_DIM

LANES = 128
MXU_COLS = 256
V7X_VMEM_BYTES = 64 * 1024 * 1024
VMEM_LIMIT = V7X_VMEM_BYTES - 8 * 1024 * 1024


def _params(n_grid):
    return pltpu.CompilerParams(dimension_semantics=("arbitrary",) * n_grid,
                                vmem_limit_bytes=VMEM_LIMIT)


def _silu(x):
    return x * (1.0 / (1.0 + jnp.exp(-x)))


def _sigmoid(x):
    return 1.0 / (1.0 + jnp.exp(-x))


def _softplus(x):
    return jnp.maximum(x, 0.0) + jnp.log(1.0 + jnp.exp(-jnp.abs(x)))


def _rms(x):
    return x * lax.rsqrt(jnp.mean(x * x, axis=-1, keepdims=True) + EPS)


def _mod_kernel(c_ref, w_ref, b_ref, o_ref):
    a = _silu(c_ref[...]).astype(BF16)
    o_ref[...] = jnp.dot(a, w_ref[...].astype(BF16), preferred_element_type=F32) + b_ref[...]


def _modulation(cs, w_mod, b_mod):
    rows, d = cs.shape
    n = w_mod.shape[1]
    tn = 1024
    return pl.pallas_call(
        _mod_kernel,
        grid=(n // tn,),
        in_specs=[pl.BlockSpec((rows, d), lambda j: (0, 0)),
                  pl.BlockSpec((d, tn), lambda j: (0, j)),
                  pl.BlockSpec((1, tn), lambda j: (0, j))],
        out_specs=pl.BlockSpec((rows, tn), lambda j: (0, j)),
        out_shape=jax.ShapeDtypeStruct((rows, n), F32),
        compiler_params=_params(1),
        name="modulation",
    )(cs, w_mod, b_mod.reshape(1, n))


def _norm_mod_kernel(x_ref, w_ref, sc_ref, sh_ref, o_ref):
    y = _rms(x_ref[...]) * w_ref[...]
    o_ref[...] = (y * (1.0 + sc_ref[...]) + sh_ref[...]).astype(o_ref.dtype)


def _norm_mod(x, w, mod3, sh_row, sc_row, tm):
    bsz, length, d = x.shape
    return pl.pallas_call(
        _norm_mod_kernel,
        grid=(bsz, length // tm),
        in_specs=[pl.BlockSpec((None, tm, d), lambda b, i: (b, i, 0)),
                  pl.BlockSpec((1, d), lambda b, i: (0, 0)),
                  pl.BlockSpec((None, 1, d), lambda b, i: (sc_row(b), 0, 0)),
                  pl.BlockSpec((None, 1, d), lambda b, i: (sh_row(b), 0, 0))],
        out_specs=pl.BlockSpec((None, tm, d), lambda b, i: (b, i, 0)),
        out_shape=jax.ShapeDtypeStruct((bsz, length, d), BF16),
        compiler_params=_params(2),
        name="norm_mod",
    )(x, w.reshape(1, d), mod3, mod3)


def _fused_matmul(a_list, w_list, w_to_a, col_list, tile_list, epilogue, out_dtypes, n, tm, tn, name):
    m, k = a_list[0].shape
    na, nw, nc, nt = len(a_list), len(w_list), len(col_list), len(tile_list)
    sub = min(tn, 2 * MXU_COLS)

    def body(*refs):
        a_refs = refs[:na]
        w_refs = refs[na:na + nw]
        c_refs = refs[na + nw:na + nw + nc]
        t_refs = refs[na + nw + nc:na + nw + nc + nt]
        o_refs = refs[na + nw + nc + nt:]
        for s in range(tn // sub):
            sl = slice(s * sub, (s + 1) * sub)
            prods = [jnp.dot(a_refs[w_to_a[i]][...], w_refs[i][:, sl], preferred_element_type=F32)
                     for i in range(nw)]
            outs = epilogue(prods, [c[:, sl] for c in c_refs], [t[:, sl] for t in t_refs])
            for o_ref, o in zip(o_refs, outs):
                o_ref[:, sl] = o.astype(o_ref.dtype)

    in_specs = [pl.BlockSpec((tm, k), lambda i, j: (i, 0)) for _ in a_list]
    args = list(a_list)
    for w, off in w_list:
        assert off % LANES == 0, "weight sections must start on a lane-tile boundary"
        in_specs.append(pl.BlockSpec((pl.Element(k), pl.Element(tn)),
                                     functools.partial(lambda i, j, off: (0, pl.multiple_of(off + j * tn, LANES)),
                                                       off=off)))
        args.append(w)
    for c in col_list:
        in_specs.append(pl.BlockSpec((c.shape[0], tn), lambda i, j: (0, j)))
        args.append(c)
    for t, off in tile_list:
        in_specs.append(pl.BlockSpec((tm, tn), functools.partial(lambda i, j, off: (i, j + off), off=off)))
        args.append(t)
    return pl.pallas_call(
        body,
        grid=(m // tm, n // tn),
        in_specs=in_specs,
        out_specs=[pl.BlockSpec((tm, tn), lambda i, j: (i, j)) for _ in out_dtypes],
        out_shape=[jax.ShapeDtypeStruct((m, n), dt) for dt in out_dtypes],
        compiler_params=_params(2),
        name=name,
    )(*args)


def _conv3_rows(x, cw, seg):
    tm = x.shape[0]
    r = lax.broadcasted_iota(jnp.int32, x.shape, 0) & (seg - 1)
    xp = jnp.where(r == 0, 0.0, pltpu.roll(x, 1, 0))
    xn = jnp.where(r == seg - 1, 0.0, pltpu.roll(x, tm - 1, 0))
    return xp * cw[0:1, :] + x * cw[1:2, :] + xn * cw[2:3, :]


def _epi_identity(prods, cols, tiles):
    return [prods[0]]


def _epi_conv_silu(seg, prods, cols, tiles):
    return [_silu(_conv3_rows(prods[0], cols[0], seg) + cols[1])]


def _epi_softplus(prods, cols, tiles):
    return [_softplus(prods[0] + cols[0])]


def _epi_shortconv(seg, prods, cols, tiles):
    sc_b, sc_c, sc_v = prods
    return [sc_b * _conv3_rows(sc_c * sc_v, cols[0], seg)]


def _epi_sigmoid(prods, cols, tiles):
    return [_sigmoid(prods[0] + cols[0])]


def _epi_gate_merge(prods, cols, tiles):
    return [tiles[0] * prods[0] + tiles[1] * prods[1]]


def _split3(x):
    hi = x.astype(BF16)
    r1 = x - hi.astype(F32)
    mid = r1.astype(BF16)
    lo = (r1 - mid.astype(F32)).astype(BF16)
    return hi, mid, lo


def _cumsum_rows(tri, x):
    return sum(jnp.dot(tri, p, preferred_element_type=F32) for p in _split3(x))


def _cumsum_cols(x, tri):
    return sum(jnp.dot(p, tri, preferred_element_type=F32) for p in _split3(x))


def _pair(lo_mask, v, h):
    return jnp.where(lo_mask, v[:, h:h + 1], v[:, h + 1:h + 2])


def _spread(x, sel):
    return jnp.dot(jnp.concatenate(_split3(x), axis=1), sel, preferred_element_type=F32)


def _ssd_kernel(post, xs_ref, bc_ref, dtc_ref, dtr_ref, arow_ref, acol_ref, tri_ref, trit_ref, sel128_ref,
                sel64_ref, h0_ref, *rest):
    if post:
        yo_ref, z_ref, dcol_ref, nw_ref, y_ref, hfin_ref, st_ref = rest
    else:
        y_ref, hfin_ref, st_ref = rest
    q, n = CHUNK, SSD_STATE
    step = pl.program_id(2)

    @pl.when(step == 0)
    def _():
        st_ref[...] = h0_ref[...]

    tri = tri_ref[...]
    dtc = dtc_ref[...]
    dtr = dtr_ref[...]
    al_c = dtc * arow_ref[...]
    al_r = dtr * acol_ref[...]
    cs_c = _cumsum_rows(tri, al_c)
    cs_r = _cumsum_cols(al_r, trit_ref[...])
    tot_c = jnp.sum(al_c, axis=0, keepdims=True)
    cs_b = _spread(cs_c, sel128_ref[...])
    dt_b = _spread(dtc, sel64_ref[...])
    mask = tri.astype(F32) > 0.5
    lo_q = lax.broadcasted_iota(jnp.int32, (q, LANES), 1) < SSD_HEAD_DIM
    lo_1 = lax.broadcasted_iota(jnp.int32, (1, LANES), 1) < SSD_HEAD_DIM
    for gi in range(SSD_GROUPS):
        bm = bc_ref[:, gi * n:(gi + 1) * n]
        cm = bc_ref[:, (SSD_GROUPS + gi) * n:(SSD_GROUPS + gi + 1) * n]
        cb = lax.dot_general(cm, bm, (((1,), (1,)), ((), ())), preferred_element_type=F32)
        ht = st_ref[gi]
        y_off = jnp.dot(cm, ht.astype(BF16), preferred_element_type=F32)
        xw = []
        dec = []
        ys = []
        for k in range(SSD_HPG // 2):
            h0 = gi * SSD_HPG + 2 * k
            col = gi * GROUP_W + k * LANES
            ms = []
            for h in (h0, h0 + 1):
                seg = cs_b[:, h * LANES:(h + 1) * LANES] - cs_r[h:h + 1, :]
                lmat = jnp.exp(jnp.where(mask, seg, -1e30))
                ms.append((cb * lmat * dtr[h:h + 1, :]).astype(BF16))
            lhs = jnp.concatenate(ms, axis=1)
            xp = xs_ref[:, col:col + LANES]
            rhs = jnp.concatenate([jnp.where(lo_q, xp, 0.0).astype(BF16),
                                   jnp.where(lo_q, 0.0, xp).astype(BF16)], axis=0)
            y_diag = jnp.dot(lhs, rhs, preferred_element_type=F32)
            cs_p = jnp.where(lo_q, cs_b[:, h0 * LANES:(h0 + 1) * LANES], cs_b[:, (h0 + 1) * LANES:(h0 + 2) * LANES])
            tot_p = _pair(lo_1, tot_c, h0)
            dt_p = dt_b[:, h0 * SSD_HEAD_DIM:(h0 + 2) * SSD_HEAD_DIM]
            y_k = y_diag + y_off[:, k * LANES:(k + 1) * LANES] * jnp.exp(cs_p)
            if post:
                cs = slice(col, col + LANES)
                ys.append((y_k + yo_ref[:, cs] + dcol_ref[:, cs] * xp) * _silu(z_ref[:, cs]))
            else:
                y_ref[:, col:col + LANES] = y_k
            xw.append((xp * (jnp.exp(tot_p - cs_p) * dt_p)).astype(BF16))
            dec.append(jnp.exp(tot_p))
        xw = jnp.concatenate(xw, axis=1)
        s_t = lax.dot_general(bm, xw, (((0,), (0,)), ((), ())), preferred_element_type=F32)
        st_ref[gi] = ht * jnp.concatenate(dec, axis=1) + s_t
        if post:
            ssq = sum(jnp.sum(v * v, axis=-1, keepdims=True) for v in ys)
            inv = lax.rsqrt(ssq * (1.0 / GROUP_W) + EPS)
            for k, v in enumerate(ys):
                cs = slice(gi * GROUP_W + k * LANES, gi * GROUP_W + (k + 1) * LANES)
                y_ref[:, cs] = (v * inv * nw_ref[:, cs]).astype(y_ref.dtype)

    @pl.when(step == pl.num_programs(2) - 1)
    def _():
        hfin_ref[...] = st_ref[...]


def _ssd_scan(xs, bc, dt, a_neg, tri, h0, post=None):
    bsz, length, _ = xs.shape
    nc = length // CHUNK
    g, q, n, nh = SSD_GROUPS, CHUNK, SSD_STATE, SSD_HEADS
    dt4 = dt[:, :, :2 * nh].reshape(bsz, length, 2, nh)
    dt_col = jnp.transpose(dt4, (2, 0, 1, 3))
    dt_row = jnp.transpose(dt4, (2, 0, 3, 1))
    a_row = a_neg.reshape(2, 1, nh)
    a_col = a_neg.reshape(2, nh, 1)

    def head_select(width):
        rows = lax.broadcasted_iota(jnp.int32, (3 * nh, nh * width), 0) % nh
        cols = lax.broadcasted_iota(jnp.int32, (3 * nh, nh * width), 1) // width
        return (rows == cols).astype(BF16)

    sel128, sel64 = head_select(LANES), head_select(SSD_HEAD_DIM)

    def call(d0, nd, extra, y_dtype):
        def dd(d):
            return d + d0

        def chunk(d, c):
            return c + dd(d) * (nc - 1 - 2 * c)

        extra_specs = []
        if extra:
            extra_specs = [pl.BlockSpec((None, q, SSD_INNER), lambda d, b, c: (b, chunk(d, c), 0)),
                           pl.BlockSpec((None, q, SSD_INNER), lambda d, b, c: (b, chunk(d, c), 0)),
                           pl.BlockSpec((1, SSD_INNER), lambda d, b, c: (0, 0)),
                           pl.BlockSpec((1, SSD_INNER), lambda d, b, c: (0, 0))]
        return pl.pallas_call(
            functools.partial(_ssd_kernel, bool(extra)),
            grid=(nd, bsz, nc),
            in_specs=[
                pl.BlockSpec((None, q, SSD_INNER), lambda d, b, c: (b, chunk(d, c), 0)),
                pl.BlockSpec((None, q, 2 * g * n), lambda d, b, c: (b, chunk(d, c), 0)),
                pl.BlockSpec((None, None, q, nh), lambda d, b, c: (dd(d), b, chunk(d, c), 0)),
                pl.BlockSpec((None, None, nh, q), lambda d, b, c: (dd(d), b, 0, chunk(d, c))),
                pl.BlockSpec((None, 1, nh), lambda d, b, c: (dd(d), 0, 0)),
                pl.BlockSpec((None, nh, 1), lambda d, b, c: (dd(d), 0, 0)),
                pl.BlockSpec((None, q, q), lambda d, b, c: (dd(d), 0, 0)),
                pl.BlockSpec((None, q, q), lambda d, b, c: (1 - dd(d), 0, 0)),
                pl.BlockSpec(sel128.shape, lambda d, b, c: (0, 0)),
                pl.BlockSpec(sel64.shape, lambda d, b, c: (0, 0)),
                pl.BlockSpec((None, None, g, n, GROUP_W), lambda d, b, c: (dd(d), b, 0, 0, 0)),
            ] + extra_specs,
            out_specs=[
                pl.BlockSpec((None, None, q, SSD_INNER), lambda d, b, c: (d, b, chunk(d, c), 0)),
                pl.BlockSpec((None, None, g, n, GROUP_W), lambda d, b, c: (d, b, 0, 0, 0)),
            ],
            out_shape=[jax.ShapeDtypeStruct((nd, bsz, length, SSD_INNER), y_dtype),
                       jax.ShapeDtypeStruct((nd, bsz, g, n, GROUP_W), F32)],
            scratch_shapes=[pltpu.VMEM((g, n, GROUP_W), F32)],
            compiler_params=_params(3),
            name="ssd_scan",
        )(xs, bc, dt_col, dt_row, a_row, a_col, tri, tri, sel128, sel64, h0, *extra)

    if post is None:
        return call(0, 2, (), F32)
    y_fwd, _ = call(0, 1, (), F32)
    y_out, _ = call(1, 1, (y_fwd[0],) + tuple(post), BF16)
    return y_out[0]


def _mix_out_kernel(m_ref, wo_ref, x_ref, gx_ref, scf_ref, shf_ref, npost_ref, npre_ref, wr_ref,
                    x1_ref, h2_ref, aff_ref):
    half = m_ref.shape[0] // 2
    for p in range(2):
        rs = slice(p * half, (p + 1) * half)
        out = jnp.dot(m_ref[rs, :], wo_ref[...], preferred_element_type=F32)
        x1 = x_ref[rs, :] + gx_ref[...] * (_rms(out) * npost_ref[...])
        x1_ref[rs, :] = x1
        h2 = (_rms(x1) * npre_ref[...]) * (1.0 + scf_ref[...]) + shf_ref[...]
        h2_ref[rs, :] = h2
        logits = jnp.dot(h2.astype(BF16), wr_ref[...], preferred_element_type=F32)
        lane = lax.broadcasted_iota(jnp.int32, logits.shape, 1)
        logits = jnp.where(lane < N_EXPERTS, logits, -1e30)
        e = jnp.exp(logits - jnp.max(logits, axis=-1, keepdims=True))
        aff_ref[rs, :] = e / jnp.sum(e, axis=-1, keepdims=True)


def _mix_out(merged, w_o, x2d, mod3, npost, npre, w_router_pad, length, tm):
    m, d = merged.shape
    nt = length // tm
    row = lambda kind: (lambda i: ((i // nt) * 6 + kind, 0, 0))
    return pl.pallas_call(
        _mix_out_kernel,
        grid=(m // tm,),
        in_specs=[pl.BlockSpec((tm, d), lambda i: (i, 0)),
                  pl.BlockSpec((d, d), lambda i: (0, 0), pipeline_mode=pl.Buffered(1)),
                  pl.BlockSpec((tm, d), lambda i: (i, 0)),
                  pl.BlockSpec((None, 1, d), row(2)),
                  pl.BlockSpec((None, 1, d), row(4)),
                  pl.BlockSpec((None, 1, d), row(3)),
                  pl.BlockSpec((1, d), lambda i: (0, 0)),
                  pl.BlockSpec((1, d), lambda i: (0, 0)),
                  pl.BlockSpec((d, LANES), lambda i: (0, 0))],
        out_specs=[pl.BlockSpec((tm, d), lambda i: (i, 0)),
                   pl.BlockSpec((tm, d), lambda i: (i, 0)),
                   pl.BlockSpec((tm, LANES), lambda i: (i, 0))],
        out_shape=[jax.ShapeDtypeStruct((m, d), F32),
                   jax.ShapeDtypeStruct((m, d), F32),
                   jax.ShapeDtypeStruct((m, LANES), F32)],
        compiler_params=_params(1),
        name="mix_out_router",
    )(merged, w_o, x2d, mod3, mod3, mod3, npost, npre, w_router_pad)


def _route_select(cap, aff_ref, triu_ref, low_ref, pos_ref, slot_ref, ts_ref):
    aff = aff_ref[...]
    e_cnt, n_tok = aff.shape

    def count(m):
        return jnp.sum(jnp.where(m, 1.0, 0.0), axis=1, keepdims=True)

    def search_bits(_, carry):
        lo, hi = carry
        mid = lo + ((hi - lo + 1) >> 1)
        ok = count(aff >= pltpu.bitcast(mid, F32)) >= cap
        return jnp.where(ok, mid, lo), jnp.where(ok, hi, mid - 1)

    lo0 = jnp.zeros((e_cnt, 1), jnp.int32)
    hi0 = jnp.full((e_cnt, 1), 0x7F800000, jnp.int32)
    lo_b, _ = lax.fori_loop(0, 32, search_bits, (lo0, hi0))

    def search_mid(_, carry):
        lo, hi = carry
        mid = 0.5 * (lo + hi)
        ok = count(aff >= mid) >= cap
        return jnp.where(ok, mid, lo), jnp.where(ok, hi, mid)

    lo_f, _ = lax.fori_loop(0, 24, search_mid, (pltpu.bitcast(lo_b, F32), pltpu.bitcast(lo_b + 1, F32)))
    thr = jnp.min(jnp.where(aff >= lo_f, aff, jnp.inf), axis=1, keepdims=True)
    need_f = cap - count(aff > thr)
    triu = triu_ref[...]
    carry_eq = jnp.zeros((e_cnt, 1), F32)
    carry_sel = jnp.zeros((e_cnt, 1), F32)
    for k in range(n_tok // LANES):
        sl = slice(k * LANES, (k + 1) * LANES)
        aff_k = aff[:, sl]
        eq_f = jnp.where(aff_k == thr, 1.0, 0.0)
        eq_incl = jnp.dot(eq_f.astype(BF16), triu, preferred_element_type=F32) + carry_eq
        sel_k = (aff_k > thr) | ((aff_k == thr) & (eq_incl <= need_f))
        sel_f = jnp.where(sel_k, 1.0, 0.0)
        sel_b = sel_f.astype(BF16)
        sel_incl = jnp.dot(sel_b, triu, preferred_element_type=F32) + carry_sel
        excl = sel_incl - sel_f
        pos_ref[:, sl] = jnp.where(sel_k, excl.astype(jnp.int32), -1)
        tok_start = jnp.sum(excl, axis=0, keepdims=True)
        ts_ref[:, sl] = tok_start
        slot_ref[:, sl] = tok_start + jnp.dot(low_ref[...], sel_b, preferred_element_type=F32)
        carry_eq = eq_incl[:, LANES - 1:LANES]
        carry_sel = sel_incl[:, LANES - 1:LANES]


def _route_kernel(cap, aff_ref, triu_ref, low_ref, list_ref, ts_ref, pos_ref, slot_ref):
    e = pl.program_id(1)

    @pl.when(e == 0)
    def _():
        _route_select(cap, aff_ref, triu_ref, low_ref, pos_ref, slot_ref, ts_ref)

    n_tok = aff_ref.shape[1]
    pos = pos_ref[pl.ds(e, 1), :]
    slot = slot_ref[pl.ds(e, 1), :]
    aff = aff_ref[pl.ds(e, 1), :]
    hit = lax.broadcasted_iota(jnp.int32, (cap, n_tok), 0) == pos
    tok = lax.broadcasted_iota(jnp.int32, (1, n_tok), 1)
    slot_hi = jnp.floor(slot * (1.0 / LANES))
    a_hi = aff.astype(BF16).astype(F32)
    a_mid = (aff - a_hi).astype(BF16).astype(F32)
    rows = [(tok >> 6).astype(F32), (tok & 63).astype(F32), slot_hi, slot - slot_hi * LANES,
            a_hi, a_mid, aff - a_hi - a_mid]
    sub = lax.broadcasted_iota(jnp.int32, (8, n_tok), 0)
    vals = jnp.zeros((8, n_tok), F32)
    for i, r in enumerate(rows):
        vals = jnp.where(sub == i, r, vals)
    list_ref[...] = lax.dot_general(jnp.where(hit, 1.0, 0.0).astype(BF16), vals.astype(BF16),
                                    (((1,), (1,)), ((), ())), preferred_element_type=F32)


def _route(aff_t, triu, low, cap):
    bsz, e_cnt, n_tok = aff_t.shape
    return pl.pallas_call(
        functools.partial(_route_kernel, cap),
        grid=(bsz, e_cnt),
        in_specs=[pl.BlockSpec((None, e_cnt, n_tok), lambda b, e: (b, 0, 0)),
                  pl.BlockSpec((LANES, LANES), lambda b, e: (0, 0)),
                  pl.BlockSpec((e_cnt, e_cnt), lambda b, e: (0, 0))],
        out_specs=[pl.BlockSpec((None, None, cap, 8), lambda b, e: (b, e, 0, 0)),
                   pl.BlockSpec((None, 1, n_tok), lambda b, e: (b, 0, 0))],
        out_shape=[jax.ShapeDtypeStruct((bsz, e_cnt, cap, 8), F32),
                   jax.ShapeDtypeStruct((bsz, 1, n_tok), F32)],
        scratch_shapes=[pltpu.VMEM((e_cnt, n_tok), jnp.int32), pltpu.VMEM((e_cnt, n_tok), F32)],
        compiler_params=_params(2),
        name="route_topk",
    )(aff_t, triu, low)


def _expert_kernel(cap, n_e, n_f, n_pairs, tf, idx_ref, dst_ref, h_hbm, w1_hbm, w3_hbm, w2_hbm, val_ref, z_hbm,
                   gbuf, obuf, xg_ref, acc_ref, w1_buf, w3_buf, w2_buf, gsem, ssem, wsem):
    e, f, j = pl.program_id(1), pl.program_id(2), pl.program_id(3)
    pair = pl.program_id(0) * n_e + e
    nxt = jnp.minimum(pair + 1, n_pairs - 1)
    d = h_hbm.shape[1]
    rows = cap // n_f
    s = 2 * f + j
    part = lax.rem(s, n_f)
    wslot = lax.rem(pair * n_f + f, 2)

    def weight_copies(ee, ff, slot):
        col = pl.multiple_of(ff * tf, tf)
        return (pltpu.make_async_copy(w1_hbm.at[ee, :, pl.ds(col, tf)], w1_buf.at[slot], wsem.at[slot, 0]),
                pltpu.make_async_copy(w3_hbm.at[ee, :, pl.ds(col, tf)], w3_buf.at[slot], wsem.at[slot, 1]),
                pltpu.make_async_copy(w2_hbm.at[ee, pl.ds(col, tf), :], w2_buf.at[slot], wsem.at[slot, 2]))

    def gather(p, t, q, r):
        src = idx_ref[(2 * p + t) * cap + q * rows + r]
        return pltpu.make_async_copy(h_hbm.at[pl.ds(src, 1), :], gbuf.at[t, q, pl.ds(r, 1), :], gsem.at[0])

    def scatter(pos, t, q, r):
        dst = dst_ref[(2 * pos + t) * cap + q * rows + r]
        return pltpu.make_async_copy(obuf.at[t, q, pl.ds(r, 1), :], z_hbm.at[pl.ds(dst, 1), :], ssem.at[t])

    def for_all_rows(fn):
        def body(r, carry):
            for t in range(2):
                for q in range(n_f):
                    fn(t, q, r)
            return carry

        lax.fori_loop(0, rows, body, 0)

    @pl.when(jnp.logical_and(pair == 0, s == 0))
    def _():
        for c in weight_copies(e, f, wslot):
            c.start()
        obuf[...] = jnp.zeros_like(obuf)
        for_all_rows(lambda t, q, r: gather(0, t, q, r).start())

    @pl.when(j == 0)
    def _():
        last_f = f == n_f - 1
        e_nxt = jnp.where(last_f, jnp.where(e == n_e - 1, 0, e + 1), e)
        f_nxt = jnp.where(last_f, 0, f + 1)

        @pl.when(jnp.logical_not(jnp.logical_and(pair == n_pairs - 1, last_f)))
        def _():
            for c in weight_copies(e_nxt, f_nxt, 1 - wslot):
                c.start()

        for c in weight_copies(e, f, wslot):
            c.wait()

    @pl.when(s == 0)
    def _():
        pltpu.make_async_copy(gbuf, gbuf, gsem.at[0]).wait()
        for t in range(2):
            for q in range(n_f):
                xg_ref[t, q * rows:(q + 1) * rows, :] = gbuf[t, q].astype(BF16)

    @pl.when(f == 0)
    def _():
        acc_ref[j] = jnp.zeros((cap, d), F32)

    @pl.when(s < n_f)
    def _():
        for t in range(2):
            for r in range(rows):
                scatter(pair, t, part, r).start(priority=r % 2)

    @pl.when(s >= n_f)
    def _():
        for t in range(2):
            for r in range(rows):
                gather(nxt, t, part, r).start(priority=r % 2)

    xg = xg_ref[j]
    hs = []
    for c in range(tf // MXU_COLS):
        sl = slice(c * MXU_COLS, (c + 1) * MXU_COLS)
        g = jnp.dot(xg, w1_buf[wslot, :, sl].astype(BF16), preferred_element_type=F32)
        u = jnp.dot(xg, w3_buf[wslot, :, sl].astype(BF16), preferred_element_type=F32)
        hs.append((_silu(g) * u).astype(BF16))
    h = jnp.concatenate(hs, axis=1)
    for c in range(d // MXU_COLS):
        sl = slice(c * MXU_COLS, (c + 1) * MXU_COLS)
        acc_ref[j, :, sl] += jnp.dot(h, w2_buf[wslot, :, sl].astype(BF16), preferred_element_type=F32)

    @pl.when(f == n_f - 1)
    def _():
        pltpu.make_async_copy(obuf.at[j], obuf.at[j], ssem.at[j]).wait()
        for q in range(n_f):
            sl = slice(q * rows, (q + 1) * rows)
            obuf[j, q] = acc_ref[j, sl, :] * val_ref[sl, :]

        @pl.when(jnp.logical_and(pair == n_pairs - 1, j == 1))
        def _():
            for_all_rows(lambda t, q, r: scatter(n_pairs, t, q, r).start())
            for t in range(2):
                pltpu.make_async_copy(obuf.at[t], obuf.at[t], ssem.at[t]).wait()
            pltpu.make_async_copy(gbuf, gbuf, gsem.at[0]).wait()


def _experts(idx, dst_ext, h2, w1, w3, w2, vals, n_rows_out, tf):
    bsz, e_cnt, cap, _ = vals.shape
    d = h2.shape[-1]
    n_f = w1.shape[-1] // tf
    n_pairs = bsz // 2 * e_cnt
    grid_spec = pltpu.PrefetchScalarGridSpec(
        num_scalar_prefetch=2,
        grid=(bsz // 2, e_cnt, n_f, 2),
        in_specs=[pl.BlockSpec(memory_space=pl.ANY),
                  pl.BlockSpec(memory_space=pl.ANY),
                  pl.BlockSpec(memory_space=pl.ANY),
                  pl.BlockSpec(memory_space=pl.ANY),
                  pl.BlockSpec((None, None, cap, 1), lambda bh, e, f, j, *_: (2 * bh + j, e, 0, 0))],
        out_specs=pl.BlockSpec(memory_space=pl.ANY),
        scratch_shapes=[pltpu.VMEM((2, n_f, cap // n_f, d), F32), pltpu.VMEM((2, n_f, cap // n_f, d), F32),
                        pltpu.VMEM((2, cap, d), BF16), pltpu.VMEM((2, cap, d), F32),
                        pltpu.VMEM((2, d, tf), F32), pltpu.VMEM((2, d, tf), F32), pltpu.VMEM((2, tf, d), F32),
                        pltpu.SemaphoreType.DMA((1,)), pltpu.SemaphoreType.DMA((2,)),
                        pltpu.SemaphoreType.DMA((2, 3))],
    )
    return pl.pallas_call(
        functools.partial(_expert_kernel, cap, e_cnt, n_f, n_pairs, tf),
        grid_spec=grid_spec,
        out_shape=jax.ShapeDtypeStruct((n_rows_out, d), F32),
        compiler_params=_params(4),
        name="moe_experts",
    )(idx.reshape(-1), dst_ext.reshape(-1), h2, w1, w3, w2, vals)


COMBINE_ROWS = 256


COMBINE_SLOTS = 4


def _combine_kernel(n_slot, nb, nt, starts_ref, ts_ref, x1_ref, g_ref, w_ref, z_hbm, o_ref, zbuf, acc_ref, sem,
                    st_ref):
    b, t = pl.program_id(0), pl.program_id(1)
    shift = COMBINE_ROWS.bit_length() - 1

    def tile_chunks(bb, tt):
        s0 = starts_ref[bb * (nt + 1) + tt]
        s1 = starts_ref[bb * (nt + 1) + tt + 1]
        start = lax.shift_left(lax.shift_right_logical(s0, 3), 3)
        return start, jnp.maximum(lax.shift_right_logical(s1 - start + COMBINE_ROWS - 1, shift), 1)

    def copy(row, slot):
        return pltpu.make_async_copy(z_hbm.at[pl.ds(row, COMBINE_ROWS), :], zbuf.at[slot], sem.at[slot])

    def fetch_one():
        tile = st_ref[2]

        @pl.when(tile < nb * nt)
        def _():
            bb = lax.div(tile, nt)
            start, n = tile_chunks(bb, tile - bb * nt)
            c, k = st_ref[3], st_ref[1]
            copy(pl.multiple_of(bb * n_slot + start + c * COMBINE_ROWS, 8), lax.rem(k, COMBINE_SLOTS)).start()
            st_ref[1] = k + 1
            tile_done = c + 1 == n
            st_ref[2] = jnp.where(tile_done, tile + 1, tile)
            st_ref[3] = jnp.where(tile_done, 0, c + 1)

    @pl.when(jnp.logical_and(b == 0, t == 0))
    def _():
        for i in range(4):
            st_ref[i] = 0
        for _ in range(COMBINE_SLOTS - 1):
            fetch_one()

    start0, n_chunks = tile_chunks(b, t)
    acc_ref[...] = jnp.zeros_like(acc_ref)
    first = ts_ref[:, 0:1]
    last = ts_ref[:, 1:2]

    def body(c, carry):
        k = st_ref[0]
        slot = lax.rem(k, COMBINE_SLOTS)
        copy(0, slot).wait()
        st_ref[0] = k + 1
        fetch_one()
        rows = zbuf[slot]
        j = (start0 + c * COMBINE_ROWS + lax.broadcasted_iota(jnp.int32, (1, COMBINE_ROWS), 1)).astype(F32)
        seg = jnp.where(j >= first, jnp.where(j < last, 1.0, 0.0), 0.0).astype(BF16)
        acc_ref[...] += jnp.dot(seg, rows.astype(BF16), preferred_element_type=F32)
        return carry

    lax.fori_loop(0, n_chunks, body, 0)
    o_ref[...] = x1_ref[...] + g_ref[...] * (_rms(acc_ref[...]) * w_ref[...])


def _combine(starts, ts2, z, x1, mod3, npost_ffn, n_slot, tm):
    bsz, n_tok, d = x1.shape
    grid_spec = pltpu.PrefetchScalarGridSpec(
        num_scalar_prefetch=1,
        grid=(bsz, n_tok // tm),
        in_specs=[pl.BlockSpec((None, tm, 2), lambda b, t, *_: (b, t, 0)),
                  pl.BlockSpec((None, tm, d), lambda b, t, *_: (b, t, 0)),
                  pl.BlockSpec((None, 1, d), lambda b, t, *_: (b * 6 + 5, 0, 0)),
                  pl.BlockSpec((1, d), lambda b, t, *_: (0, 0)),
                  pl.BlockSpec(memory_space=pl.ANY)],
        out_specs=pl.BlockSpec((None, tm, d), lambda b, t, *_: (b, t, 0)),
        scratch_shapes=[pltpu.VMEM((COMBINE_SLOTS, COMBINE_ROWS, d), F32), pltpu.VMEM((tm, d), F32),
                        pltpu.SemaphoreType.DMA((COMBINE_SLOTS,)), pltpu.SMEM((4,), jnp.int32)],
    )
    return pl.pallas_call(
        functools.partial(_combine_kernel, n_slot, bsz, n_tok // tm),
        grid_spec=grid_spec,
        out_shape=jax.ShapeDtypeStruct((bsz, n_tok, d), F32),
        compiler_params=_params(2),
        name="moe_combine",
    )(starts, ts2, x1, mod3, npost_ffn, z)


def _ssd_inputs(h, w_xs, w_bc, w_dt, conv_w, conv_b, dt_bias_pad, seg, tm):
    cw_x, cw_bc = conv_w[:, :SSD_INNER], conv_w[:, SSD_INNER:]
    cb_x, cb_bc = conv_b[:, :SSD_INNER], conv_b[:, SSD_INNER:]
    epi = functools.partial(_epi_conv_silu, seg)
    (xs,) = _fused_matmul([h], [w_xs], [0], [cw_x, cb_x], [], epi, [F32], SSD_INNER, tm, 1024, "proj_xs")
    (bc,) = _fused_matmul([h], [w_bc], [0], [cw_bc, cb_bc], [], epi, [BF16],
                          2 * SSD_GROUPS * SSD_STATE, tm, 1024, "proj_bc")
    (dt,) = _fused_matmul([h], [w_dt], [0], [dt_bias_pad], [], _epi_softplus, [F32], LANES, tm, LANES,
                          "proj_dt")
    return xs, bc, dt


def kernel(x, c, ctx, c_ctx, w_mod, b_mod, norm_pre_mix, norm_post_mix, w_in, ssd_conv_w, ssd_conv_b,
           dt_bias, a_log, ssd_d, ssd_norm_w, w_ssd_out, sc_conv_w, w_sc_out, b_gate, w_o, norm_pre_ffn,
           norm_post_ffn, w_router, w_e1, w_e3, w_e2):
    assert w_mod.shape[0] == 1, "single-layer block"
    bsz, length, d = x.shape
    ctx_len = ctx.shape[1]
    m = bsz * length
    cap = CAPACITY_FACTOR * length // N_EXPERTS
    row = lambda v: v.reshape(1, -1)

    o_z, o_xbc, o_dt, o_scb, o_scc, o_scv = (SSD_INNER, SSD_INNER + SSD_XBC,
                                             SSD_INNER + SSD_XBC + 2 * SSD_HEADS,
                                             SSD_INNER + SSD_XBC + 2 * SSD_HEADS + SC_DIM,
                                             SSD_INNER + SSD_XBC + 2 * SSD_HEADS + 2 * SC_DIM,
                                             SSD_INNER + SSD_XBC + 2 * SSD_HEADS + 3 * SC_DIM)
    w_in0 = w_in[0].astype(BF16)
    w_z, w_xs, w_bc, w_dt = (w_in0, 0), (w_in0, o_z), (w_in0, o_z + SSD_INNER), (w_in0, o_xbc)
    w_rest = w_in0[:, o_dt:]
    w_scb, w_scc, w_scv, w_gate = ((w_rest, 0), (w_rest, o_scb - o_dt), (w_rest, o_scc - o_dt),
                                   (w_rest, o_scv - o_dt))
    dt_bias_pad = jnp.pad(dt_bias[0].reshape(1, -1), ((0, 0), (0, LANES - 2 * SSD_HEADS)))
    a_neg = -jnp.exp(a_log[0])
    ii = lax.broadcasted_iota(jnp.int32, (CHUNK, CHUNK), 0)
    jj = lax.broadcasted_iota(jnp.int32, (CHUNK, CHUNK), 1)
    tri = jnp.stack([ii >= jj, ii <= jj]).astype(BF16)
    conv_w, conv_b = ssd_conv_w[0], row(ssd_conv_b[0])

    cs = jnp.concatenate([c, c_ctx[None, :], jnp.zeros((8 - bsz - 1, d), F32)], axis=0)
    mod3 = _modulation(cs, w_mod[0], b_mod[0]).reshape(8 * 6, 1, d)

    hc = _norm_mod(ctx, norm_pre_mix[0], mod3, lambda b: bsz * 6, lambda b: bsz * 6 + 1, ctx_len)
    xs_c, bc_c, dt_c = _ssd_inputs(hc.reshape(bsz * ctx_len, d), w_xs, w_bc, w_dt, conv_w, conv_b,
                                   dt_bias_pad, ctx_len, ctx_len)
    h_zero = jnp.zeros((2, bsz, SSD_GROUPS, SSD_STATE, GROUP_W), F32)
    _, states = _ssd_scan(xs_c.reshape(bsz, ctx_len, -1), bc_c.reshape(bsz, ctx_len, -1),
                          dt_c.reshape(bsz, ctx_len, -1), a_neg, tri, h_zero)

    tm = 1024
    hx = _norm_mod(x, norm_pre_mix[0], mod3, lambda b: b * 6, lambda b: b * 6 + 1, tm).reshape(m, d)
    (z,) = _fused_matmul([hx], [w_z], [0], [], [], _epi_identity, [F32], SSD_INNER, tm, 1024, "proj_z")
    xs, bc, dt = _ssd_inputs(hx, w_xs, w_bc, w_dt, conv_w, conv_b, dt_bias_pad, GRID_W, tm)
    (sc_y,) = _fused_matmul([hx], [w_scb, w_scc, w_scv], [0, 0, 0], [sc_conv_w[0]], [],
                            functools.partial(_epi_shortconv, GRID_W), [BF16], SC_DIM, tm, 512, "proj_sc")
    (gates,) = _fused_matmul([hx], [w_gate], [0], [row(b_gate[0])], [], _epi_sigmoid, [F32],
                             2 * d, tm, 1024, "proj_gate")
    d_cols = jnp.repeat(ssd_d[0], SSD_HEAD_DIM).reshape(1, -1)
    y_norm = _ssd_scan(xs.reshape(bsz, length, -1), bc.reshape(bsz, length, -1), dt.reshape(bsz, length, -1),
                       a_neg, tri, states, post=(z.reshape(bsz, length, -1), d_cols, row(ssd_norm_w[0])))
    y_norm = y_norm.reshape(m, -1)
    tn = 512
    (merged,) = _fused_matmul([y_norm, sc_y], [(w_ssd_out[0].astype(BF16), 0), (w_sc_out[0].astype(BF16), 0)],
                              [0, 1], [], [(gates, 0), (gates, d // tn)], _epi_gate_merge, [BF16], d, tm, tn,
                              "mix_merge")
    w_router_pad = jnp.pad(w_router[0], ((0, 0), (0, LANES - N_EXPERTS))).astype(BF16)
    x1, h2, aff = _mix_out(merged, w_o[0].astype(BF16), x.reshape(m, d), mod3, row(norm_post_mix[0]),
                           row(norm_pre_ffn[0]), w_router_pad, length, 512)

    aff_t = jnp.transpose(aff[:, :N_EXPERTS].reshape(bsz, length, N_EXPERTS), (0, 2, 1))
    e_i = lax.broadcasted_iota(jnp.int32, (N_EXPERTS, N_EXPERTS), 0)
    e_j = lax.broadcasted_iota(jnp.int32, (N_EXPERTS, N_EXPERTS), 1)
    lists, ts = _route(aff_t, tri[1], (e_j < e_i).astype(BF16), cap)
    n_slot = N_EXPERTS * cap
    li = lists[..., :4].astype(jnp.int32)
    b_i = jnp.arange(bsz, dtype=jnp.int32)[:, None, None]
    def pair_order(a):
        return jnp.transpose(a.reshape(bsz // 2, 2, N_EXPERTS, cap), (0, 2, 1, 3)).reshape(-1, cap)

    idx = pair_order(li[..., 0] * 64 + li[..., 1] + b_i * length)
    dst = pair_order(li[..., 2] * LANES + li[..., 3] + b_i * n_slot)
    n_pad = 2 * cap
    dst_ext = jnp.concatenate([bsz * n_slot + jnp.arange(n_pad, dtype=jnp.int32).reshape(2, cap), dst], axis=0)
    vals = (lists[..., 4] + lists[..., 5] + lists[..., 6])[..., None]
    z = _experts(idx, dst_ext, h2, w_e1[0], w_e3[0], w_e2[0], vals, bsz * n_slot + n_pad, MXU_COLS)
    tm_c = 256
    ts = ts.reshape(bsz, length)
    end = jnp.full((bsz, 1), n_slot, F32)
    ts2 = jnp.stack([ts, jnp.concatenate([ts[:, 1:], end], axis=1)], axis=-1)
    starts = jnp.concatenate([ts[:, ::tm_c], end], axis=1).astype(jnp.int32).reshape(-1)
    return _combine(starts, ts2, z, x1.reshape(bsz, length, d), mod3, row(norm_post_ffn[0]), n_slot, tm_c)
```

```python
import functools
import math

import jax
import jax.numpy as jnp
from jax import lax
from jax.experimental import pallas as pl
from jax.experimental.pallas import tpu as pltpu

F32 = jnp.float32
BF16 = jnp.bfloat16

D_MODEL = 2048
GRID_W = 64
SSD_HEADS = 32
SSD_HEAD_DIM = 64
SSD_INNER = SSD_HEADS * SSD_HEAD_DIM
SSD_GROUPS = 4
SSD_HPG = SSD_HEADS // SSD_GROUPS
SSD_STATE = 128
CHUNK = 128
SSD_XBC = SSD_INNER + 2 * SSD_GROUPS * SSD_STATE
SC_DIM = 2048
N_EXPERTS = 16
EXPERT_FF = 2048
CAPACITY_FACTOR = 2
EPS = 1e-6
GROUP_W = SSD_HPG * SSD_HEAD_DIM

LANES = 128
MXU_COLS = 256
V7X_VMEM_BYTES = 64 * 1024 * 1024
VMEM_LIMIT = V7X_VMEM_BYTES - 8 * 1024 * 1024


def _params(n_grid):
    return pltpu.CompilerParams(dimension_semantics=("arbitrary",) * n_grid,
                                vmem_limit_bytes=VMEM_LIMIT)


def _silu(x):
    return x * (1.0 / (1.0 + jnp.exp(-x)))


def _sigmoid(x):
    return 1.0 / (1.0 + jnp.exp(-x))


def _softplus(x):
    return jnp.maximum(x, 0.0) + jnp.log(1.0 + jnp.exp(-jnp.abs(x)))


def _rms(x):
    return x * lax.rsqrt(jnp.mean(x * x, axis=-1, keepdims=True) + EPS)


def _mod_kernel(c_ref, w_ref, b_ref, o_ref):
    a = _silu(c_ref[...]).astype(BF16)
    o_ref[...] = jnp.dot(a, w_ref[...].astype(BF16), preferred_element_type=F32) + b_ref[...]


def _modulation(cs, w_mod, b_mod):
    rows, d = cs.shape
    n = w_mod.shape[1]
    tn = 1024
    return pl.pallas_call(
        _mod_kernel,
        grid=(n // tn,),
        in_specs=[pl.BlockSpec((rows, d), lambda j: (0, 0)),
                  pl.BlockSpec((d, tn), lambda j: (0, j)),
                  pl.BlockSpec((1, tn), lambda j: (0, j))],
        out_specs=pl.BlockSpec((rows, tn), lambda j: (0, j)),
        out_shape=jax.ShapeDtypeStruct((rows, n), F32),
        compiler_params=_params(1),
        name="modulation",
    )(cs, w_mod, b_mod.reshape(1, n))


def _norm_mod_kernel(x_ref, w_ref, sc_ref, sh_ref, o_ref):
    y = _rms(x_ref[...]) * w_ref[...]
    o_ref[...] = (y * (1.0 + sc_ref[...]) + sh_ref[...]).astype(o_ref.dtype)


def _norm_mod(x, w, mod3, sh_row, sc_row, tm):
    bsz, length, d = x.shape
    return pl.pallas_call(
        _norm_mod_kernel,
        grid=(bsz, length // tm),
        in_specs=[pl.BlockSpec((None, tm, d), lambda b, i: (b, i, 0)),
                  pl.BlockSpec((1, d), lambda b, i: (0, 0)),
                  pl.BlockSpec((None, 1, d), lambda b, i: (sc_row(b), 0, 0)),
                  pl.BlockSpec((None, 1, d), lambda b, i: (sh_row(b), 0, 0))],
        out_specs=pl.BlockSpec((None, tm, d), lambda b, i: (b, i, 0)),
        out_shape=jax.ShapeDtypeStruct((bsz, length, d), BF16),
        compiler_params=_params(2),
        name="norm_mod",
    )(x, w.reshape(1, d), mod3, mod3)


def _fused_matmul(a_list, w_list, w_to_a, col_list, tile_list, epilogue, out_dtypes, n, tm, tn, name):
    m, k = a_list[0].shape
    na, nw, nc, nt = len(a_list), len(w_list), len(col_list), len(tile_list)
    sub = min(tn, 2 * MXU_COLS)

    def body(*refs):
        a_refs = refs[:na]
        w_refs = refs[na:na + nw]
        c_refs = refs[na + nw:na + nw + nc]
        t_refs = refs[na + nw + nc:na + nw + nc + nt]
        o_refs = refs[na + nw + nc + nt:]
        for s in range(tn // sub):
            sl = slice(s * sub, (s + 1) * sub)
            prods = [jnp.dot(a_refs[w_to_a[i]][...], w_refs[i][:, sl], preferred_element_type=F32)
                     for i in range(nw)]
            outs = epilogue(prods, [c[:, sl] for c in c_refs], [t[:, sl] for t in t_refs])
            for o_ref, o in zip(o_refs, outs):
                o_ref[:, sl] = o.astype(o_ref.dtype)

    in_specs = [pl.BlockSpec((tm, k), lambda i, j: (i, 0)) for _ in a_list]
    args = list(a_list)
    for w, off in w_list:
        assert off % LANES == 0, "weight sections must start on a lane-tile boundary"
        in_specs.append(pl.BlockSpec((pl.Element(k), pl.Element(tn)),
                                     functools.partial(lambda i, j, off: (0, pl.multiple_of(off + j * tn, LANES)),
                                                       off=off)))
        args.append(w)
    for c in col_list:
        in_specs.append(pl.BlockSpec((c.shape[0], tn), lambda i, j: (0, j)))
        args.append(c)
    for t, off in tile_list:
        in_specs.append(pl.BlockSpec((tm, tn), functools.partial(lambda i, j, off: (i, j + off), off=off)))
        args.append(t)
    return pl.pallas_call(
        body,
        grid=(m // tm, n // tn),
        in_specs=in_specs,
        out_specs=[pl.BlockSpec((tm, tn), lambda i, j: (i, j)) for _ in out_dtypes],
        out_shape=[jax.ShapeDtypeStruct((m, n), dt) for dt in out_dtypes],
        compiler_params=_params(2),
        name=name,
    )(*args)


def _conv3_rows(x, cw, seg):
    tm = x.shape[0]
    r = lax.broadcasted_iota(jnp.int32, x.shape, 0) & (seg - 1)
    xp = jnp.where(r == 0, 0.0, pltpu.roll(x, 1, 0))
    xn = jnp.where(r == seg - 1, 0.0, pltpu.roll(x, tm - 1, 0))
    return xp * cw[0:1, :] + x * cw[1:2, :] + xn * cw[2:3, :]


def _epi_identity(prods, cols, tiles):
    return [prods[0]]


def _epi_conv_silu(seg, prods, cols, tiles):
    return [_silu(_conv3_rows(prods[0], cols[0], seg) + cols[1])]


def _epi_softplus(prods, cols, tiles):
    return [_softplus(prods[0] + cols[0])]


def _epi_shortconv(seg, prods, cols, tiles):
    sc_b, sc_c, sc_v = prods
    return [sc_b * _conv3_rows(sc_c * sc_v, cols[0], seg)]


def _epi_sigmoid(prods, cols, tiles):
    return [_sigmoid(prods[0] + cols[0])]


def _epi_gate_merge(prods, cols, tiles):
    return [tiles[0] * prods[0] + tiles[1] * prods[1]]


def _split3(x):
    hi = x.astype(BF16)
    r1 = x - hi.astype(F32)
    mid = r1.astype(BF16)
    lo = (r1 - mid.astype(F32)).astype(BF16)
    return hi, mid, lo


def _cumsum_rows(tri, x):
    return sum(jnp.dot(tri, p, preferred_element_type=F32) for p in _split3(x))


def _cumsum_cols(x, tri):
    return sum(jnp.dot(p, tri, preferred_element_type=F32) for p in _split3(x))


def _pair(lo_mask, v, h):
    return jnp.where(lo_mask, v[:, h:h + 1], v[:, h + 1:h + 2])


def _spread(x, sel):
    return jnp.dot(jnp.concatenate(_split3(x), axis=1), sel, preferred_element_type=F32)


def _ssd_kernel(post, xs_ref, bc_ref, dtc_ref, dtr_ref, arow_ref, acol_ref, tri_ref, trit_ref, sel128_ref,
                sel64_ref, h0_ref, *rest):
    if post:
        yo_ref, z_ref, dcol_ref, nw_ref, y_ref, hfin_ref, st_ref = rest
    else:
        y_ref, hfin_ref, st_ref = rest
    q, n = CHUNK, SSD_STATE
    step = pl.program_id(2)

    @pl.when(step == 0)
    def _():
        st_ref[...] = h0_ref[...]

    tri = tri_ref[...]
    dtc = dtc_ref[...]
    dtr = dtr_ref[...]
    al_c = dtc * arow_ref[...]
    al_r = dtr * acol_ref[...]
    cs_c = _cumsum_rows(tri, al_c)
    cs_r = _cumsum_cols(al_r, trit_ref[...])
    tot_c = jnp.sum(al_c, axis=0, keepdims=True)
    cs_b = _spread(cs_c, sel128_ref[...])
    dt_b = _spread(dtc, sel64_ref[...])
    mask = tri.astype(F32) > 0.5
    lo_q = lax.broadcasted_iota(jnp.int32, (q, LANES), 1) < SSD_HEAD_DIM
    lo_1 = lax.broadcasted_iota(jnp.int32, (1, LANES), 1) < SSD_HEAD_DIM
    for gi in range(SSD_GROUPS):
        bm = bc_ref[:, gi * n:(gi + 1) * n]
        cm = bc_ref[:, (SSD_GROUPS + gi) * n:(SSD_GROUPS + gi + 1) * n]
        cb = lax.dot_general(cm, bm, (((1,), (1,)), ((), ())), preferred_element_type=F32)
        ht = st_ref[gi]
        y_off = jnp.dot(cm, ht.astype(BF16), preferred_element_type=F32)
        xw = []
        dec = []
        ys = []
        for k in range(SSD_HPG // 2):
            h0 = gi * SSD_HPG + 2 * k
            col = gi * GROUP_W + k * LANES
            ms = []
            for h in (h0, h0 + 1):
                seg = cs_b[:, h * LANES:(h + 1) * LANES] - cs_r[h:h + 1, :]
                lmat = jnp.exp(jnp.where(mask, seg, -1e30))
                ms.append((cb * lmat * dtr[h:h + 1, :]).astype(BF16))
            lhs = jnp.concatenate(ms, axis=1)
            xp = xs_ref[:, col:col + LANES]
            rhs = jnp.concatenate([jnp.where(lo_q, xp, 0.0).astype(BF16),
                                   jnp.where(lo_q, 0.0, xp).astype(BF16)], axis=0)
            y_diag = jnp.dot(lhs, rhs, preferred_element_type=F32)
            cs_p = jnp.where(lo_q, cs_b[:, h0 * LANES:(h0 + 1) * LANES], cs_b[:, (h0 + 1) * LANES:(h0 + 2) * LANES])
            tot_p = _pair(lo_1, tot_c, h0)
            dt_p = dt_b[:, h0 * SSD_HEAD_DIM:(h0 + 2) * SSD_HEAD_DIM]
            y_k = y_diag + y_off[:, k * LANES:(k + 1) * LANES] * jnp.exp(cs_p)
            if post:
                cs = slice(col, col + LANES)
                ys.append((y_k + yo_ref[:, cs] + dcol_ref[:, cs] * xp) * _silu(z_ref[:, cs]))
            else:
                y_ref[:, col:col + LANES] = y_k
            xw.append((xp * (jnp.exp(tot_p - cs_p) * dt_p)).astype(BF16))
            dec.append(jnp.exp(tot_p))
        xw = jnp.concatenate(xw, axis=1)
        s_t = lax.dot_general(bm, xw, (((0,), (0,)), ((), ())), preferred_element_type=F32)
        st_ref[gi] = ht * jnp.concatenate(dec, axis=1) + s_t
        if post:
            ssq = sum(jnp.sum(v * v, axis=-1, keepdims=True) for v in ys)
            inv = lax.rsqrt(ssq * (1.0 / GROUP_W) + EPS)
            for k, v in enumerate(ys):
                cs = slice(gi * GROUP_W + k * LANES, gi * GROUP_W + (k + 1) * LANES)
                y_ref[:, cs] = (v * inv * nw_ref[:, cs]).astype(y_ref.dtype)

    @pl.when(step == pl.num_programs(2) - 1)
    def _():
        hfin_ref[...] = st_ref[...]


def _ssd_scan(xs, bc, dt, a_neg, tri, h0, post=None):
    bsz, length, _ = xs.shape
    nc = length // CHUNK
    g, q, n, nh = SSD_GROUPS, CHUNK, SSD_STATE, SSD_HEADS
    dt4 = dt[:, :, :2 * nh].reshape(bsz, length, 2, nh)
    dt_col = jnp.transpose(dt4, (2, 0, 1, 3))
    dt_row = jnp.transpose(dt4, (2, 0, 3, 1))
    a_row = a_neg.reshape(2, 1, nh)
    a_col = a_neg.reshape(2, nh, 1)

    def head_select(width):
        rows = lax.broadcasted_iota(jnp.int32, (3 * nh, nh * width), 0) % nh
        cols = lax.broadcasted_iota(jnp.int32, (3 * nh, nh * width), 1) // width
        return (rows == cols).astype(BF16)

    sel128, sel64 = head_select(LANES), head_select(SSD_HEAD_DIM)

    def call(d0, nd, extra, y_dtype):
        def dd(d):
            return d + d0

        def chunk(d, c):
            return c + dd(d) * (nc - 1 - 2 * c)

        extra_specs = []
        if extra:
            extra_specs = [pl.BlockSpec((None, q, SSD_INNER), lambda d, b, c: (b, chunk(d, c), 0)),
                           pl.BlockSpec((None, q, SSD_INNER), lambda d, b, c: (b, chunk(d, c), 0)),
                           pl.BlockSpec((1, SSD_INNER), lambda d, b, c: (0, 0)),
                           pl.BlockSpec((1, SSD_INNER), lambda d, b, c: (0, 0))]
        return pl.pallas_call(
            functools.partial(_ssd_kernel, bool(extra)),
            grid=(nd, bsz, nc),
            in_specs=[
                pl.BlockSpec((None, q, SSD_INNER), lambda d, b, c: (b, chunk(d, c), 0)),
                pl.BlockSpec((None, q, 2 * g * n), lambda d, b, c: (b, chunk(d, c), 0)),
                pl.BlockSpec((None, None, q, nh), lambda d, b, c: (dd(d), b, chunk(d, c), 0)),
                pl.BlockSpec((None, None, nh, q), lambda d, b, c: (dd(d), b, 0, chunk(d, c))),
                pl.BlockSpec((None, 1, nh), lambda d, b, c: (dd(d), 0, 0)),
                pl.BlockSpec((None, nh, 1), lambda d, b, c: (dd(d), 0, 0)),
                pl.BlockSpec((None, q, q), lambda d, b, c: (dd(d), 0, 0)),
                pl.BlockSpec((None, q, q), lambda d, b, c: (1 - dd(d), 0, 0)),
                pl.BlockSpec(sel128.shape, lambda d, b, c: (0, 0)),
                pl.BlockSpec(sel64.shape, lambda d, b, c: (0, 0)),
                pl.BlockSpec((None, None, g, n, GROUP_W), lambda d, b, c: (dd(d), b, 0, 0, 0)),
            ] + extra_specs,
            out_specs=[
                pl.BlockSpec((None, None, q, SSD_INNER), lambda d, b, c: (d, b, chunk(d, c), 0)),
                pl.BlockSpec((None, None, g, n, GROUP_W), lambda d, b, c: (d, b, 0, 0, 0)),
            ],
            out_shape=[jax.ShapeDtypeStruct((nd, bsz, length, SSD_INNER), y_dtype),
                       jax.ShapeDtypeStruct((nd, bsz, g, n, GROUP_W), F32)],
            scratch_shapes=[pltpu.VMEM((g, n, GROUP_W), F32)],
            compiler_params=_params(3),
            name="ssd_scan",
        )(xs, bc, dt_col, dt_row, a_row, a_col, tri, tri, sel128, sel64, h0, *extra)

    if post is None:
        return call(0, 2, (), F32)
    y_fwd, _ = call(0, 1, (), F32)
    y_out, _ = call(1, 1, (y_fwd[0],) + tuple(post), BF16)
    return y_out[0]


def _mix_out_kernel(m_ref, wo_ref, x_ref, gx_ref, scf_ref, shf_ref, npost_ref, npre_ref, wr_ref,
                    x1_ref, h2_ref, aff_ref):
    half = m_ref.shape[0] // 2
    for p in range(2):
        rs = slice(p * half, (p + 1) * half)
        out = jnp.dot(m_ref[rs, :], wo_ref[...], preferred_element_type=F32)
        x1 = x_ref[rs, :] + gx_ref[...] * (_rms(out) * npost_ref[...])
        x1_ref[rs, :] = x1
        h2 = (_rms(x1) * npre_ref[...]) * (1.0 + scf_ref[...]) + shf_ref[...]
        h2_ref[rs, :] = h2
        logits = jnp.dot(h2.astype(BF16), wr_ref[...], preferred_element_type=F32)
        lane = lax.broadcasted_iota(jnp.int32, logits.shape, 1)
        logits = jnp.where(lane < N_EXPERTS, logits, -1e30)
        e = jnp.exp(logits - jnp.max(logits, axis=-1, keepdims=True))
        aff_ref[rs, :] = e / jnp.sum(e, axis=-1, keepdims=True)


def _mix_out(merged, w_o, x2d, mod3, npost, npre, w_router_pad, length, tm):
    m, d = merged.shape
    nt = length // tm
    row = lambda kind: (lambda i: ((i // nt) * 6 + kind, 0, 0))
    return pl.pallas_call(
        _mix_out_kernel,
        grid=(m // tm,),
        in_specs=[pl.BlockSpec((tm, d), lambda i: (i, 0)),
                  pl.BlockSpec((d, d), lambda i: (0, 0), pipeline_mode=pl.Buffered(1)),
                  pl.BlockSpec((tm, d), lambda i: (i, 0)),
                  pl.BlockSpec((None, 1, d), row(2)),
                  pl.BlockSpec((None, 1, d), row(4)),
                  pl.BlockSpec((None, 1, d), row(3)),
                  pl.BlockSpec((1, d), lambda i: (0, 0)),
                  pl.BlockSpec((1, d), lambda i: (0, 0)),
                  pl.BlockSpec((d, LANES), lambda i: (0, 0))],
        out_specs=[pl.BlockSpec((tm, d), lambda i: (i, 0)),
                   pl.BlockSpec((tm, d), lambda i: (i, 0)),
                   pl.BlockSpec((tm, LANES), lambda i: (i, 0))],
        out_shape=[jax.ShapeDtypeStruct((m, d), F32),
                   jax.ShapeDtypeStruct((m, d), F32),
                   jax.ShapeDtypeStruct((m, LANES), F32)],
        compiler_params=_params(1),
        name="mix_out_router",
    )(merged, w_o, x2d, mod3, mod3, mod3, npost, npre, w_router_pad)


def _route_select(cap, aff_ref, triu_ref, low_ref, pos_ref, slot_ref, ts_ref):
    aff = aff_ref[...]
    e_cnt, n_tok = aff.shape

    def count(m):
        return jnp.sum(jnp.where(m, 1.0, 0.0), axis=1, keepdims=True)

    def search_bits(_, carry):
        lo, hi = carry
        mid = lo + ((hi - lo + 1) >> 1)
        ok = count(aff >= pltpu.bitcast(mid, F32)) >= cap
        return jnp.where(ok, mid, lo), jnp.where(ok, hi, mid - 1)

    lo0 = jnp.zeros((e_cnt, 1), jnp.int32)
    hi0 = jnp.full((e_cnt, 1), 0x7F800000, jnp.int32)
    lo_b, _ = lax.fori_loop(0, 32, search_bits, (lo0, hi0))

    def search_mid(_, carry):
        lo, hi = carry
        mid = 0.5 * (lo + hi)
        ok = count(aff >= mid) >= cap
        return jnp.where(ok, mid, lo), jnp.where(ok, hi, mid)

    lo_f, _ = lax.fori_loop(0, 24, search_mid, (pltpu.bitcast(lo_b, F32), pltpu.bitcast(lo_b + 1, F32)))
    thr = jnp.min(jnp.where(aff >= lo_f, aff, jnp.inf), axis=1, keepdims=True)
    need_f = cap - count(aff > thr)
    triu = triu_ref[...]
    carry_eq = jnp.zeros((e_cnt, 1), F32)
    carry_sel = jnp.zeros((e_cnt, 1), F32)
    for k in range(n_tok // LANES):
        sl = slice(k * LANES, (k + 1) * LANES)
        aff_k = aff[:, sl]
        eq_f = jnp.where(aff_k == thr, 1.0, 0.0)
        eq_incl = jnp.dot(eq_f.astype(BF16), triu, preferred_element_type=F32) + carry_eq
        sel_k = (aff_k > thr) | ((aff_k == thr) & (eq_incl <= need_f))
        sel_f = jnp.where(sel_k, 1.0, 0.0)
        sel_b = sel_f.astype(BF16)
        sel_incl = jnp.dot(sel_b, triu, preferred_element_type=F32) + carry_sel
        excl = sel_incl - sel_f
        pos_ref[:, sl] = jnp.where(sel_k, excl.astype(jnp.int32), -1)
        tok_start = jnp.sum(excl, axis=0, keepdims=True)
        ts_ref[:, sl] = tok_start
        slot_ref[:, sl] = tok_start + jnp.dot(low_ref[...], sel_b, preferred_element_type=F32)
        carry_eq = eq_incl[:, LANES - 1:LANES]
        carry_sel = sel_incl[:, LANES - 1:LANES]


def _route_kernel(cap, aff_ref, triu_ref, low_ref, list_ref, ts_ref, pos_ref, slot_ref):
    e = pl.program_id(1)

    @pl.when(e == 0)
    def _():
        _route_select(cap, aff_ref, triu_ref, low_ref, pos_ref, slot_ref, ts_ref)

    n_tok = aff_ref.shape[1]
    pos = pos_ref[pl.ds(e, 1), :]
    slot = slot_ref[pl.ds(e, 1), :]
    aff = aff_ref[pl.ds(e, 1), :]
    hit = lax.broadcasted_iota(jnp.int32, (cap, n_tok), 0) == pos
    tok = lax.broadcasted_iota(jnp.int32, (1, n_tok), 1)
    slot_hi = jnp.floor(slot * (1.0 / LANES))
    a_hi = aff.astype(BF16).astype(F32)
    a_mid = (aff - a_hi).astype(BF16).astype(F32)
    rows = [(tok >> 6).astype(F32), (tok & 63).astype(F32), slot_hi, slot - slot_hi * LANES,
            a_hi, a_mid, aff - a_hi - a_mid]
    sub = lax.broadcasted_iota(jnp.int32, (8, n_tok), 0)
    vals = jnp.zeros((8, n_tok), F32)
    for i, r in enumerate(rows):
        vals = jnp.where(sub == i, r, vals)
    lists = lax.dot_general(jnp.where(hit, 1.0, 0.0).astype(BF16), vals.astype(BF16),
                            (((1,), (1,)), ((), ())), preferred_element_type=F32)
    eye = jnp.where(lax.broadcasted_iota(jnp.int32, (8, 8), 0) == lax.broadcasted_iota(jnp.int32, (8, 8), 1),
                    1.0, 0.0).astype(BF16)
    list_ref[...] = lax.dot_general(eye, lists.astype(BF16), (((1,), (1,)), ((), ())),
                                    preferred_element_type=F32)


def _route(aff_t, triu, low, cap):
    bsz, e_cnt, n_tok = aff_t.shape
    return pl.pallas_call(
        functools.partial(_route_kernel, cap),
        grid=(bsz, e_cnt),
        in_specs=[pl.BlockSpec((None, e_cnt, n_tok), lambda b, e: (b, 0, 0)),
                  pl.BlockSpec((LANES, LANES), lambda b, e: (0, 0)),
                  pl.BlockSpec((e_cnt, e_cnt), lambda b, e: (0, 0))],
        out_specs=[pl.BlockSpec((None, None, 8, cap), lambda b, e: (b, e, 0, 0)),
                   pl.BlockSpec((None, 1, n_tok), lambda b, e: (b, 0, 0))],
        out_shape=[jax.ShapeDtypeStruct((bsz, e_cnt, 8, cap), F32),
                   jax.ShapeDtypeStruct((bsz, 1, n_tok), F32)],
        scratch_shapes=[pltpu.VMEM((e_cnt, n_tok), jnp.int32), pltpu.VMEM((e_cnt, n_tok), F32)],
        compiler_params=_params(2),
        name="route_topk",
    )(aff_t, triu, low)


def _expert_kernel(cap, n_e, n_f, n_pairs, tf, idx_ref, dst_ref, h_hbm, w1_hbm, w3_hbm, w2_hbm, val_ref, z_hbm,
                   gbuf, obuf, xg_ref, acc_ref, w1_buf, w3_buf, w2_buf, gsem, ssem, wsem):
    e, f, j = pl.program_id(1), pl.program_id(2), pl.program_id(3)
    pair = pl.program_id(0) * n_e + e
    nxt = jnp.minimum(pair + 1, n_pairs - 1)
    d = h_hbm.shape[1]
    rows = cap // n_f
    s = 2 * f + j
    part = lax.rem(s, n_f)
    wslot = lax.rem(pair * n_f + f, 2)

    def weight_copies(ee, ff, slot):
        col = pl.multiple_of(ff * tf, tf)
        return (pltpu.make_async_copy(w1_hbm.at[ee, :, pl.ds(col, tf)], w1_buf.at[slot], wsem.at[slot, 0]),
                pltpu.make_async_copy(w3_hbm.at[ee, :, pl.ds(col, tf)], w3_buf.at[slot], wsem.at[slot, 1]),
                pltpu.make_async_copy(w2_hbm.at[ee, pl.ds(col, tf), :], w2_buf.at[slot], wsem.at[slot, 2]))

    def gather(p, t, q, r):
        src = idx_ref[(2 * p + t) * cap + q * rows + r]
        return pltpu.make_async_copy(h_hbm.at[pl.ds(src, 1), :], gbuf.at[t, q, pl.ds(r, 1), :], gsem.at[0])

    def scatter(pos, t, q, r):
        dst = dst_ref[(2 * pos + t) * cap + q * rows + r]
        return pltpu.make_async_copy(obuf.at[t, q, pl.ds(r, 1), :], z_hbm.at[pl.ds(dst, 1), :], ssem.at[t])

    def for_all_rows(fn):
        def body(r, carry):
            for t in range(2):
                for q in range(n_f):
                    fn(t, q, r)
            return carry

        lax.fori_loop(0, rows, body, 0)

    @pl.when(jnp.logical_and(pair == 0, s == 0))
    def _():
        for c in weight_copies(e, f, wslot):
            c.start()
        obuf[...] = jnp.zeros_like(obuf)
        for_all_rows(lambda t, q, r: gather(0, t, q, r).start())

    @pl.when(j == 0)
    def _():
        last_f = f == n_f - 1
        e_nxt = jnp.where(last_f, jnp.where(e == n_e - 1, 0, e + 1), e)
        f_nxt = jnp.where(last_f, 0, f + 1)

        @pl.when(jnp.logical_not(jnp.logical_and(pair == n_pairs - 1, last_f)))
        def _():
            for c in weight_copies(e_nxt, f_nxt, 1 - wslot):
                c.start()

        for c in weight_copies(e, f, wslot):
            c.wait()

    @pl.when(s == 0)
    def _():
        pltpu.make_async_copy(gbuf, gbuf, gsem.at[0]).wait()
        for t in range(2):
            for q in range(n_f):
                xg_ref[t, q * rows:(q + 1) * rows, :] = gbuf[t, q].astype(BF16)

    @pl.when(f == 0)
    def _():
        acc_ref[j] = jnp.zeros((cap, d), F32)

    @pl.when(s < n_f)
    def _():
        for t in range(2):
            for r in range(rows):
                scatter(pair, t, part, r).start(priority=r % 2)

    @pl.when(s >= n_f)
    def _():
        for t in range(2):
            for r in range(rows):
                gather(nxt, t, part, r).start(priority=r % 2)

    xg = xg_ref[j]
    hs = []
    for c in range(tf // MXU_COLS):
        sl = slice(c * MXU_COLS, (c + 1) * MXU_COLS)
        g = jnp.dot(xg, w1_buf[wslot, :, sl].astype(BF16), preferred_element_type=F32)
        u = jnp.dot(xg, w3_buf[wslot, :, sl].astype(BF16), preferred_element_type=F32)
        hs.append((_silu(g) * u).astype(BF16))
    h = jnp.concatenate(hs, axis=1)
    for c in range(d // MXU_COLS):
        sl = slice(c * MXU_COLS, (c + 1) * MXU_COLS)
        acc_ref[j, :, sl] += jnp.dot(h, w2_buf[wslot, :, sl].astype(BF16), preferred_element_type=F32)

    @pl.when(f == n_f - 1)
    def _():
        pltpu.make_async_copy(obuf.at[j], obuf.at[j], ssem.at[j]).wait()
        for q in range(n_f):
            sl = slice(q * rows, (q + 1) * rows)
            obuf[j, q] = acc_ref[j, sl, :] * val_ref[sl, :]

        @pl.when(jnp.logical_and(pair == n_pairs - 1, j == 1))
        def _():
            for_all_rows(lambda t, q, r: scatter(n_pairs, t, q, r).start())
            for t in range(2):
                pltpu.make_async_copy(obuf.at[t], obuf.at[t], ssem.at[t]).wait()
            pltpu.make_async_copy(gbuf, gbuf, gsem.at[0]).wait()


def _experts(idx, dst_ext, h2, w1, w3, w2, vals, n_rows_out, tf):
    bsz, e_cnt, cap, _ = vals.shape
    d = h2.shape[-1]
    n_f = w1.shape[-1] // tf
    n_pairs = bsz // 2 * e_cnt
    grid_spec = pltpu.PrefetchScalarGridSpec(
        num_scalar_prefetch=2,
        grid=(bsz // 2, e_cnt, n_f, 2),
        in_specs=[pl.BlockSpec(memory_space=pl.ANY),
                  pl.BlockSpec(memory_space=pl.ANY),
                  pl.BlockSpec(memory_space=pl.ANY),
                  pl.BlockSpec(memory_space=pl.ANY),
                  pl.BlockSpec((None, None, cap, 1), lambda bh, e, f, j, *_: (2 * bh + j, e, 0, 0))],
        out_specs=pl.BlockSpec(memory_space=pl.ANY),
        scratch_shapes=[pltpu.VMEM((2, n_f, cap // n_f, d), F32), pltpu.VMEM((2, n_f, cap // n_f, d), F32),
                        pltpu.VMEM((2, cap, d), BF16), pltpu.VMEM((2, cap, d), F32),
                        pltpu.VMEM((2, d, tf), F32), pltpu.VMEM((2, d, tf), F32), pltpu.VMEM((2, tf, d), F32),
                        pltpu.SemaphoreType.DMA((1,)), pltpu.SemaphoreType.DMA((2,)),
                        pltpu.SemaphoreType.DMA((2, 3))],
    )
    return pl.pallas_call(
        functools.partial(_expert_kernel, cap, e_cnt, n_f, n_pairs, tf),
        grid_spec=grid_spec,
        out_shape=jax.ShapeDtypeStruct((n_rows_out, d), F32),
        compiler_params=_params(4),
        name="moe_experts",
    )(idx.reshape(-1), dst_ext.reshape(-1), h2, w1, w3, w2, vals)


COMBINE_ROWS = 256


COMBINE_SLOTS = 4


def _combine_kernel(n_slot, nb, nt, starts_ref, ts_ref, x1_ref, g_ref, w_ref, z_hbm, o_ref, zbuf, acc_ref, sem,
                    st_ref):
    b, t = pl.program_id(0), pl.program_id(1)
    shift = COMBINE_ROWS.bit_length() - 1

    def tile_chunks(bb, tt):
        s0 = starts_ref[bb * (nt + 1) + tt]
        s1 = starts_ref[bb * (nt + 1) + tt + 1]
        start = lax.shift_left(lax.shift_right_logical(s0, 3), 3)
        return start, jnp.maximum(lax.shift_right_logical(s1 - start + COMBINE_ROWS - 1, shift), 1)

    def copy(row, slot):
        return pltpu.make_async_copy(z_hbm.at[pl.ds(row, COMBINE_ROWS), :], zbuf.at[slot], sem.at[slot])

    def fetch_one():
        tile = st_ref[2]

        @pl.when(tile < nb * nt)
        def _():
            bb = lax.div(tile, nt)
            start, n = tile_chunks(bb, tile - bb * nt)
            c, k = st_ref[3], st_ref[1]
            copy(pl.multiple_of(bb * n_slot + start + c * COMBINE_ROWS, 8), lax.rem(k, COMBINE_SLOTS)).start()
            st_ref[1] = k + 1
            tile_done = c + 1 == n
            st_ref[2] = jnp.where(tile_done, tile + 1, tile)
            st_ref[3] = jnp.where(tile_done, 0, c + 1)

    @pl.when(jnp.logical_and(b == 0, t == 0))
    def _():
        for i in range(4):
            st_ref[i] = 0
        for _ in range(COMBINE_SLOTS - 1):
            fetch_one()

    start0, n_chunks = tile_chunks(b, t)
    acc_ref[...] = jnp.zeros_like(acc_ref)
    first = ts_ref[:, 0:1]
    last = ts_ref[:, 1:2]

    def body(c, carry):
        k = st_ref[0]
        slot = lax.rem(k, COMBINE_SLOTS)
        copy(0, slot).wait()
        st_ref[0] = k + 1
        fetch_one()
        rows = zbuf[slot]
        j = (start0 + c * COMBINE_ROWS + lax.broadcasted_iota(jnp.int32, (1, COMBINE_ROWS), 1)).astype(F32)
        seg = jnp.where(j >= first, jnp.where(j < last, 1.0, 0.0), 0.0).astype(BF16)
        hi = rows.astype(BF16)
        lo = (rows - hi.astype(F32)).astype(BF16)
        acc_ref[...] += (jnp.dot(seg, hi, preferred_element_type=F32)
                         + jnp.dot(seg, lo, preferred_element_type=F32))
        return carry

    lax.fori_loop(0, n_chunks, body, 0)
    o_ref[...] = x1_ref[...] + g_ref[...] * (_rms(acc_ref[...]) * w_ref[...])


def _combine(starts, ts2, z, x1, mod3, npost_ffn, n_slot, tm):
    bsz, n_tok, d = x1.shape
    grid_spec = pltpu.PrefetchScalarGridSpec(
        num_scalar_prefetch=1,
        grid=(bsz, n_tok // tm),
        in_specs=[pl.BlockSpec((None, tm, 2), lambda b, t, *_: (b, t, 0)),
                  pl.BlockSpec((None, tm, d), lambda b, t, *_: (b, t, 0)),
                  pl.BlockSpec((None, 1, d), lambda b, t, *_: (b * 6 + 5, 0, 0)),
                  pl.BlockSpec((1, d), lambda b, t, *_: (0, 0)),
                  pl.BlockSpec(memory_space=pl.ANY)],
        out_specs=pl.BlockSpec((None, tm, d), lambda b, t, *_: (b, t, 0)),
        scratch_shapes=[pltpu.VMEM((COMBINE_SLOTS, COMBINE_ROWS, d), F32), pltpu.VMEM((tm, d), F32),
                        pltpu.SemaphoreType.DMA((COMBINE_SLOTS,)), pltpu.SMEM((4,), jnp.int32)],
    )
    return pl.pallas_call(
        functools.partial(_combine_kernel, n_slot, bsz, n_tok // tm),
        grid_spec=grid_spec,
        out_shape=jax.ShapeDtypeStruct((bsz, n_tok, d), F32),
        compiler_params=_params(2),
        name="moe_combine",
    )(starts, ts2, x1, mod3, npost_ffn, z)


def _ssd_inputs(h, w_xs, w_bc, w_dt, conv_w, conv_b, dt_bias_pad, seg, tm):
    cw_x, cw_bc = conv_w[:, :SSD_INNER], conv_w[:, SSD_INNER:]
    cb_x, cb_bc = conv_b[:, :SSD_INNER], conv_b[:, SSD_INNER:]
    epi = functools.partial(_epi_conv_silu, seg)
    (xs,) = _fused_matmul([h], [w_xs], [0], [cw_x, cb_x], [], epi, [F32], SSD_INNER, tm, 1024, "proj_xs")
    (bc,) = _fused_matmul([h], [w_bc], [0], [cw_bc, cb_bc], [], epi, [BF16],
                          2 * SSD_GROUPS * SSD_STATE, tm, 1024, "proj_bc")
    (dt,) = _fused_matmul([h], [w_dt], [0], [dt_bias_pad], [], _epi_softplus, [F32], LANES, tm, LANES,
                          "proj_dt")
    return xs, bc, dt


def kernel(x, c, ctx, c_ctx, w_mod, b_mod, norm_pre_mix, norm_post_mix, w_in, ssd_conv_w, ssd_conv_b,
           dt_bias, a_log, ssd_d, ssd_norm_w, w_ssd_out, sc_conv_w, w_sc_out, b_gate, w_o, norm_pre_ffn,
           norm_post_ffn, w_router, w_e1, w_e3, w_e2):
    assert w_mod.shape[0] == 1, "single-layer block"
    bsz, length, d = x.shape
    ctx_len = ctx.shape[1]
    m = bsz * length
    cap = CAPACITY_FACTOR * length // N_EXPERTS
    row = lambda v: v.reshape(1, -1)

    o_z, o_xbc, o_dt, o_scb, o_scc, o_scv = (SSD_INNER, SSD_INNER + SSD_XBC,
                                             SSD_INNER + SSD_XBC + 2 * SSD_HEADS,
                                             SSD_INNER + SSD_XBC + 2 * SSD_HEADS + SC_DIM,
                                             SSD_INNER + SSD_XBC + 2 * SSD_HEADS + 2 * SC_DIM,
                                             SSD_INNER + SSD_XBC + 2 * SSD_HEADS + 3 * SC_DIM)
    w_in0 = w_in[0].astype(BF16)
    w_z, w_xs, w_bc, w_dt = (w_in0, 0), (w_in0, o_z), (w_in0, o_z + SSD_INNER), (w_in0, o_xbc)
    w_rest = w_in0[:, o_dt:]
    w_scb, w_scc, w_scv, w_gate = ((w_rest, 0), (w_rest, o_scb - o_dt), (w_rest, o_scc - o_dt),
                                   (w_rest, o_scv - o_dt))
    dt_bias_pad = jnp.pad(dt_bias[0].reshape(1, -1), ((0, 0), (0, LANES - 2 * SSD_HEADS)))
    a_neg = -jnp.exp(a_log[0])
    ii = lax.broadcasted_iota(jnp.int32, (CHUNK, CHUNK), 0)
    jj = lax.broadcasted_iota(jnp.int32, (CHUNK, CHUNK), 1)
    tri = jnp.stack([ii >= jj, ii <= jj]).astype(BF16)
    conv_w, conv_b = ssd_conv_w[0], row(ssd_conv_b[0])

    cs = jnp.concatenate([c, c_ctx[None, :], jnp.zeros((8 - bsz - 1, d), F32)], axis=0)
    mod3 = _modulation(cs, w_mod[0], b_mod[0]).reshape(8 * 6, 1, d)

    hc = _norm_mod(ctx, norm_pre_mix[0], mod3, lambda b: bsz * 6, lambda b: bsz * 6 + 1, ctx_len)
    xs_c, bc_c, dt_c = _ssd_inputs(hc.reshape(bsz * ctx_len, d), w_xs, w_bc, w_dt, conv_w, conv_b,
                                   dt_bias_pad, ctx_len, ctx_len)
    h_zero = jnp.zeros((2, bsz, SSD_GROUPS, SSD_STATE, GROUP_W), F32)
    _, states = _ssd_scan(xs_c.reshape(bsz, ctx_len, -1), bc_c.reshape(bsz, ctx_len, -1),
                          dt_c.reshape(bsz, ctx_len, -1), a_neg, tri, h_zero)

    tm = 1024
    hx = _norm_mod(x, norm_pre_mix[0], mod3, lambda b: b * 6, lambda b: b * 6 + 1, tm).reshape(m, d)
    (z,) = _fused_matmul([hx], [w_z], [0], [], [], _epi_identity, [F32], SSD_INNER, tm, 1024, "proj_z")
    xs, bc, dt = _ssd_inputs(hx, w_xs, w_bc, w_dt, conv_w, conv_b, dt_bias_pad, GRID_W, tm)
    (sc_y,) = _fused_matmul([hx], [w_scb, w_scc, w_scv], [0, 0, 0], [sc_conv_w[0]], [],
                            functools.partial(_epi_shortconv, GRID_W), [BF16], SC_DIM, tm, 512, "proj_sc")
    (gates,) = _fused_matmul([hx], [w_gate], [0], [row(b_gate[0])], [], _epi_sigmoid, [F32],
                             2 * d, tm, 1024, "proj_gate")
    d_cols = jnp.repeat(ssd_d[0], SSD_HEAD_DIM).reshape(1, -1)
    y_norm = _ssd_scan(xs.reshape(bsz, length, -1), bc.reshape(bsz, length, -1), dt.reshape(bsz, length, -1),
                       a_neg, tri, states, post=(z.reshape(bsz, length, -1), d_cols, row(ssd_norm_w[0])))
    y_norm = y_norm.reshape(m, -1)
    tn = 512
    (merged,) = _fused_matmul([y_norm, sc_y], [(w_ssd_out[0].astype(BF16), 0), (w_sc_out[0].astype(BF16), 0)],
                              [0, 1], [], [(gates, 0), (gates, d // tn)], _epi_gate_merge, [BF16], d, tm, tn,
                              "mix_merge")
    w_router_pad = jnp.pad(w_router[0], ((0, 0), (0, LANES - N_EXPERTS))).astype(BF16)
    x1, h2, aff = _mix_out(merged, w_o[0].astype(BF16), x.reshape(m, d), mod3, row(norm_post_mix[0]),
                           row(norm_pre_ffn[0]), w_router_pad, length, 512)

    aff_t = jnp.transpose(aff[:, :N_EXPERTS].reshape(bsz, length, N_EXPERTS), (0, 2, 1))
    e_i = lax.broadcasted_iota(jnp.int32, (N_EXPERTS, N_EXPERTS), 0)
    e_j = lax.broadcasted_iota(jnp.int32, (N_EXPERTS, N_EXPERTS), 1)
    lists, ts = _route(aff_t, tri[1], (e_j < e_i).astype(BF16), cap)
    n_slot = N_EXPERTS * cap
    li = lists[:, :, :4, :].astype(jnp.int32)
    b_i = jnp.arange(bsz, dtype=jnp.int32)[:, None, None]

    def pair_order(a):
        return jnp.transpose(a.reshape(bsz // 2, 2, N_EXPERTS, cap), (0, 2, 1, 3)).reshape(-1, cap)

    idx = pair_order(li[:, :, 0] * 64 + li[:, :, 1] + b_i * length)
    dst = pair_order(li[:, :, 2] * LANES + li[:, :, 3] + b_i * n_slot)
    n_pad = 2 * cap
    dst_ext = jnp.concatenate([bsz * n_slot + jnp.arange(n_pad, dtype=jnp.int32).reshape(2, cap), dst], axis=0)
    vals = (lists[:, :, 4] + lists[:, :, 5] + lists[:, :, 6])[..., None]
    z = _experts(idx, dst_ext, h2, w_e1[0], w_e3[0], w_e2[0], vals, bsz * n_slot + n_pad, MXU_COLS)
    tm_c = 256
    ts = ts.reshape(bsz, length)
    end = jnp.full((bsz, 1), n_slot, F32)
    ts2 = jnp.stack([ts, jnp.concatenate([ts[:, 1:], end], axis=1)], axis=-1)
    starts = jnp.concatenate([ts[:, ::tm_c], end], axis=1).astype(jnp.int32).reshape(-1)
    return _combine(starts, ts2, z, x1.reshape(bsz, length, d), mod3, row(norm_post_ffn[0]), n_slot, tm_c)
```

```python
import functools
import math

import jax
import jax.numpy as jnp
from jax import lax
from jax.experimental import pallas as pl
from jax.experimental.pallas import tpu as pltpu

F32 = jnp.float32
BF16 = jnp.bfloat16

D_MODEL = 2048
GRID_W = 64
SSD_HEADS = 32
SSD_HEAD_DIM = 64
SSD_INNER = SSD_HEADS * SSD_HEAD_DIM
SSD_GROUPS = 4
SSD_HPG = SSD_HEADS // SSD_GROUPS
SSD_STATE = 128
CHUNK = 128
SSD_XBC = SSD_INNER + 2 * SSD_GROUPS * SSD_STATE
SC_DIM = 2048
N_EXPERTS = 16
EXPERT_FF = 2048
CAPACITY_FACTOR = 2
EPS = 1e-6
GROUP_W = SSD_HPG * SSD_HEAD_DIM

LANES = 128
MXU_COLS = 256
V7X_VMEM_BYTES = 64 * 1024 * 1024
VMEM_LIMIT = V7X_VMEM_BYTES - 8 * 1024 * 1024


def _params(n_grid):
    return pltpu.CompilerParams(dimension_semantics=("arbitrary",) * n_grid,
                                vmem_limit_bytes=VMEM_LIMIT)


def _silu(x):
    return x * (1.0 / (1.0 + jnp.exp(-x)))


def _sigmoid(x):
    return 1.0 / (1.0 + jnp.exp(-x))


def _softplus(x):
    return jnp.maximum(x, 0.0) + jnp.log(1.0 + jnp.exp(-jnp.abs(x)))


def _rms(x):
    return x * lax.rsqrt(jnp.mean(x * x, axis=-1, keepdims=True) + EPS)


def _mod_kernel(c_ref, w_ref, b_ref, o_ref):
    a = _silu(c_ref[...]).astype(BF16)
    o_ref[...] = jnp.dot(a, w_ref[...].astype(BF16), preferred_element_type=F32) + b_ref[...]


def _modulation(cs, w_mod, b_mod):
    rows, d = cs.shape
    n = w_mod.shape[1]
    tn = 1024
    return pl.pallas_call(
        _mod_kernel,
        grid=(n // tn,),
        in_specs=[pl.BlockSpec((rows, d), lambda j: (0, 0)),
                  pl.BlockSpec((d, tn), lambda j: (0, j)),
                  pl.BlockSpec((1, tn), lambda j: (0, j))],
        out_specs=pl.BlockSpec((rows, tn), lambda j: (0, j)),
        out_shape=jax.ShapeDtypeStruct((rows, n), F32),
        compiler_params=_params(1),
        name="modulation",
    )(cs, w_mod, b_mod.reshape(1, n))


def _norm_mod_kernel(x_ref, w_ref, sc_ref, sh_ref, o_ref):
    y = _rms(x_ref[...]) * w_ref[...]
    o_ref[...] = (y * (1.0 + sc_ref[...]) + sh_ref[...]).astype(o_ref.dtype)


def _norm_mod(x, w, mod3, sh_row, sc_row, tm):
    bsz, length, d = x.shape
    return pl.pallas_call(
        _norm_mod_kernel,
        grid=(bsz, length // tm),
        in_specs=[pl.BlockSpec((None, tm, d), lambda b, i: (b, i, 0)),
                  pl.BlockSpec((1, d), lambda b, i: (0, 0)),
                  pl.BlockSpec((None, 1, d), lambda b, i: (sc_row(b), 0, 0)),
                  pl.BlockSpec((None, 1, d), lambda b, i: (sh_row(b), 0, 0))],
        out_specs=pl.BlockSpec((None, tm, d), lambda b, i: (b, i, 0)),
        out_shape=jax.ShapeDtypeStruct((bsz, length, d), BF16),
        compiler_params=_params(2),
        name="norm_mod",
    )(x, w.reshape(1, d), mod3, mod3)


def _fused_matmul(a_list, w_list, w_to_a, col_list, tile_list, epilogue, out_dtypes, n, tm, tn, name):
    m, k = a_list[0].shape
    na, nw, nc, nt = len(a_list), len(w_list), len(col_list), len(tile_list)
    sub = min(tn, 2 * MXU_COLS)

    def body(*refs):
        a_refs = refs[:na]
        w_refs = refs[na:na + nw]
        c_refs = refs[na + nw:na + nw + nc]
        t_refs = refs[na + nw + nc:na + nw + nc + nt]
        o_refs = refs[na + nw + nc + nt:]
        for s in range(tn // sub):
            sl = slice(s * sub, (s + 1) * sub)
            prods = [jnp.dot(a_refs[w_to_a[i]][...], w_refs[i][:, sl], preferred_element_type=F32)
                     for i in range(nw)]
            outs = epilogue(prods, [c[:, sl] for c in c_refs], [t[:, sl] for t in t_refs])
            for o_ref, o in zip(o_refs, outs):
                o_ref[:, sl] = o.astype(o_ref.dtype)

    in_specs = [pl.BlockSpec((tm, k), lambda i, j: (i, 0)) for _ in a_list]
    args = list(a_list)
    for w, off in w_list:
        assert off % LANES == 0, "weight sections must start on a lane-tile boundary"
        in_specs.append(pl.BlockSpec((pl.Element(k), pl.Element(tn)),
                                     functools.partial(lambda i, j, off: (0, pl.multiple_of(off + j * tn, LANES)),
                                                       off=off)))
        args.append(w)
    for c in col_list:
        in_specs.append(pl.BlockSpec((c.shape[0], tn), lambda i, j: (0, j)))
        args.append(c)
    for t, off in tile_list:
        in_specs.append(pl.BlockSpec((tm, tn), functools.partial(lambda i, j, off: (i, j + off), off=off)))
        args.append(t)
    return pl.pallas_call(
        body,
        grid=(m // tm, n // tn),
        in_specs=in_specs,
        out_specs=[pl.BlockSpec((tm, tn), lambda i, j: (i, j)) for _ in out_dtypes],
        out_shape=[jax.ShapeDtypeStruct((m, n), dt) for dt in out_dtypes],
        compiler_params=_params(2),
        name=name,
    )(*args)


def _conv3_rows(x, cw, seg):
    tm = x.shape[0]
    r = lax.broadcasted_iota(jnp.int32, x.shape, 0) & (seg - 1)
    xp = jnp.where(r == 0, 0.0, pltpu.roll(x, 1, 0))
    xn = jnp.where(r == seg - 1, 0.0, pltpu.roll(x, tm - 1, 0))
    return xp * cw[0:1, :] + x * cw[1:2, :] + xn * cw[2:3, :]


def _epi_identity(prods, cols, tiles):
    return [prods[0]]


def _epi_conv_silu(seg, prods, cols, tiles):
    return [_silu(_conv3_rows(prods[0], cols[0], seg) + cols[1])]


def _epi_softplus(prods, cols, tiles):
    return [_softplus(prods[0] + cols[0])]


def _epi_shortconv(seg, prods, cols, tiles):
    sc_b, sc_c, sc_v = prods
    return [sc_b * _conv3_rows(sc_c * sc_v, cols[0], seg)]


def _epi_sigmoid(prods, cols, tiles):
    return [_sigmoid(prods[0] + cols[0])]


def _epi_gate_merge(prods, cols, tiles):
    return [tiles[0] * prods[0] + tiles[1] * prods[1]]


def _split3(x):
    hi = x.astype(BF16)
    r1 = x - hi.astype(F32)
    mid = r1.astype(BF16)
    lo = (r1 - mid.astype(F32)).astype(BF16)
    return hi, mid, lo


def _cumsum_rows(tri, x):
    return sum(jnp.dot(tri, p, preferred_element_type=F32) for p in _split3(x))


def _cumsum_cols(x, tri):
    return sum(jnp.dot(p, tri, preferred_element_type=F32) for p in _split3(x))


def _pair(lo_mask, v, h):
    return jnp.where(lo_mask, v[:, h:h + 1], v[:, h + 1:h + 2])


def _ssd_kernel(post, xs_ref, bc_ref, dtc_ref, dtr_ref, arow_ref, acol_ref, tri_ref, trit_ref, sel128_ref,
                sel64_ref, h0_ref, *rest):
    if post:
        yo_ref, z_ref, dcol_ref, nw_ref, y_ref, hfin_ref, st_ref = rest
    else:
        y_ref, hfin_ref, st_ref = rest
    q, n = CHUNK, SSD_STATE
    step = pl.program_id(2)

    @pl.when(step == 0)
    def _():
        st_ref[...] = h0_ref[...]

    tri = tri_ref[...]
    dtc = dtc_ref[...]
    dtr = dtr_ref[...]
    al_c = dtc * arow_ref[...]
    al_r = dtr * acol_ref[...]
    cs_c = _cumsum_rows(tri, al_c)
    cs_r = _cumsum_cols(al_r, trit_ref[...])
    tot_c = jnp.sum(al_c, axis=0, keepdims=True)
    cs_terms = jnp.concatenate(_split3(cs_c), axis=1)
    dt_terms = jnp.concatenate(_split3(dtc), axis=1)
    mask = tri.astype(F32) > 0.5
    lo_q = lax.broadcasted_iota(jnp.int32, (q, LANES), 1) < SSD_HEAD_DIM
    lo_1 = lax.broadcasted_iota(jnp.int32, (1, LANES), 1) < SSD_HEAD_DIM
    for gi in range(SSD_GROUPS):
        bm = bc_ref[:, gi * n:(gi + 1) * n]
        cm = bc_ref[:, (SSD_GROUPS + gi) * n:(SSD_GROUPS + gi + 1) * n]
        cb = lax.dot_general(cm, bm, (((1,), (1,)), ((), ())), preferred_element_type=F32)
        ht = st_ref[gi]
        y_off = jnp.dot(cm, ht.astype(BF16), preferred_element_type=F32)
        cs_b = jnp.dot(cs_terms, sel128_ref[:, gi * SSD_HPG * LANES:(gi + 1) * SSD_HPG * LANES],
                       preferred_element_type=F32)
        dt_b = jnp.dot(dt_terms, sel64_ref[:, gi * GROUP_W:(gi + 1) * GROUP_W], preferred_element_type=F32)
        xw = []
        dec = []
        ys = []
        for k in range(SSD_HPG // 2):
            h0 = gi * SSD_HPG + 2 * k
            col = gi * GROUP_W + k * LANES
            ms = []
            for h in (h0, h0 + 1):
                hl = h - gi * SSD_HPG
                seg = cs_b[:, hl * LANES:(hl + 1) * LANES] - cs_r[h:h + 1, :]
                lmat = jnp.exp(jnp.where(mask, seg, -1e30))
                ms.append((cb * lmat * dtr[h:h + 1, :]).astype(BF16))
            lhs = jnp.concatenate(ms, axis=1)
            xp = xs_ref[:, col:col + LANES]
            rhs = jnp.concatenate([jnp.where(lo_q, xp, 0.0).astype(BF16),
                                   jnp.where(lo_q, 0.0, xp).astype(BF16)], axis=0)
            y_diag = jnp.dot(lhs, rhs, preferred_element_type=F32)
            cs_p = jnp.where(lo_q, cs_b[:, 2 * k * LANES:(2 * k + 1) * LANES],
                             cs_b[:, (2 * k + 1) * LANES:(2 * k + 2) * LANES])
            tot_p = _pair(lo_1, tot_c, h0)
            dt_p = dt_b[:, k * LANES:(k + 1) * LANES]
            y_k = y_diag + y_off[:, k * LANES:(k + 1) * LANES] * jnp.exp(cs_p)
            if post:
                cs = slice(col, col + LANES)
                ys.append((y_k + yo_ref[:, cs] + dcol_ref[:, cs] * xp) * _silu(z_ref[:, cs]))
            else:
                y_ref[:, col:col + LANES] = y_k
            xw.append((xp * (jnp.exp(tot_p - cs_p) * dt_p)).astype(BF16))
            dec.append(jnp.exp(tot_p))
        xw = jnp.concatenate(xw, axis=1)
        s_t = lax.dot_general(bm, xw, (((0,), (0,)), ((), ())), preferred_element_type=F32)
        st_ref[gi] = ht * jnp.concatenate(dec, axis=1) + s_t
        if post:
            ssq = sum(jnp.sum(v * v, axis=-1, keepdims=True) for v in ys)
            inv = lax.rsqrt(ssq * (1.0 / GROUP_W) + EPS)
            for k, v in enumerate(ys):
                cs = slice(gi * GROUP_W + k * LANES, gi * GROUP_W + (k + 1) * LANES)
                y_ref[:, cs] = (v * inv * nw_ref[:, cs]).astype(y_ref.dtype)

    @pl.when(step == pl.num_programs(2) - 1)
    def _():
        hfin_ref[...] = st_ref[...]


def _ssd_scan(xs, bc, dt, a_neg, tri, h0, post=None):
    bsz, length, _ = xs.shape
    nc = length // CHUNK
    g, q, n, nh = SSD_GROUPS, CHUNK, SSD_STATE, SSD_HEADS
    dt4 = dt[:, :, :2 * nh].reshape(bsz, length, 2, nh)
    dt_col = jnp.transpose(dt4, (2, 0, 1, 3))
    dt_row = jnp.transpose(dt4, (2, 0, 3, 1))
    a_row = a_neg.reshape(2, 1, nh)
    a_col = a_neg.reshape(2, nh, 1)

    def head_select(width):
        rows = lax.broadcasted_iota(jnp.int32, (3 * nh, nh * width), 0) % nh
        cols = lax.broadcasted_iota(jnp.int32, (3 * nh, nh * width), 1) // width
        return (rows == cols).astype(BF16)

    sel128, sel64 = head_select(LANES), head_select(SSD_HEAD_DIM)

    def call(d0, nd, extra, y_dtype):
        def dd(d):
            return d + d0

        def chunk(d, c):
            return c + dd(d) * (nc - 1 - 2 * c)

        extra_specs = []
        if extra:
            extra_specs = [pl.BlockSpec((None, q, SSD_INNER), lambda d, b, c: (b, chunk(d, c), 0)),
                           pl.BlockSpec((None, q, SSD_INNER), lambda d, b, c: (b, chunk(d, c), 0)),
                           pl.BlockSpec((1, SSD_INNER), lambda d, b, c: (0, 0)),
                           pl.BlockSpec((1, SSD_INNER), lambda d, b, c: (0, 0))]
        return pl.pallas_call(
            functools.partial(_ssd_kernel, bool(extra)),
            grid=(nd, bsz, nc),
            in_specs=[
                pl.BlockSpec((None, q, SSD_INNER), lambda d, b, c: (b, chunk(d, c), 0)),
                pl.BlockSpec((None, q, 2 * g * n), lambda d, b, c: (b, chunk(d, c), 0)),
                pl.BlockSpec((None, None, q, nh), lambda d, b, c: (dd(d), b, chunk(d, c), 0)),
                pl.BlockSpec((None, None, nh, q), lambda d, b, c: (dd(d), b, 0, chunk(d, c))),
                pl.BlockSpec((None, 1, nh), lambda d, b, c: (dd(d), 0, 0)),
                pl.BlockSpec((None, nh, 1), lambda d, b, c: (dd(d), 0, 0)),
                pl.BlockSpec((None, q, q), lambda d, b, c: (dd(d), 0, 0)),
                pl.BlockSpec((None, q, q), lambda d, b, c: (1 - dd(d), 0, 0)),
                pl.BlockSpec(sel128.shape, lambda d, b, c: (0, 0)),
                pl.BlockSpec(sel64.shape, lambda d, b, c: (0, 0)),
                pl.BlockSpec((None, None, g, n, GROUP_W), lambda d, b, c: (dd(d), b, 0, 0, 0)),
            ] + extra_specs,
            out_specs=[
                pl.BlockSpec((None, None, q, SSD_INNER), lambda d, b, c: (d, b, chunk(d, c), 0)),
                pl.BlockSpec((None, None, g, n, GROUP_W), lambda d, b, c: (d, b, 0, 0, 0)),
            ],
            out_shape=[jax.ShapeDtypeStruct((nd, bsz, length, SSD_INNER), y_dtype),
                       jax.ShapeDtypeStruct((nd, bsz, g, n, GROUP_W), F32)],
            scratch_shapes=[pltpu.VMEM((g, n, GROUP_W), F32)],
            compiler_params=_params(3),
            name="ssd_scan",
        )(xs, bc, dt_col, dt_row, a_row, a_col, tri, tri, sel128, sel64, h0, *extra)

    if post is None:
        return call(0, 2, (), F32)
    y_fwd, _ = call(0, 1, (), F32)
    y_out, _ = call(1, 1, (y_fwd[0],) + tuple(post), BF16)
    return y_out[0]


def _mix_out_kernel(m_ref, wo_ref, x_ref, gx_ref, scf_ref, shf_ref, npost_ref, npre_ref, wr_ref,
                    x1_ref, h2_ref, aff_ref):
    half = m_ref.shape[0] // 2
    for p in range(2):
        rs = slice(p * half, (p + 1) * half)
        out = jnp.dot(m_ref[rs, :], wo_ref[...], preferred_element_type=F32)
        x1 = x_ref[rs, :] + gx_ref[...] * (_rms(out) * npost_ref[...])
        x1_ref[rs, :] = x1
        h2 = (_rms(x1) * npre_ref[...]) * (1.0 + scf_ref[...]) + shf_ref[...]
        h2_ref[rs, :] = h2
        logits = jnp.dot(h2.astype(BF16), wr_ref[...], preferred_element_type=F32)
        lane = lax.broadcasted_iota(jnp.int32, logits.shape, 1)
        logits = jnp.where(lane < N_EXPERTS, logits, -1e30)
        e = jnp.exp(logits - jnp.max(logits, axis=-1, keepdims=True))
        aff_ref[rs, :] = e / jnp.sum(e, axis=-1, keepdims=True)


def _mix_out(merged, w_o, x2d, mod3, npost, npre, w_router_pad, length, tm):
    m, d = merged.shape
    nt = length // tm
    row = lambda kind: (lambda i: ((i // nt) * 6 + kind, 0, 0))
    return pl.pallas_call(
        _mix_out_kernel,
        grid=(m // tm,),
        in_specs=[pl.BlockSpec((tm, d), lambda i: (i, 0)),
                  pl.BlockSpec((d, d), lambda i: (0, 0), pipeline_mode=pl.Buffered(1)),
                  pl.BlockSpec((tm, d), lambda i: (i, 0)),
                  pl.BlockSpec((None, 1, d), row(2)),
                  pl.BlockSpec((None, 1, d), row(4)),
                  pl.BlockSpec((None, 1, d), row(3)),
                  pl.BlockSpec((1, d), lambda i: (0, 0)),
                  pl.BlockSpec((1, d), lambda i: (0, 0)),
                  pl.BlockSpec((d, LANES), lambda i: (0, 0))],
        out_specs=[pl.BlockSpec((tm, d), lambda i: (i, 0)),
                   pl.BlockSpec((tm, d), lambda i: (i, 0)),
                   pl.BlockSpec((tm, LANES), lambda i: (i, 0))],
        out_shape=[jax.ShapeDtypeStruct((m, d), F32),
                   jax.ShapeDtypeStruct((m, d), F32),
                   jax.ShapeDtypeStruct((m, LANES), F32)],
        compiler_params=_params(1),
        name="mix_out_router",
    )(merged, w_o, x2d, mod3, mod3, mod3, npost, npre, w_router_pad)


def _route_select(cap, aff_ref, triu_ref, low_ref, pos_ref, slot_ref, ts_ref):
    aff = aff_ref[...]
    e_cnt, n_tok = aff.shape

    def count(m):
        return jnp.sum(jnp.where(m, 1.0, 0.0), axis=1, keepdims=True)

    def search_bits(_, carry):
        lo, hi = carry
        mid = lo + ((hi - lo + 1) >> 1)
        ok = count(aff >= pltpu.bitcast(mid, F32)) >= cap
        return jnp.where(ok, mid, lo), jnp.where(ok, hi, mid - 1)

    lo0 = jnp.zeros((e_cnt, 1), jnp.int32)
    hi0 = jnp.full((e_cnt, 1), 0x7F800000, jnp.int32)
    lo_b, _ = lax.fori_loop(0, 32, search_bits, (lo0, hi0))

    def search_mid(_, carry):
        lo, hi = carry
        mid = 0.5 * (lo + hi)
        ok = count(aff >= mid) >= cap
        return jnp.where(ok, mid, lo), jnp.where(ok, hi, mid)

    lo_f, _ = lax.fori_loop(0, 24, search_mid, (pltpu.bitcast(lo_b, F32), pltpu.bitcast(lo_b + 1, F32)))
    thr = jnp.min(jnp.where(aff >= lo_f, aff, jnp.inf), axis=1, keepdims=True)
    need_f = cap - count(aff > thr)
    triu = triu_ref[...]
    carry_eq = jnp.zeros((e_cnt, 1), F32)
    carry_sel = jnp.zeros((e_cnt, 1), F32)
    for k in range(n_tok // LANES):
        sl = slice(k * LANES, (k + 1) * LANES)
        aff_k = aff[:, sl]
        eq_f = jnp.where(aff_k == thr, 1.0, 0.0)
        eq_incl = jnp.dot(eq_f.astype(BF16), triu, preferred_element_type=F32) + carry_eq
        sel_k = (aff_k > thr) | ((aff_k == thr) & (eq_incl <= need_f))
        sel_f = jnp.where(sel_k, 1.0, 0.0)
        sel_b = sel_f.astype(BF16)
        sel_incl = jnp.dot(sel_b, triu, preferred_element_type=F32) + carry_sel
        excl = sel_incl - sel_f
        pos_ref[:, sl] = jnp.where(sel_k, excl.astype(jnp.int32), -1)
        tok_start = jnp.sum(excl, axis=0, keepdims=True)
        ts_ref[:, sl] = tok_start
        slot_ref[:, sl] = tok_start + jnp.dot(low_ref[...], sel_b, preferred_element_type=F32)
        carry_eq = eq_incl[:, LANES - 1:LANES]
        carry_sel = sel_incl[:, LANES - 1:LANES]


def _route_kernel(cap, aff_ref, triu_ref, low_ref, list_ref, ts_ref, pos_ref, slot_ref):
    e = pl.program_id(1)

    @pl.when(e == 0)
    def _():
        _route_select(cap, aff_ref, triu_ref, low_ref, pos_ref, slot_ref, ts_ref)

    n_tok = aff_ref.shape[1]
    pos = pos_ref[pl.ds(e, 1), :]
    slot = slot_ref[pl.ds(e, 1), :]
    aff = aff_ref[pl.ds(e, 1), :]
    hit = lax.broadcasted_iota(jnp.int32, (cap, n_tok), 0) == pos
    tok = lax.broadcasted_iota(jnp.int32, (1, n_tok), 1)
    slot_hi = jnp.floor(slot * (1.0 / LANES))
    a_hi = aff.astype(BF16).astype(F32)
    a_mid = (aff - a_hi).astype(BF16).astype(F32)
    rows = [(tok >> 6).astype(F32), (tok & 63).astype(F32), slot_hi, slot - slot_hi * LANES,
            a_hi, a_mid, aff - a_hi - a_mid]
    sub = lax.broadcasted_iota(jnp.int32, (8, n_tok), 0)
    vals = jnp.zeros((8, n_tok), F32)
    for i, r in enumerate(rows):
        vals = jnp.where(sub == i, r, vals)
    lists = lax.dot_general(jnp.where(hit, 1.0, 0.0).astype(BF16), vals.astype(BF16),
                            (((1,), (1,)), ((), ())), preferred_element_type=F32)
    eye = jnp.where(lax.broadcasted_iota(jnp.int32, (8, 8), 0) == lax.broadcasted_iota(jnp.int32, (8, 8), 1),
                    1.0, 0.0).astype(BF16)
    list_ref[...] = lax.dot_general(eye, lists.astype(BF16), (((1,), (1,)), ((), ())),
                                    preferred_element_type=F32)


def _route(aff_t, triu, low, cap):
    bsz, e_cnt, n_tok = aff_t.shape
    return pl.pallas_call(
        functools.partial(_route_kernel, cap),
        grid=(bsz, e_cnt),
        in_specs=[pl.BlockSpec((None, e_cnt, n_tok), lambda b, e: (b, 0, 0)),
                  pl.BlockSpec((LANES, LANES), lambda b, e: (0, 0)),
                  pl.BlockSpec((e_cnt, e_cnt), lambda b, e: (0, 0))],
        out_specs=[pl.BlockSpec((None, None, 8, cap), lambda b, e: (b, e, 0, 0)),
                   pl.BlockSpec((None, 1, n_tok), lambda b, e: (b, 0, 0))],
        out_shape=[jax.ShapeDtypeStruct((bsz, e_cnt, 8, cap), F32),
                   jax.ShapeDtypeStruct((bsz, 1, n_tok), F32)],
        scratch_shapes=[pltpu.VMEM((e_cnt, n_tok), jnp.int32), pltpu.VMEM((e_cnt, n_tok), F32)],
        compiler_params=_params(2),
        name="route_topk",
    )(aff_t, triu, low)


def _expert_kernel(cap, n_e, n_f, n_pairs, tf, idx_ref, dst_ref, h_hbm, w1_hbm, w3_hbm, w2_hbm, val_ref, z_hbm,
                   gbuf, obuf, xg_ref, acc_ref, w1_buf, w3_buf, w2_buf, gsem, ssem, wsem):
    e, f, j = pl.program_id(1), pl.program_id(2), pl.program_id(3)
    pair = pl.program_id(0) * n_e + e
    nxt = jnp.minimum(pair + 1, n_pairs - 1)
    d = h_hbm.shape[1]
    rows = cap // n_f
    s = 2 * f + j
    part = lax.rem(s, n_f)
    wslot = lax.rem(pair * n_f + f, 2)

    def weight_copies(ee, ff, slot):
        col = pl.multiple_of(ff * tf, tf)
        return (pltpu.make_async_copy(w1_hbm.at[ee, :, pl.ds(col, tf)], w1_buf.at[slot], wsem.at[slot, 0]),
                pltpu.make_async_copy(w3_hbm.at[ee, :, pl.ds(col, tf)], w3_buf.at[slot], wsem.at[slot, 1]),
                pltpu.make_async_copy(w2_hbm.at[ee, pl.ds(col, tf), :], w2_buf.at[slot], wsem.at[slot, 2]))

    def gather(p, t, q, r):
        src = idx_ref[(2 * p + t) * cap + q * rows + r]
        return pltpu.make_async_copy(h_hbm.at[pl.ds(src, 1), :], gbuf.at[t, q, pl.ds(r, 1), :], gsem.at[0])

    def scatter(pos, t, q, r):
        dst = dst_ref[(2 * pos + t) * cap + q * rows + r]
        return pltpu.make_async_copy(obuf.at[t, q, pl.ds(r, 1), :], z_hbm.at[pl.ds(dst, 1), :], ssem.at[t])

    def for_all_rows(fn):
        def body(r, carry):
            for t in range(2):
                for q in range(n_f):
                    fn(t, q, r)
            return carry

        lax.fori_loop(0, rows, body, 0)

    @pl.when(jnp.logical_and(pair == 0, s == 0))
    def _():
        for c in weight_copies(e, f, wslot):
            c.start()
        obuf[...] = jnp.zeros_like(obuf)
        for_all_rows(lambda t, q, r: gather(0, t, q, r).start())

    @pl.when(j == 0)
    def _():
        last_f = f == n_f - 1
        e_nxt = jnp.where(last_f, jnp.where(e == n_e - 1, 0, e + 1), e)
        f_nxt = jnp.where(last_f, 0, f + 1)

        @pl.when(jnp.logical_not(jnp.logical_and(pair == n_pairs - 1, last_f)))
        def _():
            for c in weight_copies(e_nxt, f_nxt, 1 - wslot):
                c.start()

        for c in weight_copies(e, f, wslot):
            c.wait()

    @pl.when(s == 0)
    def _():
        pltpu.make_async_copy(gbuf, gbuf, gsem.at[0]).wait()
        for t in range(2):
            for q in range(n_f):
                xg_ref[t, q * rows:(q + 1) * rows, :] = gbuf[t, q].astype(BF16)

    @pl.when(f == 0)
    def _():
        acc_ref[j] = jnp.zeros((cap, d), F32)

    @pl.when(s < n_f)
    def _():
        for t in range(2):
            for r in range(rows):
                scatter(pair, t, part, r).start(priority=r % 2)

    @pl.when(s >= n_f)
    def _():
        for t in range(2):
            for r in range(rows):
                gather(nxt, t, part, r).start(priority=r % 2)

    xg = xg_ref[j]
    hs = []
    for c in range(tf // MXU_COLS):
        sl = slice(c * MXU_COLS, (c + 1) * MXU_COLS)
        g = jnp.dot(xg, w1_buf[wslot, :, sl].astype(BF16), preferred_element_type=F32)
        u = jnp.dot(xg, w3_buf[wslot, :, sl].astype(BF16), preferred_element_type=F32)
        hs.append((_silu(g) * u).astype(BF16))
    h = jnp.concatenate(hs, axis=1)
    for c in range(d // MXU_COLS):
        sl = slice(c * MXU_COLS, (c + 1) * MXU_COLS)
        acc_ref[j, :, sl] += jnp.dot(h, w2_buf[wslot, :, sl].astype(BF16), preferred_element_type=F32)

    @pl.when(f == n_f - 1)
    def _():
        pltpu.make_async_copy(obuf.at[j], obuf.at[j], ssem.at[j]).wait()
        for q in range(n_f):
            sl = slice(q * rows, (q + 1) * rows)
            obuf[j, q] = acc_ref[j, sl, :] * val_ref[sl, :]

        @pl.when(jnp.logical_and(pair == n_pairs - 1, j == 1))
        def _():
            for_all_rows(lambda t, q, r: scatter(n_pairs, t, q, r).start())
            for t in range(2):
                pltpu.make_async_copy(obuf.at[t], obuf.at[t], ssem.at[t]).wait()
            pltpu.make_async_copy(gbuf, gbuf, gsem.at[0]).wait()


def _experts(idx, dst_ext, h2, w1, w3, w2, vals, n_rows_out, tf):
    bsz, e_cnt, cap, _ = vals.shape
    d = h2.shape[-1]
    n_f = w1.shape[-1] // tf
    n_pairs = bsz // 2 * e_cnt
    grid_spec = pltpu.PrefetchScalarGridSpec(
        num_scalar_prefetch=2,
        grid=(bsz // 2, e_cnt, n_f, 2),
        in_specs=[pl.BlockSpec(memory_space=pl.ANY),
                  pl.BlockSpec(memory_space=pl.ANY),
                  pl.BlockSpec(memory_space=pl.ANY),
                  pl.BlockSpec(memory_space=pl.ANY),
                  pl.BlockSpec((None, None, cap, 1), lambda bh, e, f, j, *_: (2 * bh + j, e, 0, 0))],
        out_specs=pl.BlockSpec(memory_space=pl.ANY),
        scratch_shapes=[pltpu.VMEM((2, n_f, cap // n_f, d), F32), pltpu.VMEM((2, n_f, cap // n_f, d), F32),
                        pltpu.VMEM((2, cap, d), BF16), pltpu.VMEM((2, cap, d), F32),
                        pltpu.VMEM((2, d, tf), F32), pltpu.VMEM((2, d, tf), F32), pltpu.VMEM((2, tf, d), F32),
                        pltpu.SemaphoreType.DMA((1,)), pltpu.SemaphoreType.DMA((2,)),
                        pltpu.SemaphoreType.DMA((2, 3))],
    )
    return pl.pallas_call(
        functools.partial(_expert_kernel, cap, e_cnt, n_f, n_pairs, tf),
        grid_spec=grid_spec,
        out_shape=jax.ShapeDtypeStruct((n_rows_out, d), F32),
        compiler_params=_params(4),
        name="moe_experts",
    )(idx.reshape(-1), dst_ext.reshape(-1), h2, w1, w3, w2, vals)


COMBINE_ROWS = 256


COMBINE_SLOTS = 4


def _combine_kernel(n_slot, nb, nt, starts_ref, ts_ref, x1_ref, g_ref, w_ref, z_hbm, o_ref, zbuf, acc_ref, sem,
                    st_ref):
    b, t = pl.program_id(0), pl.program_id(1)
    shift = COMBINE_ROWS.bit_length() - 1

    def tile_chunks(bb, tt):
        s0 = starts_ref[bb * (nt + 1) + tt]
        s1 = starts_ref[bb * (nt + 1) + tt + 1]
        start = lax.shift_left(lax.shift_right_logical(s0, 3), 3)
        return start, jnp.maximum(lax.shift_right_logical(s1 - start + COMBINE_ROWS - 1, shift), 1)

    def copy(row, slot):
        return pltpu.make_async_copy(z_hbm.at[pl.ds(row, COMBINE_ROWS), :], zbuf.at[slot], sem.at[slot])

    def fetch_one():
        tile = st_ref[2]

        @pl.when(tile < nb * nt)
        def _():
            bb = lax.div(tile, nt)
            start, n = tile_chunks(bb, tile - bb * nt)
            c, k = st_ref[3], st_ref[1]
            copy(pl.multiple_of(bb * n_slot + start + c * COMBINE_ROWS, 8), lax.rem(k, COMBINE_SLOTS)).start()
            st_ref[1] = k + 1
            tile_done = c + 1 == n
            st_ref[2] = jnp.where(tile_done, tile + 1, tile)
            st_ref[3] = jnp.where(tile_done, 0, c + 1)

    @pl.when(jnp.logical_and(b == 0, t == 0))
    def _():
        for i in range(4):
            st_ref[i] = 0
        for _ in range(COMBINE_SLOTS - 1):
            fetch_one()

    start0, n_chunks = tile_chunks(b, t)
    acc_ref[...] = jnp.zeros_like(acc_ref)
    first = ts_ref[:, 0:1]
    last = ts_ref[:, 1:2]

    def body(c, carry):
        k = st_ref[0]
        slot = lax.rem(k, COMBINE_SLOTS)
        copy(0, slot).wait()
        st_ref[0] = k + 1
        fetch_one()
        rows = zbuf[slot]
        j = (start0 + c * COMBINE_ROWS + lax.broadcasted_iota(jnp.int32, (1, COMBINE_ROWS), 1)).astype(F32)
        seg = jnp.where(j >= first, jnp.where(j < last, 1.0, 0.0), 0.0).astype(BF16)
        hi = rows.astype(BF16)
        lo = (rows - hi.astype(F32)).astype(BF16)
        acc_ref[...] += (jnp.dot(seg, hi, preferred_element_type=F32)
                         + jnp.dot(seg, lo, preferred_element_type=F32))
        return carry

    lax.fori_loop(0, n_chunks, body, 0)
    o_ref[...] = x1_ref[...] + g_ref[...] * (_rms(acc_ref[...]) * w_ref[...])


def _combine(starts, ts2, z, x1, mod3, npost_ffn, n_slot, tm):
    bsz, n_tok, d = x1.shape
    grid_spec = pltpu.PrefetchScalarGridSpec(
        num_scalar_prefetch=1,
        grid=(bsz, n_tok // tm),
        in_specs=[pl.BlockSpec((None, tm, 2), lambda b, t, *_: (b, t, 0)),
                  pl.BlockSpec((None, tm, d), lambda b, t, *_: (b, t, 0)),
                  pl.BlockSpec((None, 1, d), lambda b, t, *_: (b * 6 + 5, 0, 0)),
                  pl.BlockSpec((1, d), lambda b, t, *_: (0, 0)),
                  pl.BlockSpec(memory_space=pl.ANY)],
        out_specs=pl.BlockSpec((None, tm, d), lambda b, t, *_: (b, t, 0)),
        scratch_shapes=[pltpu.VMEM((COMBINE_SLOTS, COMBINE_ROWS, d), F32), pltpu.VMEM((tm, d), F32),
                        pltpu.SemaphoreType.DMA((COMBINE_SLOTS,)), pltpu.SMEM((4,), jnp.int32)],
    )
    return pl.pallas_call(
        functools.partial(_combine_kernel, n_slot, bsz, n_tok // tm),
        grid_spec=grid_spec,
        out_shape=jax.ShapeDtypeStruct((bsz, n_tok, d), F32),
        compiler_params=_params(2),
        name="moe_combine",
    )(starts, ts2, x1, mod3, npost_ffn, z)


def _ssd_inputs(h, w_xs, w_bc, w_dt, conv_w, conv_b, dt_bias_pad, seg, tm):
    cw_x, cw_bc = conv_w[:, :SSD_INNER], conv_w[:, SSD_INNER:]
    cb_x, cb_bc = conv_b[:, :SSD_INNER], conv_b[:, SSD_INNER:]
    epi = functools.partial(_epi_conv_silu, seg)
    (xs,) = _fused_matmul([h], [w_xs], [0], [cw_x, cb_x], [], epi, [F32], SSD_INNER, tm, 1024, "proj_xs")
    (bc,) = _fused_matmul([h], [w_bc], [0], [cw_bc, cb_bc], [], epi, [BF16],
                          2 * SSD_GROUPS * SSD_STATE, tm, 1024, "proj_bc")
    (dt,) = _fused_matmul([h], [w_dt], [0], [dt_bias_pad], [], _epi_softplus, [F32], LANES, tm, LANES,
                          "proj_dt")
    return xs, bc, dt


def kernel(x, c, ctx, c_ctx, w_mod, b_mod, norm_pre_mix, norm_post_mix, w_in, ssd_conv_w, ssd_conv_b,
           dt_bias, a_log, ssd_d, ssd_norm_w, w_ssd_out, sc_conv_w, w_sc_out, b_gate, w_o, norm_pre_ffn,
           norm_post_ffn, w_router, w_e1, w_e3, w_e2):
    assert w_mod.shape[0] == 1, "single-layer block"
    bsz, length, d = x.shape
    ctx_len = ctx.shape[1]
    m = bsz * length
    cap = CAPACITY_FACTOR * length // N_EXPERTS
    row = lambda v: v.reshape(1, -1)

    o_z, o_xbc, o_dt, o_scb, o_scc, o_scv = (SSD_INNER, SSD_INNER + SSD_XBC,
                                             SSD_INNER + SSD_XBC + 2 * SSD_HEADS,
                                             SSD_INNER + SSD_XBC + 2 * SSD_HEADS + SC_DIM,
                                             SSD_INNER + SSD_XBC + 2 * SSD_HEADS + 2 * SC_DIM,
                                             SSD_INNER + SSD_XBC + 2 * SSD_HEADS + 3 * SC_DIM)
    w_in0 = w_in[0].astype(BF16)
    w_z, w_xs, w_bc, w_dt = (w_in0, 0), (w_in0, o_z), (w_in0, o_z + SSD_INNER), (w_in0, o_xbc)
    w_rest = w_in0[:, o_dt:]
    w_scb, w_scc, w_scv, w_gate = ((w_rest, 0), (w_rest, o_scb - o_dt), (w_rest, o_scc - o_dt),
                                   (w_rest, o_scv - o_dt))
    dt_bias_pad = jnp.pad(dt_bias[0].reshape(1, -1), ((0, 0), (0, LANES - 2 * SSD_HEADS)))
    a_neg = -jnp.exp(a_log[0])
    ii = lax.broadcasted_iota(jnp.int32, (CHUNK, CHUNK), 0)
    jj = lax.broadcasted_iota(jnp.int32, (CHUNK, CHUNK), 1)
    tri = jnp.stack([ii >= jj, ii <= jj]).astype(BF16)
    conv_w, conv_b = ssd_conv_w[0], row(ssd_conv_b[0])

    cs = jnp.concatenate([c, c_ctx[None, :], jnp.zeros((8 - bsz - 1, d), F32)], axis=0)
    mod3 = _modulation(cs, w_mod[0], b_mod[0]).reshape(8 * 6, 1, d)

    hc = _norm_mod(ctx, norm_pre_mix[0], mod3, lambda b: bsz * 6, lambda b: bsz * 6 + 1, ctx_len)
    xs_c, bc_c, dt_c = _ssd_inputs(hc.reshape(bsz * ctx_len, d), w_xs, w_bc, w_dt, conv_w, conv_b,
                                   dt_bias_pad, ctx_len, ctx_len)
    h_zero = jnp.zeros((2, bsz, SSD_GROUPS, SSD_STATE, GROUP_W), F32)
    _, states = _ssd_scan(xs_c.reshape(bsz, ctx_len, -1), bc_c.reshape(bsz, ctx_len, -1),
                          dt_c.reshape(bsz, ctx_len, -1), a_neg, tri, h_zero)

    tm = 1024
    hx = _norm_mod(x, norm_pre_mix[0], mod3, lambda b: b * 6, lambda b: b * 6 + 1, tm).reshape(m, d)
    (z,) = _fused_matmul([hx], [w_z], [0], [], [], _epi_identity, [F32], SSD_INNER, tm, 1024, "proj_z")
    xs, bc, dt = _ssd_inputs(hx, w_xs, w_bc, w_dt, conv_w, conv_b, dt_bias_pad, GRID_W, tm)
    (sc_y,) = _fused_matmul([hx], [w_scb, w_scc, w_scv], [0, 0, 0], [sc_conv_w[0]], [],
                            functools.partial(_epi_shortconv, GRID_W), [BF16], SC_DIM, tm, 512, "proj_sc")
    (gates,) = _fused_matmul([hx], [w_gate], [0], [row(b_gate[0])], [], _epi_sigmoid, [F32],
                             2 * d, tm, 1024, "proj_gate")
    d_cols = jnp.repeat(ssd_d[0], SSD_HEAD_DIM).reshape(1, -1)
    y_norm = _ssd_scan(xs.reshape(bsz, length, -1), bc.reshape(bsz, length, -1), dt.reshape(bsz, length, -1),
                       a_neg, tri, states, post=(z.reshape(bsz, length, -1), d_cols, row(ssd_norm_w[0])))
    y_norm = y_norm.reshape(m, -1)
    tn = 512
    (merged,) = _fused_matmul([y_norm, sc_y], [(w_ssd_out[0].astype(BF16), 0), (w_sc_out[0].astype(BF16), 0)],
                              [0, 1], [], [(gates, 0), (gates, d // tn)], _epi_gate_merge, [BF16], d, tm, tn,
                              "mix_merge")
    w_router_pad = jnp.pad(w_router[0], ((0, 0), (0, LANES - N_EXPERTS))).astype(BF16)
    x1, h2, aff = _mix_out(merged, w_o[0].astype(BF16), x.reshape(m, d), mod3, row(norm_post_mix[0]),
                           row(norm_pre_ffn[0]), w_router_pad, length, 512)

    aff_t = jnp.transpose(aff[:, :N_EXPERTS].reshape(bsz, length, N_EXPERTS), (0, 2, 1))
    e_i = lax.broadcasted_iota(jnp.int32, (N_EXPERTS, N_EXPERTS), 0)
    e_j = lax.broadcasted_iota(jnp.int32, (N_EXPERTS, N_EXPERTS), 1)
    lists, ts = _route(aff_t, tri[1], (e_j < e_i).astype(BF16), cap)
    n_slot = N_EXPERTS * cap
    li = lists[:, :, :4, :].astype(jnp.int32)
    b_i = jnp.arange(bsz, dtype=jnp.int32)[:, None, None]

    def pair_order(a):
        return jnp.transpose(a.reshape(bsz // 2, 2, N_EXPERTS, cap), (0, 2, 1, 3)).reshape(-1, cap)

    idx = pair_order(li[:, :, 0] * 64 + li[:, :, 1] + b_i * length)
    dst = pair_order(li[:, :, 2] * LANES + li[:, :, 3] + b_i * n_slot)
    n_pad = 2 * cap
    dst_ext = jnp.concatenate([bsz * n_slot + jnp.arange(n_pad, dtype=jnp.int32).reshape(2, cap), dst], axis=0)
    vals = (lists[:, :, 4] + lists[:, :, 5] + lists[:, :, 6])[..., None]
    z = _experts(idx, dst_ext, h2, w_e1[0], w_e3[0], w_e2[0], vals, bsz * n_slot + n_pad, MXU_COLS)
    tm_c = 256
    ts = ts.reshape(bsz, length)
    end = jnp.full((bsz, 1), n_slot, F32)
    ts2 = jnp.stack([ts, jnp.concatenate([ts[:, 1:], end], axis=1)], axis=-1)
    starts = jnp.concatenate([ts[:, ::tm_c], end], axis=1).astype(jnp.int32).reshape(-1)
    return _combine(starts, ts2, z, x1.reshape(bsz, length, d), mod3, row(norm_post_ffn[0]), n_slot, tm_c)
```
